```python
import math
import jax, jax.numpy as jnp
from jax import lax
import numpy as np

D_MODEL = 1024
BATCH = 8
SEQ = 2048
DEPTH = 4
DEC_BATCH = 128
DEC_SEQ = 1
PAST_LEN = 16384
PAGE_SIZE = 128

N_MIXERS = 3
LAYER_MIXER = tuple(i % N_MIXERS for i in range(DEPTH))
N_RET_LAYERS = LAYER_MIXER.count(0)
N_SSD_LAYERS = LAYER_MIXER.count(1)
N_HG_LAYERS = LAYER_MIXER.count(2)
EPS = 1e-6

RET_HEADS = D_MODEL // 256
RET_QK_DIM = D_MODEL // RET_HEADS
RET_V_DIM = 2 * RET_QK_DIM
RET_CHUNK = 128
ROPE_BASE = 10000.0

SSD_D_INNER = 2 * D_MODEL
SSD_HEAD_DIM = 64
SSD_HEADS = SSD_D_INNER // SSD_HEAD_DIM
SSD_GROUPS = 4
SSD_STATE = 128
SSD_CONV = 4
SSD_CONV_DIM = SSD_D_INNER + 2 * SSD_GROUPS * SSD_STATE
SSD_CHUNK = 128
DT_MIN = 0.001
DT_MAX = 0.1

HG_KEY_DIM = 128
HG_HEADS = D_MODEL // HG_KEY_DIM
HG_VAL_DIM = D_MODEL // HG_HEADS
HG_CHUNK = 64

D_FF = -(-8 * D_MODEL // (3 * 256)) * 256

kernel_name = "hybrid_ret_ssd_hgrn2_adaln_step"


def _rmsnorm(x, w=None):
    y = x * lax.rsqrt(jnp.mean(x * x, axis=-1, keepdims=True) + EPS)
    return y if w is None else y * w


def _rotary(x, pos):
    half = x.shape[-1] // 2
    inv = 1.0 / (ROPE_BASE ** (jnp.arange(half, dtype=jnp.float32) / half))
    ang = pos.astype(jnp.float32)[:, None] * inv[None, :]
    cos = jnp.cos(ang)[None, :, None, :]
    sin = jnp.sin(ang)[None, :, None, :]
    x1, x2 = x[..., :half], x[..., half:]
    return jnp.concatenate([x1 * cos - x2 * sin, x2 * cos + x1 * sin], axis=-1)


def _chunk_recurrence(q, k, v, log_a, h0, chunk):
    B, L = q.shape[0], q.shape[1]
    c = min(chunk, L)
    n = -(-L // c)
    pad = n * c - L

    def prep(t):
        t = t.astype(jnp.float32)
        if pad:
            t = jnp.pad(t, [(0, 0), (0, pad)] + [(0, 0)] * (t.ndim - 2))
        return jnp.moveaxis(t.reshape((B, n, c) + t.shape[2:]), 1, 0)

    qs, ks, vs, las = prep(q), prep(k), prep(v), prep(log_a)
    mask = jnp.tril(jnp.ones((c, c), dtype=bool))
    per_channel = log_a.ndim == 4

    def step(S, blk):
        qc, kc, vc, ac = blk
        cum = jnp.cumsum(ac, axis=1)
        tot = cum[:, -1]
        seg = cum[:, :, None] - cum[:, None]
        if per_channel:
            dec = jnp.exp(jnp.where(mask[None, :, :, None, None], seg, -jnp.inf))
            scores = jnp.einsum('bihk,bjhk,bijhk->bijh', qc, kc, dec)
            q_in = qc * jnp.exp(cum)
            k_out = kc * jnp.exp(tot[:, None] - cum)
            s_dec = jnp.exp(tot)[..., None]
        else:
            dec = jnp.exp(jnp.where(mask[None, :, :, None], seg, -jnp.inf))
            scores = jnp.einsum('bihk,bjhk->bijh', qc, kc) * dec
            q_in = qc * jnp.exp(cum)[..., None]
            k_out = kc * jnp.exp(tot[:, None] - cum)[..., None]
            s_dec = jnp.exp(tot)[..., None, None]
        o = jnp.einsum('bijh,bjhv->bihv', scores, vc) + jnp.einsum('bihk,bhkv->bihv', q_in, S)
        S = S * s_dec + jnp.einsum('bjhk,bjhv->bhkv', k_out, vc)
        return S, o

    S, o = lax.scan(step, h0.astype(jnp.float32), (qs, ks, vs, las))
    o = jnp.moveaxis(o, 0, 1).reshape((B, n * c) + o.shape[3:])[:, :L]
    return o, S


def _retention(h, pos, state, w_in, w_out):
    B, L, _ = h.shape
    qk = RET_HEADS * RET_QK_DIM
    vd = RET_HEADS * RET_V_DIM
    q, k, v, g = jnp.split(h @ w_in, [qk, 2 * qk, 2 * qk + vd], axis=-1)
    q = _rotary(q.reshape(B, L, RET_HEADS, RET_QK_DIM), pos)
    k = _rotary(k.reshape(B, L, RET_HEADS, RET_QK_DIM), pos) * (RET_QK_DIM ** -0.5)
    v = v.reshape(B, L, RET_HEADS, RET_V_DIM)
    log_gamma = jnp.log1p(-jnp.exp2(-5.0 - jnp.arange(RET_HEADS, dtype=jnp.float32)))
    log_a = jnp.broadcast_to(log_gamma, (B, L, RET_HEADS))
    o, new_state = _chunk_recurrence(q, k, v, log_a, state, RET_CHUNK)
    o = _rmsnorm(o).reshape(B, L, vd)
    return (jax.nn.silu(g) * o) @ w_out, new_state


def _ssd(h, state, conv_buf, w_in, conv_w, conv_b, dt_bias, a_log, d_skip, norm_w, w_out):
    B, L, _ = h.shape
    z, xbc, dt = jnp.split(h @ w_in, [SSD_D_INNER, SSD_D_INNER + SSD_CONV_DIM], axis=-1)
    xpad = jnp.concatenate([conv_buf.astype(jnp.float32), xbc], axis=1)
    conv = conv_b + sum(xpad[:, w:w + L] * conv_w[w] for w in range(SSD_CONV))
    new_buf = xpad[:, L:]
    xbc = jax.nn.silu(conv)
    gn = SSD_GROUPS * SSD_STATE
    xs, bm, cm = jnp.split(xbc, [SSD_D_INNER, SSD_D_INNER + gn], axis=-1)
    dt = jax.nn.softplus(dt + dt_bias)
    a = -jnp.exp(a_log.astype(jnp.float32))
    xs = xs.reshape(B, L, SSD_HEADS, SSD_HEAD_DIM)
    rep = SSD_HEADS // SSD_GROUPS
    bm = jnp.repeat(bm.reshape(B, L, SSD_GROUPS, SSD_STATE), rep, axis=2)
    cm = jnp.repeat(cm.reshape(B, L, SSD_GROUPS, SSD_STATE), rep, axis=2)
    y, new_state = _chunk_recurrence(cm, bm, xs * dt[..., None], dt * a, state, SSD_CHUNK)
    y = y + d_skip[:, None] * xs
    y = y.reshape(B, L, SSD_D_INNER) * jax.nn.silu(z)
    y = _rmsnorm(y.reshape(B, L, SSD_GROUPS, SSD_D_INNER // SSD_GROUPS)).reshape(B, L, SSD_D_INNER) * norm_w
    return y @ w_out, new_state, new_buf


def _hgrn2(h, state, lb, w_in, norm_w, w_out):
    B, L, _ = h.shape
    kd = HG_HEADS * HG_KEY_DIM
    vd = HG_HEADS * HG_VAL_DIM
    q, f, i, g = jnp.split(h @ w_in, [kd, 2 * kd, 2 * kd + vd], axis=-1)
    q = jax.nn.silu(q).reshape(B, L, HG_HEADS, HG_KEY_DIM) * (HG_KEY_DIM ** -0.5)
    f = lb + (1.0 - lb) * jax.nn.sigmoid(f)
    k = (1.0 - f).reshape(B, L, HG_HEADS, HG_KEY_DIM)
    log_a = jnp.log(f).reshape(B, L, HG_HEADS, HG_KEY_DIM)
    v = i.reshape(B, L, HG_HEADS, HG_VAL_DIM)
    o, new_state = _chunk_recurrence(q, k, v, log_a, state, HG_CHUNK)
    o = _rmsnorm(o, norm_w).reshape(B, L, vd) * jax.nn.silu(g)
    return o @ w_out, new_state


def _swiglu(h, w_in, w_out):
    gate, up = jnp.split(h @ w_in, 2, axis=-1)
    return (jax.nn.silu(gate) * up) @ w_out


def _trunk(x, c, pos0, h_ret, h_ssd, conv_buf, h_hg, p):
    B, L, _ = x.shape
    x = x.astype(jnp.float32)
    pos = pos0 + jnp.arange(L, dtype=jnp.int32)
    c_act = jax.nn.silu(c.astype(jnp.float32))
    lb = jnp.cumsum(jax.nn.softmax(p['hg_lb_logits'].astype(jnp.float32), axis=0), axis=0)
    lb = lb - lb[:1]
    new_ret, new_ssd, new_conv, new_hg = [], [], [], []
    for layer in range(DEPTH):
        mod = c_act @ p['w_ada'][layer] + p['b_ada'][layer]
        sh_m, sc_m, g_m, sh_f, sc_f, g_f = [m[:, None, :] for m in jnp.split(mod, 6, axis=-1)]
        h = _rmsnorm(x, p['norm_mix_pre'][layer]) * (1.0 + sc_m) + sh_m
        kind = LAYER_MIXER[layer]
        j = LAYER_MIXER[:layer].count(kind)
        if kind == 0:
            out, s = _retention(h, pos, h_ret[j], p['ret_w_in'][j], p['ret_w_out'][j])
            new_ret.append(s)
        elif kind == 1:
            out, s, buf = _ssd(h, h_ssd[j], conv_buf[j], p['ssd_w_in'][j], p['ssd_conv_w'][j],
                               p['ssd_conv_b'][j], p['ssd_dt_bias'][j], p['ssd_a_log'][j],
                               p['ssd_d'][j], p['ssd_norm'][j], p['ssd_w_out'][j])
            new_ssd.append(s)
            new_conv.append(buf)
        else:
            out, s = _hgrn2(h, h_hg[j], lb[layer], p['hg_w_in'][j], p['hg_norm'][j], p['hg_w_out'][j])
            new_hg.append(s)
        x = x + g_m * _rmsnorm(out, p['norm_mix_post'][layer])
        h = _rmsnorm(x, p['norm_ffn_pre'][layer]) * (1.0 + sc_f) + sh_f
        x = x + g_f * _rmsnorm(_swiglu(h, p['ffn_w_in'][layer], p['ffn_w_out'][layer]), p['norm_ffn_post'][layer])
    return x, jnp.stack(new_ret), jnp.stack(new_ssd), jnp.stack(new_conv), jnp.stack(new_hg)


def setup_inputs(seed: int = 0) -> dict:
    key = jax.random.key(seed)
    keys = iter(jax.random.split(key, 32))

    def nrm(shape, scale):
        return scale * jax.random.normal(next(keys), shape, jnp.float32)

    def gain(shape):
        return 1.0 + nrm(shape, 0.05)

    D = D_MODEL
    ret_in = 2 * RET_HEADS * RET_QK_DIM + 2 * RET_HEADS * RET_V_DIM
    ssd_in = 2 * SSD_D_INNER + 2 * SSD_GROUPS * SSD_STATE + SSD_HEADS
    hg_in = 2 * HG_HEADS * HG_KEY_DIM + 2 * HG_HEADS * HG_VAL_DIM
    dt = jnp.exp(jax.random.uniform(next(keys), (N_SSD_LAYERS, SSD_HEADS), jnp.float32,
                                    math.log(DT_MIN), math.log(DT_MAX)))
    a_log = jnp.log(jax.random.uniform(next(keys), (N_SSD_LAYERS, SSD_HEADS), jnp.float32, 1.0, 16.0))
    return {
        'x_prompt': nrm((BATCH, SEQ, D), 1.0),
        'x_sample': nrm((DEC_BATCH, DEC_SEQ, D), 1.0),
        'c_prompt': nrm((BATCH, D), 1.0),
        'c_sample': nrm((DEC_BATCH, D), 1.0),
        'state_ret': nrm((N_RET_LAYERS, DEC_BATCH, RET_HEADS, RET_QK_DIM, RET_V_DIM), 0.1),
        'state_ssd': nrm((N_SSD_LAYERS, DEC_BATCH, SSD_HEADS, SSD_STATE, SSD_HEAD_DIM), 0.1),
        'state_conv': nrm((N_SSD_LAYERS, DEC_BATCH, SSD_CONV - 1, SSD_CONV_DIM), 1.0),
        'state_hgrn': nrm((N_HG_LAYERS, DEC_BATCH, HG_HEADS, HG_KEY_DIM, HG_VAL_DIM), 0.3),
        'w_ada': nrm((DEPTH, D, 6 * D), 0.5 * D ** -0.5),
        'b_ada': nrm((DEPTH, 6 * D), 0.1),
        'norm_mix_pre': gain((DEPTH, D)),
        'norm_mix_post': gain((DEPTH, D)),
        'norm_ffn_pre': gain((DEPTH, D)),
        'norm_ffn_post': gain((DEPTH, D)),
        'ret_w_in': nrm((N_RET_LAYERS, D, ret_in), D ** -0.5),
        'ret_w_out': nrm((N_RET_LAYERS, RET_HEADS * RET_V_DIM, D), (RET_HEADS * RET_V_DIM) ** -0.5),
        'ssd_w_in': nrm((N_SSD_LAYERS, D, ssd_in), D ** -0.5),
        'ssd_conv_w': nrm((N_SSD_LAYERS, SSD_CONV, SSD_CONV_DIM), SSD_CONV ** -0.5),
        'ssd_conv_b': nrm((N_SSD_LAYERS, SSD_CONV_DIM), 0.02),
        'ssd_dt_bias': dt + jnp.log(-jnp.expm1(-dt)),
        'ssd_a_log': a_log,
        'ssd_d': gain((N_SSD_LAYERS, SSD_HEADS)),
        'ssd_norm': gain((N_SSD_LAYERS, SSD_D_INNER)),
        'ssd_w_out': nrm((N_SSD_LAYERS, SSD_D_INNER, D), SSD_D_INNER ** -0.5),
        'hg_w_in': nrm((N_HG_LAYERS, D, hg_in), D ** -0.5),
        'hg_lb_logits': nrm((DEPTH, HG_HEADS * HG_KEY_DIM), 1.0),
        'hg_norm': gain((N_HG_LAYERS, HG_VAL_DIM)),
        'hg_w_out': nrm((N_HG_LAYERS, HG_HEADS * HG_VAL_DIM, D), (HG_HEADS * HG_VAL_DIM) ** -0.5),
        'ffn_w_in': nrm((DEPTH, D, 2 * D_FF), D ** -0.5),
        'ffn_w_out': nrm((DEPTH, D_FF, D), D_FF ** -0.5),
    }


def reference(x_prompt, x_sample, c_prompt, c_sample, state_ret, state_ssd, state_conv, state_hgrn,
              w_ada, b_ada, norm_mix_pre, norm_mix_post, norm_ffn_pre, norm_ffn_post,
              ret_w_in, ret_w_out, ssd_w_in, ssd_conv_w, ssd_conv_b, ssd_dt_bias, ssd_a_log,
              ssd_d, ssd_norm, ssd_w_out, hg_w_in, hg_lb_logits, hg_norm, hg_w_out,
              ffn_w_in, ffn_w_out):
    p = dict(w_ada=w_ada, b_ada=b_ada, norm_mix_pre=norm_mix_pre, norm_mix_post=norm_mix_post,
             norm_ffn_pre=norm_ffn_pre, norm_ffn_post=norm_ffn_post,
             ret_w_in=ret_w_in, ret_w_out=ret_w_out,
             ssd_w_in=ssd_w_in, ssd_conv_w=ssd_conv_w, ssd_conv_b=ssd_conv_b,
             ssd_dt_bias=ssd_dt_bias, ssd_a_log=ssd_a_log, ssd_d=ssd_d, ssd_norm=ssd_norm,
             ssd_w_out=ssd_w_out, hg_w_in=hg_w_in, hg_lb_logits=hg_lb_logits, hg_norm=hg_norm,
             hg_w_out=hg_w_out, ffn_w_in=ffn_w_in, ffn_w_out=ffn_w_out)
    bp = x_prompt.shape[0]
    f32 = jnp.float32
    ret0 = jnp.zeros((N_RET_LAYERS, bp, RET_HEADS, RET_QK_DIM, RET_V_DIM), f32)
    ssd0 = jnp.zeros((N_SSD_LAYERS, bp, SSD_HEADS, SSD_STATE, SSD_HEAD_DIM), f32)
    conv0 = jnp.zeros((N_SSD_LAYERS, bp, SSD_CONV - 1, SSD_CONV_DIM), f32)
    hg0 = jnp.zeros((N_HG_LAYERS, bp, HG_HEADS, HG_KEY_DIM, HG_VAL_DIM), f32)
    y_p, ret_p, ssd_p, conv_p, hg_p = _trunk(x_prompt, c_prompt, 0, ret0, ssd0, conv0, hg0, p)
    y_s, ret_s, ssd_s, conv_s, hg_s = _trunk(x_sample, c_sample, PAST_LEN, state_ret, state_ssd,
                                             state_conv, state_hgrn, p)
    return (y_p.astype(x_prompt.dtype), y_s.astype(x_sample.dtype),
            ret_p.astype(state_ret.dtype), ret_s.astype(state_ret.dtype),
            ssd_p.astype(state_ssd.dtype), ssd_s.astype(state_ssd.dtype),
            conv_p.astype(state_conv.dtype), conv_s.astype(state_conv.dtype),
            hg_p.astype(state_hgrn.dtype), hg_s.astype(state_hgrn.dtype))
```

```python
import functools

import numpy as np
import jax
import jax.numpy as jnp
from jax import lax
from jax.experimental import pallas as pl
from jax.experimental.pallas import tpu as pltpu

F32 = jnp.float32
BF16 = jnp.bfloat16
EPS = 1e-6
PAST_LEN = 16384
ROPE_BASE = 10000.0
SSD_HEAD_DIM = 64
SSD_GROUPS = 4
SSD_STATE = 128
SSD_CONV = 4
HG_KEY_DIM = 128
LANES = 128
SUBLANES = 8
VMEM_LIMIT = 56 * 1024 * 1024
TM_PROMPT = 256
C_RET = 256
C_SSD = 128
C_HG = 128
HG_SUB = 16
FFN_CHUNK = 1408


def _rms(x):
    return x * lax.rsqrt(jnp.mean(x * x, axis=-1, keepdims=True) + EPS)


def _silu(x):
    return x * jax.nn.sigmoid(x)


def _softplus(x):
    return jnp.maximum(x, 0.0) + jnp.log1p(jnp.exp(-jnp.abs(x)))


def _dot(a, b):
    return jnp.dot(a, b, preferred_element_type=F32)


def _dot_nt(a, b):
    return lax.dot_general(a, b, (((1,), (1,)), ((), ())), preferred_element_type=F32)


def _dot_tn(a, b):
    return lax.dot_general(a, b, (((0,), (0,)), ((), ())), preferred_element_type=F32)


def _split3(x):
    hi = x.astype(BF16)
    r = x - hi.astype(F32)
    mid = r.astype(BF16)
    lo = (r - mid.astype(F32)).astype(BF16)
    return hi, mid, lo


def _sel_dot(sel, parts):
    acc = _dot(sel, parts[0])
    for p in parts[1:]:
        acc = acc + _dot(sel, p)
    return acc


def _dot_sel(parts, sel):
    acc = _dot(parts[0], sel)
    for p in parts[1:]:
        acc = acc + _dot(p, sel)
    return acc


def _col(row, eye):
    return jnp.sum(jnp.where(eye, row, 0.0), axis=1, keepdims=True)


def _norm_mod(x_ref, sh_ref, sc_ref, nw_ref):
    x = x_ref[0]
    h = _rms(x) * nw_ref[...] * (1.0 + sc_ref[0]) + sh_ref[0]
    return h.astype(BF16)


def _params(sem):
    return pltpu.CompilerParams(dimension_semantics=sem, vmem_limit_bytes=VMEM_LIMIT)


def _const_spec(shape):
    nd = len(shape)
    return pl.BlockSpec(shape, lambda *_: (0,) * nd, pipeline_mode=pl.Buffered(1))


def _ret_gate(o, g, H, V):
    outs = []
    for h in range(H):
        outs.append(_rms(o[:, h * V:(h + 1) * V]) * _silu(g[:, h * V:(h + 1) * V]))
    return outs


def _ssd_gate(y, xs, z, d_rep, nw, groups):
    y = (y + d_rep * xs) * _silu(z)
    w = y.shape[-1] // groups
    outs = []
    for g in range(groups):
        outs.append(_rms(y[:, g * w:(g + 1) * w]) * nw[:, g * w:(g + 1) * w])
    return outs


def _hg_gate(o, g, nw, H, V):
    outs = []
    for h in range(H):
        sl = slice(h * V, (h + 1) * V)
        outs.append(_rms(o[:, sl]) * nw[:, sl] * _silu(g[:, sl]))
    return outs


def _adaln_body(c_ref, w_ref, b_ref, o_ref):
    a = _silu(c_ref[...]).astype(BF16)
    o_ref[0] = _dot(a, w_ref[0].astype(BF16)) + b_ref[0]


def _adaln(c_all, w_ada, b_ada):
    depth, d, n = w_ada.shape
    bc = c_all.shape[0]
    tn = 1024
    return pl.pallas_call(
        _adaln_body,
        grid=(depth, n // tn),
        in_specs=[
            pl.BlockSpec((bc, d), lambda l, j: (0, 0)),
            pl.BlockSpec((1, d, tn), lambda l, j: (l, 0, j)),
            pl.BlockSpec((1, 1, tn), lambda l, j: (l, 0, j)),
        ],
        out_specs=pl.BlockSpec((1, bc, tn), lambda l, j: (l, 0, j)),
        out_shape=jax.ShapeDtypeStruct((depth, bc, n), F32),
        compiler_params=_params(("arbitrary", "arbitrary")),
        name="adaln",
    )(c_all, w_ada, b_ada.reshape(depth, 1, n))


def _k1_specs(x3, mod3, sh_idx, sc_idx, tm):
    _, _, d = x3.shape
    rm = mod3.shape[1]
    return [
        pl.BlockSpec((1, tm, d), lambda g, t: (g, t, 0)),
        pl.BlockSpec((1, rm, d), lambda g, t: (g, 0, sh_idx)),
        pl.BlockSpec((1, rm, d), lambda g, t: (g, 0, sc_idx)),
        _const_spec((1, d)),
    ]


def _row_spec(tm, n):
    return pl.BlockSpec((1, tm, n), lambda g, t: (g, t, 0))


def _k1_ret_body(x_ref, sh_ref, sc_ref, nw_ref, w_ref, cos_ref, sin_ref,
                 q_ref, k_ref, v_ref, g_ref, *, H, K, V):
    hb = _norm_mod(x_ref, sh_ref, sc_ref, nw_ref)
    cos = cos_ref[...]
    sin = sin_ref[...]
    half = K // 2
    for h in range(H):
        for dst, base, scale in ((q_ref, 0, None), (k_ref, H * K, K ** -0.5)):
            a = _dot(hb, w_ref[:, base + h * K:base + (h + 1) * K])
            x1 = a[:, :half]
            x2 = a[:, half:]
            r1 = x1 * cos - x2 * sin
            r2 = x2 * cos + x1 * sin
            if scale is not None:
                r1 = r1 * scale
                r2 = r2 * scale
            dst[0, :, h * K:h * K + half] = r1.astype(dst.dtype)
            dst[0, :, h * K + half:(h + 1) * K] = r2.astype(dst.dtype)
    vb = 2 * H * K
    for c in range(H):
        v_ref[0, :, c * V:(c + 1) * V] = _dot(hb, w_ref[:, vb + c * V:vb + (c + 1) * V]).astype(v_ref.dtype)
        g_ref[0, :, c * V:(c + 1) * V] = _dot(
            hb, w_ref[:, vb + H * V + c * V:vb + H * V + (c + 1) * V]).astype(g_ref.dtype)


def _k1_ret(x3, mod3, nw, w, cos, sin, tm, out_dtype, H, K, V):
    G, R, d = x3.shape
    rc = cos.shape[0]
    cs_spec = (pl.BlockSpec((tm, K // 2), lambda g, t: (t, 0)) if rc == R
               else _const_spec((1, K // 2)))
    outs = [jax.ShapeDtypeStruct((G, R, H * K), out_dtype), jax.ShapeDtypeStruct((G, R, H * K), out_dtype),
            jax.ShapeDtypeStruct((G, R, H * V), out_dtype), jax.ShapeDtypeStruct((G, R, H * V), out_dtype)]
    return pl.pallas_call(
        functools.partial(_k1_ret_body, H=H, K=K, V=V),
        grid=(G, R // tm),
        in_specs=_k1_specs(x3, mod3, 0, 1, tm) + [_const_spec(w.shape), cs_spec, cs_spec],
        out_specs=[_row_spec(tm, H * K), _row_spec(tm, H * K), _row_spec(tm, H * V), _row_spec(tm, H * V)],
        out_shape=outs,
        compiler_params=_params(("arbitrary", "arbitrary")),
        name="ret_inproj",
    )(x3, mod3, mod3, nw, w, cos, sin)


def _k2_ret_body(q_ref, k_ref, v_ref, g_ref, dmat_ref, qdec_ref, kdec_ref, y_ref, s_ref,
                 *, H, K, V, sdec):
    @pl.when(pl.program_id(1) == 0)
    def _():
        s_ref[...] = jnp.zeros_like(s_ref)

    for h in range(H):
        qh = q_ref[0, :, h * K:(h + 1) * K]
        kh = k_ref[0, :, h * K:(h + 1) * K]
        vh = v_ref[0, :, h * V:(h + 1) * V]
        state = s_ref[0, h]
        scores = _dot_nt(qh, kh) * dmat_ref[h]
        o = _dot(scores.astype(BF16), vh) + _dot(qh, state.astype(BF16)) * qdec_ref[h]
        k_out = (kh.astype(F32) * kdec_ref[h]).astype(BF16)
        s_ref[0, h] = sdec[h] * state + _dot_tn(k_out, vh)
        gh = g_ref[0, :, h * V:(h + 1) * V].astype(F32)
        y_ref[0, :, h * V:(h + 1) * V] = (_rms(o) * _silu(gh)).astype(y_ref.dtype)


def _ret_tables(H, K, V, C):
    lg = np.log1p(-np.exp2(-5.0 - np.arange(H, dtype=np.float64)))
    i = np.arange(C, dtype=np.float64)
    diff = i[:, None] - i[None, :]
    dmat = np.where(diff >= 0, np.exp(lg[:, None, None] * np.maximum(diff, 0.0)), 0.0)
    qdec = np.broadcast_to(np.exp(lg[:, None, None] * (i[None, :, None] + 1.0)), (H, C, V))
    kdec = np.broadcast_to(np.exp(lg[:, None, None] * (C - 1.0 - i[None, :, None])), (H, C, K))
    sdec = tuple(float(np.exp(l * C)) for l in lg)
    return (jnp.asarray(dmat, F32), jnp.asarray(qdec, F32), jnp.asarray(kdec, F32), sdec)


def _k2_ret(q, k, v, g, H, K, V):
    G, R, _ = q.shape
    C = C_RET
    dmat, qdec, kdec, sdec = _ret_tables(H, K, V, C)
    return pl.pallas_call(
        functools.partial(_k2_ret_body, H=H, K=K, V=V, sdec=sdec),
        grid=(G, R // C),
        in_specs=[_row_spec(C, H * K), _row_spec(C, H * K), _row_spec(C, H * V), _row_spec(C, H * V),
                  _const_spec(dmat.shape), _const_spec(qdec.shape), _const_spec(kdec.shape)],
        out_specs=[_row_spec(C, H * V), pl.BlockSpec((1, H, K, V), lambda b, c: (b, 0, 0, 0))],
        out_shape=[jax.ShapeDtypeStruct((G, R, H * V), BF16), jax.ShapeDtypeStruct((G, H, K, V), F32)],
        compiler_params=_params(("arbitrary", "arbitrary")),
        name="ret_core",
    )(q, k, v, g, dmat, qdec, kdec)


def _dec_ret_body(q_ref, k_ref, v_ref, s_ref, so_ref, o_ref, *, H, K, V, gammas):
    eye = lax.broadcasted_iota(jnp.int32, (K, K), 0) == lax.broadcasted_iota(jnp.int32, (K, K), 1)
    qrow = q_ref[0]
    krow = k_ref[0]
    vrow = v_ref[0]
    for h in range(H):
        qc = _col(qrow[:, h * K:(h + 1) * K], eye)
        kc = _col(krow[:, h * K:(h + 1) * K], eye)
        new = gammas[h] * s_ref[0, h] + kc * vrow[:, h * V:(h + 1) * V]
        so_ref[0, h] = new
        o_ref[0, :, h * V:(h + 1) * V] = jnp.sum(qc * new, axis=0, keepdims=True)


def _step_spec(n):
    return pl.BlockSpec((1, 1, n), lambda b: (b, 0, 0))


def _as_steps(x):
    return x.reshape(x.shape[0], 1, x.shape[1])


def _dec_ret(q, k, v, state, H, K, V):
    B = q.shape[0]
    gammas = tuple(float(1.0 - 2.0 ** (-5 - h)) for h in range(H))
    return pl.pallas_call(
        functools.partial(_dec_ret_body, H=H, K=K, V=V, gammas=gammas),
        grid=(B,),
        in_specs=[_step_spec(H * K), _step_spec(H * K), _step_spec(H * V),
                  pl.BlockSpec((1, H, K, V), lambda b: (b, 0, 0, 0))],
        out_specs=[pl.BlockSpec((1, H, K, V), lambda b: (b, 0, 0, 0)), _step_spec(H * V)],
        out_shape=[jax.ShapeDtypeStruct(state.shape, F32), jax.ShapeDtypeStruct((B, 1, H * V), F32)],
        compiler_params=_params(("arbitrary",)),
        name="ret_step",
    )(_as_steps(q), _as_steps(k), _as_steps(v), state)


def _ssd_route(c, DI, xs_ref, b_ref, c_ref):
    w = 512
    if c * w < DI:
        return xs_ref, c * w
    if c * w < DI + b_ref.shape[-1]:
        return b_ref, c * w - DI
    return c_ref, c * w - DI - b_ref.shape[-1]


def _k1_ssd_prompt_body(x_ref, sh_ref, sc_ref, nw_ref, w_ref, wdt_ref, cw_ref, cb_ref, dtb_ref,
                        z_ref, xs_ref, b_ref, c_ref, dt_ref, nconv_ref, cbuf, *, tm, DI, CD, nt):
    t = pl.program_id(1)
    hb = _norm_mod(x_ref, sh_ref, sc_ref, nw_ref)
    w = 512
    for c in range(DI // w):
        z_ref[0, :, c * w:(c + 1) * w] = _dot(hb, w_ref[:, c * w:(c + 1) * w]).astype(z_ref.dtype)
    dt_ref[0] = _softplus(_dot(hb, wdt_ref[...]) + dtb_ref[...])

    @pl.when(t == 0)
    def _():
        cbuf[0:SUBLANES, :] = jnp.zeros((SUBLANES, CD), F32)

    for c in range(CD // w):
        cols = slice(c * w, (c + 1) * w)
        raw = _dot(hb, w_ref[:, DI + c * w:DI + (c + 1) * w])
        cbuf[SUBLANES:SUBLANES + tm, cols] = raw
        conv = cb_ref[:, cols] + cw_ref[3:4, cols] * raw
        for k in range(SSD_CONV - 1):
            conv = conv + cw_ref[k:k + 1, cols] * cbuf[SUBLANES - 3 + k:SUBLANES - 3 + k + tm, cols]
        dst, off = _ssd_route(c, DI, xs_ref, b_ref, c_ref)
        dst[0, :, off:off + w] = _silu(conv).astype(dst.dtype)

    @pl.when(t == nt - 1)
    def _():
        nconv_ref[0] = cbuf[tm + SUBLANES - 3:tm + SUBLANES, :]

    cbuf[0:SUBLANES, :] = cbuf[tm:tm + SUBLANES, :]


def _k1_ssd_prompt(x3, mod3, nw, w, wdt, cw, cb, dtb, tm, DI, GN):
    G, R, d = x3.shape
    CD = DI + 2 * GN
    nt = R // tm
    outs = [jax.ShapeDtypeStruct((G, R, DI), BF16), jax.ShapeDtypeStruct((G, R, DI), BF16),
            jax.ShapeDtypeStruct((G, R, GN), BF16), jax.ShapeDtypeStruct((G, R, GN), BF16),
            jax.ShapeDtypeStruct((G, R, LANES), F32), jax.ShapeDtypeStruct((G, SSD_CONV - 1, CD), F32)]
    return pl.pallas_call(
        functools.partial(_k1_ssd_prompt_body, tm=tm, DI=DI, CD=CD, nt=nt),
        grid=(G, nt),
        in_specs=_k1_specs(x3, mod3, 0, 1, tm) + [_const_spec(w.shape), _const_spec(wdt.shape),
                                                  _const_spec(cw.shape), _const_spec(cb.shape),
                                                  _const_spec(dtb.shape)],
        out_specs=[_row_spec(tm, DI), _row_spec(tm, DI), _row_spec(tm, GN), _row_spec(tm, GN),
                   _row_spec(tm, LANES), pl.BlockSpec((1, SSD_CONV - 1, CD), lambda g, t: (g, 0, 0))],
        out_shape=outs,
        scratch_shapes=[pltpu.VMEM((tm + SUBLANES, CD), F32)],
        compiler_params=_params(("arbitrary", "arbitrary")),
        name="ssd_inproj",
    )(x3, mod3, mod3, nw, w, wdt, cw, cb, dtb)


def _k1_ssd_sample_body(x_ref, sh_ref, sc_ref, nw_ref, w_ref, wdt_ref, cw_ref, cb_ref, dtb_ref, buf_ref,
                        z_ref, xs_ref, b_ref, c_ref, dt_ref, nconv_ref, *, DI, CD):
    hb = _norm_mod(x_ref, sh_ref, sc_ref, nw_ref)
    w = 512
    for c in range(DI // w):
        z_ref[0, :, c * w:(c + 1) * w] = _dot(hb, w_ref[:, c * w:(c + 1) * w])
    dt_ref[0] = _softplus(_dot(hb, wdt_ref[...]) + dtb_ref[...])
    for c in range(CD // w):
        cols = slice(c * w, (c + 1) * w)
        raw = _dot(hb, w_ref[:, DI + c * w:DI + (c + 1) * w])
        conv = cb_ref[:, cols] + cw_ref[3:4, cols] * raw
        for k in range(SSD_CONV - 1):
            conv = conv + cw_ref[k:k + 1, cols] * buf_ref[k, :, cols]
        dst, off = _ssd_route(c, DI, xs_ref, b_ref, c_ref)
        dst[0, :, off:off + w] = _silu(conv)
        nconv_ref[0, :, cols] = buf_ref[1, :, cols]
        nconv_ref[1, :, cols] = buf_ref[2, :, cols]
        nconv_ref[2, :, cols] = raw


def _k1_ssd_sample(x3, mod3, nw, w, wdt, cw, cb, dtb, buf_t, DI, GN):
    G, R, d = x3.shape
    CD = DI + 2 * GN
    tm = R
    outs = [jax.ShapeDtypeStruct((G, R, DI), F32), jax.ShapeDtypeStruct((G, R, DI), F32),
            jax.ShapeDtypeStruct((G, R, GN), F32), jax.ShapeDtypeStruct((G, R, GN), F32),
            jax.ShapeDtypeStruct((G, R, LANES), F32), jax.ShapeDtypeStruct((SSD_CONV - 1, R, CD), F32)]
    return pl.pallas_call(
        functools.partial(_k1_ssd_sample_body, DI=DI, CD=CD),
        grid=(G, 1),
        in_specs=_k1_specs(x3, mod3, 0, 1, tm) + [_const_spec(w.shape), _const_spec(wdt.shape),
                                                  _const_spec(cw.shape), _const_spec(cb.shape),
                                                  _const_spec(dtb.shape), _const_spec(buf_t.shape)],
        out_specs=[_row_spec(tm, DI), _row_spec(tm, DI), _row_spec(tm, GN), _row_spec(tm, GN),
                   _row_spec(tm, LANES), pl.BlockSpec((SSD_CONV - 1, R, CD), lambda g, t: (0, 0, 0))],
        out_shape=outs,
        compiler_params=_params(("arbitrary", "arbitrary")),
        name="ssd_inproj_step",
    )(x3, mod3, mod3, nw, w, wdt, cw, cb, dtb, buf_t)


def _k2_ssd_body(z_ref, xs_ref, b_ref, c_ref, dt_ref, tri_ref, alog_ref, exp_ref, drep_ref, nw_ref,
                 y_ref, so_ref, s_scr, *, C, NH, P, N, NG, nc):
    ci = pl.program_id(1)

    @pl.when(ci == 0)
    def _():
        s_scr[...] = jnp.zeros_like(s_scr)

    hpg = NH // NG
    gw = hpg * P
    dt = dt_ref[0]
    la = dt * (-jnp.exp(alog_ref[...]))
    cum = _sel_dot(tri_ref[...], _split3(la))
    tot = cum[C - 1:C, :]
    cum_t = cum.T
    dt_t = dt.T
    expand = exp_ref[...]
    w_state = _dot_sel(_split3(jnp.exp(tot - cum) * dt), expand)
    w_in = _dot_sel(_split3(jnp.exp(cum)), expand)
    xs = xs_ref[0]
    xsf = xs.astype(F32)
    xp = (xsf * w_state).astype(BF16)
    row = lax.broadcasted_iota(jnp.int32, (C, C), 0)
    colm = lax.broadcasted_iota(jnp.int32, (C, C), 1)
    causal = row >= colm
    lane = lax.broadcasted_iota(jnp.int32, (1, 2 * P), 1)
    head_keep = ((lane < P).astype(BF16), (lane >= P).astype(BF16))
    ys = []
    for g in range(NG):
        bg = b_ref[0, :, g * N:(g + 1) * N]
        cg = c_ref[0, :, g * N:(g + 1) * N]
        gsl = slice(g * gw, (g + 1) * gw)
        gmat = _dot_nt(cg, bg)
        state = s_scr[g]
        y_inter = _dot(cg, state.astype(BF16)) * w_in[:, gsl]
        s_scr[g] = state * w_in[C - 1:C, gsl] + _dot_tn(bg, xp[:, gsl])
        for hp in range(hpg // 2):
            h0 = g * hpg + 2 * hp
            xpair = xs[:, h0 * P:(h0 + 2) * P]
            acc = y_inter[:, 2 * hp * P:(2 * hp + 2) * P]
            for e in range(2):
                h = h0 + e
                seg = cum[:, h:h + 1] - cum_t[h:h + 1, :]
                dec = jnp.where(causal, jnp.exp(jnp.minimum(seg, 0.0)), 0.0) * dt_t[h:h + 1, :]
                m = (gmat * dec).astype(BF16)
                acc = acc + _dot(m, xpair * head_keep[e])
            ys.append(acc)
    y = jnp.concatenate(ys, axis=-1)
    outs = _ssd_gate(y, xsf, z_ref[0].astype(F32), drep_ref[...], nw_ref[...], NG)
    for g in range(NG):
        y_ref[0, :, g * gw:(g + 1) * gw] = outs[g].astype(y_ref.dtype)

    @pl.when(ci == nc - 1)
    def _():
        for h in range(NH):
            g, hl = divmod(h, hpg)
            so_ref[0, h] = s_scr[g, :, hl * P:(hl + 1) * P]


def _k2_ssd(z, xs, bm, cm, dt, alog_row, drep, nw, NH, P, N, NG):
    G, R, DI = xs.shape
    C = C_SSD
    nc = R // C
    i = np.arange(C)
    tri = jnp.asarray(i[:, None] >= i[None, :], BF16)
    e = np.zeros((LANES, NH * P), np.float32)
    for h in range(NH):
        e[h, h * P:(h + 1) * P] = 1.0
    expand = jnp.asarray(e, BF16)
    gw = (NH // NG) * P
    return pl.pallas_call(
        functools.partial(_k2_ssd_body, C=C, NH=NH, P=P, N=N, NG=NG, nc=nc),
        grid=(G, nc),
        in_specs=[_row_spec(C, DI), _row_spec(C, DI), _row_spec(C, NG * N), _row_spec(C, NG * N),
                  _row_spec(C, LANES), _const_spec(tri.shape), _const_spec(alog_row.shape),
                  _const_spec(expand.shape), _const_spec(drep.shape), _const_spec(nw.shape)],
        out_specs=[_row_spec(C, DI), pl.BlockSpec((1, NH, N, P), lambda b, c: (b, 0, 0, 0))],
        out_shape=[jax.ShapeDtypeStruct((G, R, DI), BF16), jax.ShapeDtypeStruct((G, NH, N, P), F32)],
        scratch_shapes=[pltpu.VMEM((NG, N, gw), F32)],
        compiler_params=_params(("arbitrary", "arbitrary")),
        name="ssd_core",
    )(z, xs, bm, cm, dt, tri, alog_row, expand, drep, nw)


def _dec_ssd_body(xs_ref, b_ref, c_ref, dt_ref, alog_ref, s_ref, so_ref, o_ref, *, NH, P, N, NG):
    eye = lax.broadcasted_iota(jnp.int32, (N, N), 0) == lax.broadcasted_iota(jnp.int32, (N, N), 1)
    dtrow = dt_ref[0]
    arow = jnp.exp(dtrow * (-jnp.exp(alog_ref[...])))
    xrow = xs_ref[0]
    brow = b_ref[0]
    crow = c_ref[0]
    hpg = NH // NG
    per_store = LANES // P
    for g in range(NG):
        qc = _col(crow[:, g * N:(g + 1) * N], eye)
        kc = _col(brow[:, g * N:(g + 1) * N], eye)
        pieces = []
        for hl in range(hpg):
            h = g * hpg + hl
            vr = xrow[:, h * P:(h + 1) * P] * dtrow[:, h:h + 1]
            new = arow[:, h:h + 1] * s_ref[0, h] + kc * vr
            so_ref[0, h] = new
            pieces.append(jnp.sum(qc * new, axis=0, keepdims=True))
            if len(pieces) == per_store:
                lo = (h + 1) * P - LANES
                o_ref[0, :, lo:lo + LANES] = jnp.concatenate(pieces, axis=-1)
                pieces = []


def _dec_ssd(xs, bm, cm, dt, alog_row, state, NH, P, N, NG):
    B = xs.shape[0]
    return pl.pallas_call(
        functools.partial(_dec_ssd_body, NH=NH, P=P, N=N, NG=NG),
        grid=(B,),
        in_specs=[_step_spec(NH * P), _step_spec(NG * N), _step_spec(NG * N), _step_spec(LANES),
                  _const_spec(alog_row.shape), pl.BlockSpec((1, NH, N, P), lambda b: (b, 0, 0, 0))],
        out_specs=[pl.BlockSpec((1, NH, N, P), lambda b: (b, 0, 0, 0)), _step_spec(NH * P)],
        out_shape=[jax.ShapeDtypeStruct(state.shape, F32), jax.ShapeDtypeStruct((B, 1, NH * P), F32)],
        compiler_params=_params(("arbitrary",)),
        name="ssd_step",
    )(_as_steps(xs), _as_steps(bm), _as_steps(cm), _as_steps(dt), alog_row, state)


def _hg_lower_bound(lbl_ref, layer):
    logits = lbl_ref[...]
    e = jnp.exp(logits - jnp.max(logits, axis=0, keepdims=True))
    sm = e / jnp.sum(e, axis=0, keepdims=True)
    lb = jnp.zeros_like(sm[0:1])
    for i in range(1, layer + 1):
        lb = lb + sm[i:i + 1]
    return lb


def _k1_hg_body(x_ref, sh_ref, sc_ref, nw_ref, w_ref, lbl_ref, *out_refs, layer, KD, VD, prompt):
    hb = _norm_mod(x_ref, sh_ref, sc_ref, nw_ref)
    lb = _hg_lower_bound(lbl_ref, layer)
    w = 512
    if prompt:
        q_ref, la_ref, v_ref, g_ref = out_refs
    else:
        q_ref, k_ref, a_ref, v_ref, g_ref = out_refs
    for c in range(KD // w):
        cols = slice(c * w, (c + 1) * w)
        q = _silu(_dot(hb, w_ref[:, c * w:(c + 1) * w])) * (HG_KEY_DIM ** -0.5)
        q_ref[0, :, cols] = q.astype(q_ref.dtype)
        lbc = lb[:, cols]
        f = lbc + (1.0 - lbc) * jax.nn.sigmoid(_dot(hb, w_ref[:, KD + c * w:KD + (c + 1) * w]))
        if prompt:
            la_ref[0, :, cols] = jnp.log(f)
        else:
            k_ref[0, :, cols] = 1.0 - f
            a_ref[0, :, cols] = f
    for c in range(VD // w):
        cols = slice(c * w, (c + 1) * w)
        v_ref[0, :, cols] = _dot(hb, w_ref[:, 2 * KD + c * w:2 * KD + (c + 1) * w]).astype(v_ref.dtype)
        g_ref[0, :, cols] = _dot(hb, w_ref[:, 2 * KD + VD + c * w:2 * KD + VD + (c + 1) * w]).astype(g_ref.dtype)


def _k1_hg(x3, mod3, nw, w, lbl, tm, layer, KD, VD, prompt):
    G, R, d = x3.shape
    if prompt:
        dts = [(KD, BF16), (KD, F32), (VD, BF16), (VD, BF16)]
    else:
        dts = [(KD, F32), (KD, F32), (KD, F32), (VD, F32), (VD, F32)]
    return pl.pallas_call(
        functools.partial(_k1_hg_body, layer=layer, KD=KD, VD=VD, prompt=prompt),
        grid=(G, R // tm),
        in_specs=_k1_specs(x3, mod3, 0, 1, tm) + [_const_spec(w.shape), _const_spec(lbl.shape)],
        out_specs=[_row_spec(tm, n) for n, _ in dts],
        out_shape=[jax.ShapeDtypeStruct((G, R, n), dt) for n, dt in dts],
        compiler_params=_params(("arbitrary", "arbitrary")),
        name="hg_inproj" if prompt else "hg_inproj_step",
    )(x3, mod3, mod3, nw, w, lbl)


def _hg_masks(C):
    i = np.arange(C)
    r, t = i[:, None], i[None, :]
    sels = [t <= r, (t <= r) & (t >= HG_SUB * (r // HG_SUB))]
    pairs = []
    s = C // 2
    while s >= HG_SUB:
        mid = 2 * s * (r // (2 * s)) + s - 1
        sels.append(np.where(r > mid, (t > mid) & (t <= r), (t > r) & (t <= mid)))
        pairs.append((r // (2 * s) == t // (2 * s)) & (r % (2 * s) >= s) & (t % (2 * s) < s))
        s //= 2
    return (jnp.asarray(np.stack(sels), BF16), jnp.asarray(np.stack(pairs), F32))


def _k2_hg_body(q_ref, la_ref, v_ref, g_ref, sel_ref, pair_ref, ones_ref, nw_ref, y_ref, so_ref,
                st_scr, kpad, cpad, vpad, *, C, H, K, V, nlev, nc):
    ci = pl.program_id(1)

    @pl.when(ci == 0)
    def _():
        st_scr[...] = jnp.zeros_like(st_scr)

    D = H * K
    la = la_ref[0]
    parts = _split3(la)
    cum = _sel_dot(sel_ref[0], parts)
    cw = _sel_dot(sel_ref[1], parts)
    kf = 1.0 - jnp.exp(la)
    qf = q_ref[0].astype(F32)
    vb = v_ref[0]
    vf = vb.astype(F32)
    tot = cum[C - 1:C, :]
    q_in = (qf * jnp.exp(cum)).astype(BF16)
    k_out = (kf * jnp.exp(tot - cum)).astype(BF16)
    e_tot = jnp.exp(tot)

    scores = [jnp.zeros((C, C), F32) for _ in range(H)]
    for lev in range(nlev):
        a = jnp.exp(_sel_dot(sel_ref[2 + lev], parts))
        ql = (qf * a).astype(BF16)
        kl = (kf * a).astype(BF16)
        pm = pair_ref[lev] > 0.5
        for h in range(H):
            sl = slice(h * K, (h + 1) * K)
            scores[h] = scores[h] + jnp.where(pm, _dot_nt(ql[:, sl], kl[:, sl]), 0.0)

    zpad = jnp.zeros((HG_SUB, D), F32)
    kpad[0:HG_SUB, :] = zpad
    cpad[0:HG_SUB, :] = zpad
    vpad[0:HG_SUB, :] = zpad
    kpad[HG_SUB:HG_SUB + C, :] = kf
    cpad[HG_SUB:HG_SUB + C, :] = cw
    vpad[HG_SUB:HG_SUB + C, :] = vf
    rowmod = lax.broadcasted_iota(jnp.int32, (C, D), 0) & (HG_SUB - 1)
    ones_bd = ones_ref[...]
    pw = ones_bd.shape[0]
    o_diag = [jnp.zeros((C, pw), F32) for _ in range(D // pw)]
    for d in range(HG_SUB):
        ks = kpad[HG_SUB - d:HG_SUB - d + C, :]
        cs = cpad[HG_SUB - d:HG_SUB - d + C, :]
        vs = vpad[HG_SUB - d:HG_SUB - d + C, :]
        p = jnp.where(rowmod >= d, qf * ks * jnp.exp(jnp.minimum(cw - cs, 0.0)), 0.0).astype(BF16)
        for j in range(D // pw):
            sl = slice(j * pw, (j + 1) * pw)
            o_diag[j] = o_diag[j] + _dot(p[:, sl], ones_bd) * vs[:, sl]

    nw = nw_ref[...]
    for h in range(H):
        sl = slice(h * K, (h + 1) * K)
        vsl = slice(h * V, (h + 1) * V)
        st = st_scr[h]
        o = (_dot(scores[h].astype(BF16), vb[:, vsl]) + _dot_nt(q_in[:, sl], st.astype(BF16))
             + o_diag[(h * V) // pw][:, (h * V) % pw:(h * V) % pw + V])
        st_scr[h] = st * e_tot[:, sl] + _dot_tn(vb[:, vsl], k_out[:, sl])
        gh = g_ref[0, :, vsl].astype(F32)
        y_ref[0, :, vsl] = (_rms(o) * nw[:, vsl] * _silu(gh)).astype(y_ref.dtype)

    @pl.when(ci == nc - 1)
    def _():
        for h in range(H):
            so_ref[0, h] = st_scr[h].T


def _k2_hg(q, la, v, g, nw_rep, H, K, V):
    G, R, D = q.shape
    C = C_HG
    nc = R // C
    sels, pairs = _hg_masks(C)
    nlev = pairs.shape[0]
    pw = 2 * K
    ob = np.zeros((pw, pw), np.float32)
    for j in range(pw // K):
        ob[j * K:(j + 1) * K, j * K:(j + 1) * K] = 1.0
    ones_bd = jnp.asarray(ob, BF16)
    return pl.pallas_call(
        functools.partial(_k2_hg_body, C=C, H=H, K=K, V=V, nlev=nlev, nc=nc),
        grid=(G, nc),
        in_specs=[_row_spec(C, D), _row_spec(C, D), _row_spec(C, H * V), _row_spec(C, H * V),
                  _const_spec(sels.shape), _const_spec(pairs.shape), _const_spec(ones_bd.shape),
                  _const_spec(nw_rep.shape)],
        out_specs=[_row_spec(C, H * V), pl.BlockSpec((1, H, K, V), lambda b, c: (b, 0, 0, 0))],
        out_shape=[jax.ShapeDtypeStruct((G, R, H * V), BF16), jax.ShapeDtypeStruct((G, H, K, V), F32)],
        scratch_shapes=[pltpu.VMEM((H, V, K), F32), pltpu.VMEM((HG_SUB + C, D), F32),
                        pltpu.VMEM((HG_SUB + C, D), F32), pltpu.VMEM((HG_SUB + C, H * V), F32)],
        compiler_params=_params(("arbitrary", "arbitrary")),
        name="hg_core",
    )(q, la, v, g, sels, pairs, ones_bd, nw_rep)


def _dec_hg_body(q_ref, k_ref, a_ref, v_ref, s_ref, so_ref, o_ref, *, H, K, V):
    eye = lax.broadcasted_iota(jnp.int32, (K, K), 0) == lax.broadcasted_iota(jnp.int32, (K, K), 1)
    qrow = q_ref[0]
    krow = k_ref[0]
    arow = a_ref[0]
    vrow = v_ref[0]
    for h in range(H):
        sl = slice(h * K, (h + 1) * K)
        qc = _col(qrow[:, sl], eye)
        kc = _col(krow[:, sl], eye)
        ac = _col(arow[:, sl], eye)
        new = ac * s_ref[0, h] + kc * vrow[:, h * V:(h + 1) * V]
        so_ref[0, h] = new
        o_ref[0, :, h * V:(h + 1) * V] = jnp.sum(qc * new, axis=0, keepdims=True)


def _dec_hg(q, k, a, v, state, H, K, V):
    B = q.shape[0]
    return pl.pallas_call(
        functools.partial(_dec_hg_body, H=H, K=K, V=V),
        grid=(B,),
        in_specs=[_step_spec(H * K), _step_spec(H * K), _step_spec(H * K), _step_spec(H * V),
                  pl.BlockSpec((1, H, K, V), lambda b: (b, 0, 0, 0))],
        out_specs=[pl.BlockSpec((1, H, K, V), lambda b: (b, 0, 0, 0)), _step_spec(H * V)],
        out_shape=[jax.ShapeDtypeStruct(state.shape, F32), jax.ShapeDtypeStruct((B, 1, H * V), F32)],
        compiler_params=_params(("arbitrary",)),
        name="hg_step",
    )(_as_steps(q), _as_steps(k), _as_steps(a), _as_steps(v), state)


def _pro_identity(y_ref):
    return y_ref[0]


def _pro_ret(o_ref, g_ref, *, H, V):
    return jnp.concatenate(_ret_gate(o_ref[0], g_ref[0], H, V), axis=-1).astype(BF16)


def _pro_ssd(o_ref, xs_ref, z_ref, drep_ref, nw_ref, *, NG):
    return jnp.concatenate(_ssd_gate(o_ref[0], xs_ref[0], z_ref[0], drep_ref[...], nw_ref[...], NG),
                           axis=-1).astype(BF16)


def _pro_hg(o_ref, g_ref, nw_ref, *, H, V):
    return jnp.concatenate(_hg_gate(o_ref[0], g_ref[0], nw_ref[...], H, V), axis=-1).astype(BF16)


def _k3_body(*refs, prologue, n_pro):
    w_ref, x_ref, gate_ref, nw_ref, o_ref = refs[n_pro:]
    y = prologue(*refs[:n_pro])
    o = _dot(y, w_ref[...])
    o_ref[0] = x_ref[0] + gate_ref[0] * (_rms(o) * nw_ref[...])


def _k3(prologue, pro_args, pro_specs, w, x3, mod3, gate_idx, nw, tm):
    G, R, d = x3.shape
    rm = mod3.shape[1]
    n_pro = len(pro_args)
    return pl.pallas_call(
        functools.partial(_k3_body, prologue=prologue, n_pro=n_pro),
        grid=(G, R // tm),
        in_specs=list(pro_specs) + [
            _const_spec(w.shape),
            pl.BlockSpec((1, tm, d), lambda g, t: (g, t, 0)),
            pl.BlockSpec((1, rm, d), lambda g, t: (g, 0, gate_idx)),
            _const_spec((1, d)),
        ],
        out_specs=pl.BlockSpec((1, tm, d), lambda g, t: (g, t, 0)),
        out_shape=jax.ShapeDtypeStruct((G, R, d), F32),
        compiler_params=_params(("arbitrary", "arbitrary")),
        name="outproj",
    )(*pro_args, w, x3, mod3, nw)


def _k4_body(x_ref, sh_ref, sc_ref, nw_ref, gate_ref, nwp_ref, win_ref, wout_ref, o_ref, *, F, chunk):
    x = x_ref[0]
    hb = _norm_mod(x_ref, sh_ref, sc_ref, nw_ref)
    acc = jnp.zeros(x.shape, F32)
    for c in range(F // chunk):
        gt = _dot(hb, win_ref[:, c * chunk:(c + 1) * chunk])
        up = _dot(hb, win_ref[:, F + c * chunk:F + (c + 1) * chunk])
        a = (_silu(gt) * up).astype(BF16)
        acc = acc + _dot(a, wout_ref[c * chunk:(c + 1) * chunk, :])
    o_ref[0] = x + gate_ref[0] * (_rms(acc) * nwp_ref[...])


def _k4(x3, mod3, nw_pre, nw_post, win, wout, tm):
    G, R, d = x3.shape
    rm = mod3.shape[1]
    F = wout.shape[0]
    return pl.pallas_call(
        functools.partial(_k4_body, F=F, chunk=FFN_CHUNK),
        grid=(G, R // tm),
        in_specs=_k1_specs(x3, mod3, 3, 4, tm) + [
            pl.BlockSpec((1, rm, d), lambda g, t: (g, 0, 5)),
            _const_spec((1, d)), _const_spec(win.shape), _const_spec(wout.shape)],
        out_specs=pl.BlockSpec((1, tm, d), lambda g, t: (g, t, 0)),
        out_shape=jax.ShapeDtypeStruct((G, R, d), F32),
        compiler_params=_params(("arbitrary", "arbitrary")),
        name="ffn",
    )(x3, mod3, mod3, nw_pre, mod3, nw_post, win, wout)


def _rope_table(pos, half):
    inv = 1.0 / (ROPE_BASE ** (jnp.arange(half, dtype=F32) / half))
    ang = pos.astype(F32)[:, None] * inv[None, :]
    return jnp.cos(ang), jnp.sin(ang)


def kernel(x_prompt, x_sample, c_prompt, c_sample, state_ret, state_ssd, state_conv, state_hgrn, w_ada, b_ada, norm_mix_pre, norm_mix_post, norm_ffn_pre, norm_ffn_post, ret_w_in, ret_w_out, ssd_w_in, ssd_conv_w, ssd_conv_b, ssd_dt_bias, ssd_a_log, ssd_d, ssd_norm, ssd_w_out, hg_w_in, hg_lb_logits, hg_norm, hg_w_out, ffn_w_in, ffn_w_out):
    bp, seq, d = x_prompt.shape
    bs = x_sample.shape[0]
    depth = w_ada.shape[0]
    ret_h = d // 256
    ret_k = d // ret_h
    ret_v = 2 * ret_k
    di = 2 * d
    ssd_nh = di // SSD_HEAD_DIM
    gn = SSD_GROUPS * SSD_STATE
    hg_h = d // HG_KEY_DIM
    hg_v = d // hg_h
    kd = hg_h * HG_KEY_DIM
    vd = hg_h * hg_v
    tm = min(TM_PROMPT, seq)

    mod = _adaln(jnp.concatenate([c_prompt, c_sample], axis=0), w_ada, b_ada)
    xp = x_prompt.astype(F32)
    xs = x_sample.astype(F32).reshape(1, bs, d)
    cos_p, sin_p = _rope_table(jnp.arange(seq, dtype=jnp.int32), ret_k // 2)
    cos_s, sin_s = _rope_table(jnp.full((1,), PAST_LEN, jnp.int32), ret_k // 2)

    new = {k: [] for k in ("ret_p", "ret_s", "ssd_p", "ssd_s", "conv_p", "conv_s", "hg_p", "hg_s")}
    counts = [0, 0, 0]
    for layer in range(depth):
        mod_p = mod[layer, :bp].reshape(bp, 1, 6 * d)
        mod_s = mod[layer, bp:].reshape(1, bs, 6 * d)
        nw_pre = norm_mix_pre[layer].reshape(1, d)
        nw_post = norm_mix_post[layer].reshape(1, d)
        kind = layer % 3
        j = counts[kind]
        counts[kind] += 1
        y_spec = lambda n: [_row_spec(tm, n)]
        s_spec = lambda n: _row_spec(bs, n)
        if kind == 0:
            w_in = ret_w_in[j].astype(BF16)
            w_out = ret_w_out[j].astype(BF16)
            q, k, v, g = _k1_ret(xp, mod_p, nw_pre, w_in, cos_p, sin_p, tm, BF16, ret_h, ret_k, ret_v)
            y, st = _k2_ret(q, k, v, g, ret_h, ret_k, ret_v)
            new["ret_p"].append(st)
            xp = _k3(_pro_identity, [y], y_spec(ret_h * ret_v), w_out, xp, mod_p, 2, nw_post, tm)
            q, k, v, g = _k1_ret(xs, mod_s, nw_pre, w_in, cos_s, sin_s, bs, F32, ret_h, ret_k, ret_v)
            st, o = _dec_ret(q[0], k[0], v[0], state_ret[j].astype(F32), ret_h, ret_k, ret_v)
            new["ret_s"].append(st)
            xs = _k3(functools.partial(_pro_ret, H=ret_h, V=ret_v), [o.reshape(1, bs, -1), g],
                     [s_spec(ret_h * ret_v)] * 2, w_out, xs, mod_s, 2, nw_post, bs)
        elif kind == 1:
            w_main = ssd_w_in[j][:, :di + di + 2 * gn].astype(BF16)
            w_dt = jnp.pad(ssd_w_in[j][:, di + di + 2 * gn:], ((0, 0), (0, LANES - ssd_nh))).astype(BF16)
            w_out = ssd_w_out[j].astype(BF16)
            cw = ssd_conv_w[j]
            cb = ssd_conv_b[j].reshape(1, -1)
            dtb = jnp.pad(ssd_dt_bias[j], (0, LANES - ssd_nh)).reshape(1, LANES)
            alog = jnp.pad(ssd_a_log[j].astype(F32), (0, LANES - ssd_nh)).reshape(1, LANES)
            drep = jnp.repeat(ssd_d[j], SSD_HEAD_DIM).reshape(1, di)
            nw_ssd = ssd_norm[j].reshape(1, di)
            z, xc, bm, cm, dt, nconv = _k1_ssd_prompt(xp, mod_p, nw_pre, w_main, w_dt, cw, cb, dtb, tm, di, gn)
            new["conv_p"].append(nconv)
            y, st = _k2_ssd(z, xc, bm, cm, dt, alog, drep, nw_ssd, ssd_nh, SSD_HEAD_DIM, SSD_STATE, SSD_GROUPS)
            new["ssd_p"].append(st)
            xp = _k3(_pro_identity, [y], y_spec(di), w_out, xp, mod_p, 2, nw_post, tm)
            buf_t = jnp.transpose(state_conv[j].astype(F32), (1, 0, 2))
            z, xc, bm, cm, dt, nconv = _k1_ssd_sample(xs, mod_s, nw_pre, w_main, w_dt, cw, cb, dtb, buf_t, di, gn)
            new["conv_s"].append(jnp.transpose(nconv, (1, 0, 2)))
            st, o = _dec_ssd(xc[0], bm[0], cm[0], dt[0], alog, state_ssd[j].astype(F32),
                             ssd_nh, SSD_HEAD_DIM, SSD_STATE, SSD_GROUPS)
            new["ssd_s"].append(st)
            xs = _k3(functools.partial(_pro_ssd, NG=SSD_GROUPS), [o.reshape(1, bs, -1), xc, z, drep, nw_ssd],
                     [s_spec(di)] * 3 + [_const_spec((1, di))] * 2, w_out, xs, mod_s, 2, nw_post, bs)
        else:
            w_in = hg_w_in[j].astype(BF16)
            w_out = hg_w_out[j].astype(BF16)
            nw_hg = jnp.tile(hg_norm[j], hg_h).reshape(1, vd)
            lbl = hg_lb_logits.astype(F32)
            q, la, v, g = _k1_hg(xp, mod_p, nw_pre, w_in, lbl, tm, layer, kd, vd, True)
            y, st = _k2_hg(q, la, v, g, nw_hg, hg_h, HG_KEY_DIM, hg_v)
            new["hg_p"].append(st)
            xp = _k3(_pro_identity, [y], y_spec(vd), w_out, xp, mod_p, 2, nw_post, tm)
            q, k, a, v, g = _k1_hg(xs, mod_s, nw_pre, w_in, lbl, bs, layer, kd, vd, False)
            st, o = _dec_hg(q[0], k[0], a[0], v[0], state_hgrn[j].astype(F32), hg_h, HG_KEY_DIM, hg_v)
            new["hg_s"].append(st)
            xs = _k3(functools.partial(_pro_hg, H=hg_h, V=hg_v), [o.reshape(1, bs, -1), g, nw_hg],
                     [s_spec(vd)] * 2 + [_const_spec((1, vd))], w_out, xs, mod_s, 2, nw_post, bs)
        nf_pre = norm_ffn_pre[layer].reshape(1, d)
        nf_post = norm_ffn_post[layer].reshape(1, d)
        f_in = ffn_w_in[layer].astype(BF16)
        f_out = ffn_w_out[layer].astype(BF16)
        xp = _k4(xp, mod_p, nf_pre, nf_post, f_in, f_out, tm)
        xs = _k4(xs, mod_s, nf_pre, nf_post, f_in, f_out, bs)

    stack = lambda name, like: jnp.stack(new[name]).astype(like.dtype)
    return (xp.astype(x_prompt.dtype), xs.reshape(bs, 1, d).astype(x_sample.dtype),
            stack("ret_p", state_ret), stack("ret_s", state_ret),
            stack("ssd_p", state_ssd), stack("ssd_s", state_ssd),
            stack("conv_p", state_conv), stack("conv_s", state_conv),
            stack("hg_p", state_hgrn), stack("hg_s", state_hgrn))
```

```python
import functools

import numpy as np
import jax
import jax.numpy as jnp
from jax import lax
from jax.experimental import pallas as pl
from jax.experimental.pallas import tpu as pltpu

F32 = jnp.float32
BF16 = jnp.bfloat16
EPS = 1e-6
PAST_LEN = 16384
ROPE_BASE = 10000.0
SSD_HEAD_DIM = 64
SSD_GROUPS = 4
SSD_STATE = 128
SSD_CONV = 4
HG_KEY_DIM = 128
LANES = 128
SUBLANES = 8
VMEM_LIMIT = 56 * 1024 * 1024
TM_PROMPT = 512
MXU_TILE = 256
C_RET = 256
C_SSD = 128
C_HG = 128
HG_SUB = 16
FFN_MAX_CHUNK_TILES = 6


def _rms(x):
    return x * lax.rsqrt(jnp.mean(x * x, axis=-1, keepdims=True) + EPS)


def _silu(x):
    return x * jax.nn.sigmoid(x)


def _softplus(x):
    return jnp.maximum(x, 0.0) + jnp.log1p(jnp.exp(-jnp.abs(x)))


def _dot(a, b):
    return jnp.dot(a, b, preferred_element_type=F32)


def _dot_nt(a, b):
    return lax.dot_general(a, b, (((1,), (1,)), ((), ())), preferred_element_type=F32)


def _dot_tn(a, b):
    return lax.dot_general(a, b, (((0,), (0,)), ((), ())), preferred_element_type=F32)


def _split3(x):
    hi = x.astype(BF16)
    r = x - hi.astype(F32)
    mid = r.astype(BF16)
    lo = (r - mid.astype(F32)).astype(BF16)
    return hi, mid, lo


def _sel_dot(sel, parts):
    acc = _dot(sel, parts[0])
    for p in parts[1:]:
        acc = acc + _dot(sel, p)
    return acc


def _dot_sel(parts, sel):
    acc = _dot(parts[0], sel)
    for p in parts[1:]:
        acc = acc + _dot(p, sel)
    return acc


def _col(row, eye):
    return jnp.sum(jnp.where(eye, row, 0.0), axis=1, keepdims=True)


def _norm_mod(x_ref, sh_ref, sc_ref, nw_ref):
    x = x_ref[0]
    h = _rms(x) * nw_ref[...] * (1.0 + sc_ref[0]) + sh_ref[0]
    return h.astype(BF16)


def _params(sem):
    return pltpu.CompilerParams(dimension_semantics=sem, vmem_limit_bytes=VMEM_LIMIT)


def _const_spec(shape):
    nd = len(shape)
    return pl.BlockSpec(shape, lambda *_: (0,) * nd, pipeline_mode=pl.Buffered(1))


def _ret_gate(o, g, H, V):
    outs = []
    for h in range(H):
        outs.append(_rms(o[:, h * V:(h + 1) * V]) * _silu(g[:, h * V:(h + 1) * V]))
    return outs


def _ssd_gate(y, xs, z, d_rep, nw, groups):
    y = (y + d_rep * xs) * _silu(z)
    w = y.shape[-1] // groups
    outs = []
    for g in range(groups):
        outs.append(_rms(y[:, g * w:(g + 1) * w]) * nw[:, g * w:(g + 1) * w])
    return outs


def _hg_gate(o, g, nw, H, V):
    outs = []
    for h in range(H):
        sl = slice(h * V, (h + 1) * V)
        outs.append(_rms(o[:, sl]) * nw[:, sl] * _silu(g[:, sl]))
    return outs


def _adaln_body(c_ref, w_ref, b_ref, o_ref):
    a = _silu(c_ref[...]).astype(BF16)
    o_ref[0] = _dot(a, w_ref[0].astype(BF16)) + b_ref[0]


def _adaln(c_all, w_ada, b_ada):
    depth, d, n = w_ada.shape
    bc = c_all.shape[0]
    tn = 1024
    return pl.pallas_call(
        _adaln_body,
        grid=(depth, n // tn),
        in_specs=[
            pl.BlockSpec((bc, d), lambda l, j: (0, 0)),
            pl.BlockSpec((1, d, tn), lambda l, j: (l, 0, j)),
            pl.BlockSpec((1, 1, tn), lambda l, j: (l, 0, j)),
        ],
        out_specs=pl.BlockSpec((1, bc, tn), lambda l, j: (l, 0, j)),
        out_shape=jax.ShapeDtypeStruct((depth, bc, n), F32),
        compiler_params=_params(("arbitrary", "arbitrary")),
        name="adaln",
    )(c_all, w_ada, b_ada.reshape(depth, 1, n))


def _k1_specs(x3, mod3, sh_idx, sc_idx, tm):
    _, _, d = x3.shape
    rm = mod3.shape[1]
    return [
        pl.BlockSpec((1, tm, d), lambda g, t: (g, t, 0)),
        pl.BlockSpec((1, rm, d), lambda g, t: (g, 0, sh_idx)),
        pl.BlockSpec((1, rm, d), lambda g, t: (g, 0, sc_idx)),
        _const_spec((1, d)),
    ]


def _row_spec(tm, n):
    return pl.BlockSpec((1, tm, n), lambda g, t: (g, t, 0))


def _layer_state_spec(tail, j):
    nd = len(tail)
    return pl.BlockSpec((None, None) + tuple(tail), lambda b, *_: (j, b) + (0,) * nd)


def _call_into_stack(body, *, grid, inputs, in_specs, out_specs, out_shape, stack_out, prev, name,
                     scratch_shapes=()):
    inputs, in_specs = list(inputs), list(in_specs)
    n_in = len(inputs)
    aliases = {}
    kernel_fn = body
    if prev is not None:
        inputs.append(prev)
        in_specs.append(pl.BlockSpec(memory_space=pl.ANY))
        aliases = {n_in: stack_out}

        def kernel_fn(*refs):
            body(*refs[:n_in], *refs[n_in + 1:])

    return pl.pallas_call(
        kernel_fn, grid=grid, in_specs=in_specs, out_specs=out_specs, out_shape=out_shape,
        scratch_shapes=scratch_shapes, input_output_aliases=aliases,
        compiler_params=_params(("arbitrary",) * len(grid)), name=name)(*inputs)


def _k1_ret_body(x_ref, sh_ref, sc_ref, nw_ref, w_ref, cos_ref, sin_ref,
                 q_ref, k_ref, v_ref, g_ref, *, H, K, V):
    hb = _norm_mod(x_ref, sh_ref, sc_ref, nw_ref)
    cos = cos_ref[...]
    sin = sin_ref[...]
    half = K // 2
    for h in range(H):
        for dst, base, scale in ((q_ref, 0, None), (k_ref, H * K, K ** -0.5)):
            a = _dot(hb, w_ref[:, base + h * K:base + (h + 1) * K])
            x1 = a[:, :half]
            x2 = a[:, half:]
            r1 = x1 * cos - x2 * sin
            r2 = x2 * cos + x1 * sin
            if scale is not None:
                r1 = r1 * scale
                r2 = r2 * scale
            dst[0, :, h * K:h * K + half] = r1.astype(dst.dtype)
            dst[0, :, h * K + half:(h + 1) * K] = r2.astype(dst.dtype)
    vb = 2 * H * K
    for c in range(H):
        v_ref[0, :, c * V:(c + 1) * V] = _dot(hb, w_ref[:, vb + c * V:vb + (c + 1) * V]).astype(v_ref.dtype)
        g_ref[0, :, c * V:(c + 1) * V] = _dot(
            hb, w_ref[:, vb + H * V + c * V:vb + H * V + (c + 1) * V]).astype(g_ref.dtype)


def _k1_ret(x3, mod3, nw, w, cos, sin, tm, out_dtype, H, K, V):
    G, R, d = x3.shape
    rc = cos.shape[0]
    cs_spec = (pl.BlockSpec((tm, K // 2), lambda g, t: (t, 0)) if rc == R
               else _const_spec((1, K // 2)))
    outs = [jax.ShapeDtypeStruct((G, R, H * K), out_dtype), jax.ShapeDtypeStruct((G, R, H * K), out_dtype),
            jax.ShapeDtypeStruct((G, R, H * V), out_dtype), jax.ShapeDtypeStruct((G, R, H * V), out_dtype)]
    return pl.pallas_call(
        functools.partial(_k1_ret_body, H=H, K=K, V=V),
        grid=(G, R // tm),
        in_specs=_k1_specs(x3, mod3, 0, 1, tm) + [_const_spec(w.shape), cs_spec, cs_spec],
        out_specs=[_row_spec(tm, H * K), _row_spec(tm, H * K), _row_spec(tm, H * V), _row_spec(tm, H * V)],
        out_shape=outs,
        compiler_params=_params(("arbitrary", "arbitrary")),
        name="ret_inproj",
    )(x3, mod3, mod3, nw, w, cos, sin)


def _k2_ret_body(q_ref, k_ref, v_ref, g_ref, dmat_ref, qdec_ref, kdec_ref, y_ref, s_ref,
                 *, H, K, V, sdec):
    @pl.when(pl.program_id(1) == 0)
    def _():
        s_ref[...] = jnp.zeros_like(s_ref)

    for h in range(H):
        qh = q_ref[0, :, h * K:(h + 1) * K]
        kh = k_ref[0, :, h * K:(h + 1) * K]
        vh = v_ref[0, :, h * V:(h + 1) * V]
        state = s_ref[h]
        scores = _dot_nt(qh, kh) * dmat_ref[h]
        o = _dot(scores.astype(BF16), vh) + _dot(qh, state.astype(BF16)) * qdec_ref[h]
        k_out = (kh.astype(F32) * kdec_ref[h]).astype(BF16)
        s_ref[h] = sdec[h] * state + _dot_tn(k_out, vh)
        gh = g_ref[0, :, h * V:(h + 1) * V].astype(F32)
        y_ref[0, :, h * V:(h + 1) * V] = (_rms(o) * _silu(gh)).astype(y_ref.dtype)


def _ret_tables(H, K, V, C):
    lg = np.log1p(-np.exp2(-5.0 - np.arange(H, dtype=np.float64)))
    i = np.arange(C, dtype=np.float64)
    diff = i[:, None] - i[None, :]
    dmat = np.where(diff >= 0, np.exp(lg[:, None, None] * np.maximum(diff, 0.0)), 0.0)
    qdec = np.broadcast_to(np.exp(lg[:, None, None] * (i[None, :, None] + 1.0)), (H, C, V))
    kdec = np.broadcast_to(np.exp(lg[:, None, None] * (C - 1.0 - i[None, :, None])), (H, C, K))
    sdec = tuple(float(np.exp(l * C)) for l in lg)
    return (jnp.asarray(dmat, F32), jnp.asarray(qdec, F32), jnp.asarray(kdec, F32), sdec)


def _k2_ret(q, k, v, g, H, K, V, nl, j, prev):
    G, R, _ = q.shape
    C = min(C_RET, R)
    dmat, qdec, kdec, sdec = _ret_tables(H, K, V, C)
    return _call_into_stack(
        functools.partial(_k2_ret_body, H=H, K=K, V=V, sdec=sdec),
        grid=(G, R // C),
        inputs=[q, k, v, g, dmat, qdec, kdec],
        in_specs=[_row_spec(C, H * K), _row_spec(C, H * K), _row_spec(C, H * V), _row_spec(C, H * V),
                  _const_spec(dmat.shape), _const_spec(qdec.shape), _const_spec(kdec.shape)],
        out_specs=[_row_spec(C, H * V), _layer_state_spec((H, K, V), j)],
        out_shape=[jax.ShapeDtypeStruct((G, R, H * V), BF16), jax.ShapeDtypeStruct((nl, G, H, K, V), F32)],
        stack_out=1, prev=prev, name="ret_core")


def _dec_ret_body(q_ref, k_ref, v_ref, s_ref, so_ref, o_ref, *, H, K, V, gammas):
    eye = lax.broadcasted_iota(jnp.int32, (K, K), 0) == lax.broadcasted_iota(jnp.int32, (K, K), 1)
    qrow = q_ref[0]
    krow = k_ref[0]
    vrow = v_ref[0]
    for h in range(H):
        qc = _col(qrow[:, h * K:(h + 1) * K], eye)
        kc = _col(krow[:, h * K:(h + 1) * K], eye)
        new = gammas[h] * s_ref[h] + kc * vrow[:, h * V:(h + 1) * V]
        so_ref[h] = new
        o_ref[0, :, h * V:(h + 1) * V] = jnp.sum(qc * new, axis=0, keepdims=True)


def _step_spec(n):
    return pl.BlockSpec((1, 1, n), lambda b: (b, 0, 0))


def _as_steps(x):
    return x.reshape(x.shape[0], 1, x.shape[1])


def _dec_ret(q, k, v, states, H, K, V, j, prev):
    B = q.shape[0]
    gammas = tuple(float(1.0 - 2.0 ** (-5 - h)) for h in range(H))
    return _call_into_stack(
        functools.partial(_dec_ret_body, H=H, K=K, V=V, gammas=gammas),
        grid=(B,),
        inputs=[_as_steps(q), _as_steps(k), _as_steps(v), states],
        in_specs=[_step_spec(H * K), _step_spec(H * K), _step_spec(H * V), _layer_state_spec((H, K, V), j)],
        out_specs=[_layer_state_spec((H, K, V), j), _step_spec(H * V)],
        out_shape=[jax.ShapeDtypeStruct(states.shape, F32), jax.ShapeDtypeStruct((B, 1, H * V), F32)],
        stack_out=0, prev=prev, name="ret_step")


def _ssd_route(c, DI, xs_ref, b_ref, c_ref):
    w = 512
    if c * w < DI:
        return xs_ref, c * w
    if c * w < DI + b_ref.shape[-1]:
        return b_ref, c * w - DI
    return c_ref, c * w - DI - b_ref.shape[-1]


def _k1_ssd_prompt_body(x_ref, sh_ref, sc_ref, nw_ref, w_ref, wdt_ref, cw_ref, cb_ref, dtb_ref,
                        z_ref, xs_ref, b_ref, c_ref, dt_ref, nconv_ref, cbuf, *, tm, DI, CD, nt):
    t = pl.program_id(1)
    hb = _norm_mod(x_ref, sh_ref, sc_ref, nw_ref)
    w = 512
    for c in range(DI // w):
        z_ref[0, :, c * w:(c + 1) * w] = _dot(hb, w_ref[:, c * w:(c + 1) * w]).astype(z_ref.dtype)
    dt_ref[0] = _softplus(_dot(hb, wdt_ref[...]) + dtb_ref[...])

    @pl.when(t == 0)
    def _():
        cbuf[0:SUBLANES, :] = jnp.zeros((SUBLANES, CD), F32)

    for c in range(CD // w):
        cols = slice(c * w, (c + 1) * w)
        raw = _dot(hb, w_ref[:, DI + c * w:DI + (c + 1) * w])
        cbuf[SUBLANES:SUBLANES + tm, cols] = raw
        conv = cb_ref[:, cols] + cw_ref[3:4, cols] * raw
        for k in range(SSD_CONV - 1):
            conv = conv + cw_ref[k:k + 1, cols] * cbuf[SUBLANES - 3 + k:SUBLANES - 3 + k + tm, cols]
        dst, off = _ssd_route(c, DI, xs_ref, b_ref, c_ref)
        dst[0, :, off:off + w] = _silu(conv).astype(dst.dtype)

    @pl.when(t == nt - 1)
    def _():
        nconv_ref[0] = cbuf[tm + SUBLANES - 3:tm + SUBLANES, :]

    cbuf[0:SUBLANES, :] = cbuf[tm:tm + SUBLANES, :]


def _k1_ssd_prompt(x3, mod3, nw, w, wdt, cw, cb, dtb, tm, DI, GN):
    G, R, d = x3.shape
    CD = DI + 2 * GN
    nt = R // tm
    outs = [jax.ShapeDtypeStruct((G, R, DI), BF16), jax.ShapeDtypeStruct((G, R, DI), BF16),
            jax.ShapeDtypeStruct((G, R, GN), BF16), jax.ShapeDtypeStruct((G, R, GN), BF16),
            jax.ShapeDtypeStruct((G, R, LANES), F32), jax.ShapeDtypeStruct((G, SSD_CONV - 1, CD), F32)]
    return pl.pallas_call(
        functools.partial(_k1_ssd_prompt_body, tm=tm, DI=DI, CD=CD, nt=nt),
        grid=(G, nt),
        in_specs=_k1_specs(x3, mod3, 0, 1, tm) + [_const_spec(w.shape), _const_spec(wdt.shape),
                                                  _const_spec(cw.shape), _const_spec(cb.shape),
                                                  _const_spec(dtb.shape)],
        out_specs=[_row_spec(tm, DI), _row_spec(tm, DI), _row_spec(tm, GN), _row_spec(tm, GN),
                   _row_spec(tm, LANES), pl.BlockSpec((1, SSD_CONV - 1, CD), lambda g, t: (g, 0, 0))],
        out_shape=outs,
        scratch_shapes=[pltpu.VMEM((tm + SUBLANES, CD), F32)],
        compiler_params=_params(("arbitrary", "arbitrary")),
        name="ssd_inproj",
    )(x3, mod3, mod3, nw, w, wdt, cw, cb, dtb)


def _k1_ssd_sample_body(x_ref, sh_ref, sc_ref, nw_ref, w_ref, wdt_ref, cw_ref, cb_ref, dtb_ref, buf_ref,
                        z_ref, xs_ref, b_ref, c_ref, dt_ref, nconv_ref, *, DI, CD):
    hb = _norm_mod(x_ref, sh_ref, sc_ref, nw_ref)
    w = 512
    for c in range(DI // w):
        z_ref[0, :, c * w:(c + 1) * w] = _dot(hb, w_ref[:, c * w:(c + 1) * w])
    dt_ref[0] = _softplus(_dot(hb, wdt_ref[...]) + dtb_ref[...])
    for c in range(CD // w):
        cols = slice(c * w, (c + 1) * w)
        raw = _dot(hb, w_ref[:, DI + c * w:DI + (c + 1) * w])
        conv = cb_ref[:, cols] + cw_ref[3:4, cols] * raw
        for k in range(SSD_CONV - 1):
            conv = conv + cw_ref[k:k + 1, cols] * buf_ref[k, :, cols]
        dst, off = _ssd_route(c, DI, xs_ref, b_ref, c_ref)
        dst[0, :, off:off + w] = _silu(conv)
        nconv_ref[0, :, cols] = buf_ref[1, :, cols]
        nconv_ref[1, :, cols] = buf_ref[2, :, cols]
        nconv_ref[2, :, cols] = raw


def _k1_ssd_sample(x3, mod3, nw, w, wdt, cw, cb, dtb, buf_t, DI, GN):
    G, R, d = x3.shape
    CD = DI + 2 * GN
    tm = R
    outs = [jax.ShapeDtypeStruct((G, R, DI), F32), jax.ShapeDtypeStruct((G, R, DI), F32),
            jax.ShapeDtypeStruct((G, R, GN), F32), jax.ShapeDtypeStruct((G, R, GN), F32),
            jax.ShapeDtypeStruct((G, R, LANES), F32), jax.ShapeDtypeStruct((SSD_CONV - 1, R, CD), F32)]
    return pl.pallas_call(
        functools.partial(_k1_ssd_sample_body, DI=DI, CD=CD),
        grid=(G, 1),
        in_specs=_k1_specs(x3, mod3, 0, 1, tm) + [_const_spec(w.shape), _const_spec(wdt.shape),
                                                  _const_spec(cw.shape), _const_spec(cb.shape),
                                                  _const_spec(dtb.shape), _const_spec(buf_t.shape)],
        out_specs=[_row_spec(tm, DI), _row_spec(tm, DI), _row_spec(tm, GN), _row_spec(tm, GN),
                   _row_spec(tm, LANES), pl.BlockSpec((SSD_CONV - 1, R, CD), lambda g, t: (0, 0, 0))],
        out_shape=outs,
        compiler_params=_params(("arbitrary", "arbitrary")),
        name="ssd_inproj_step",
    )(x3, mod3, mod3, nw, w, wdt, cw, cb, dtb, buf_t)


def _k2_ssd_body(z_ref, xs_ref, b_ref, c_ref, dt_ref, tri_ref, alog_ref, exp_ref, drep_ref, nw_ref,
                 y_ref, so_ref, s_scr, *, C, NH, P, N, NG, nc):
    ci = pl.program_id(1)

    @pl.when(ci == 0)
    def _():
        s_scr[...] = jnp.zeros_like(s_scr)

    hpg = NH // NG
    gw = hpg * P
    dt = dt_ref[0]
    la = dt * (-jnp.exp(alog_ref[...]))
    cum = _sel_dot(tri_ref[...], _split3(la))
    tot = cum[C - 1:C, :]
    cum_t = cum.T
    dt_t = dt.T
    expand = exp_ref[...]
    w_state = _dot_sel(_split3(jnp.exp(tot - cum) * dt), expand)
    w_in = _dot_sel(_split3(jnp.exp(cum)), expand)
    xs = xs_ref[0]
    xsf = xs.astype(F32)
    xp = (xsf * w_state).astype(BF16)
    row = lax.broadcasted_iota(jnp.int32, (C, C), 0)
    colm = lax.broadcasted_iota(jnp.int32, (C, C), 1)
    causal = row >= colm
    lane = lax.broadcasted_iota(jnp.int32, (1, 2 * P), 1)
    head_keep = ((lane < P).astype(BF16), (lane >= P).astype(BF16))
    ys = []
    for g in range(NG):
        bg = b_ref[0, :, g * N:(g + 1) * N]
        cg = c_ref[0, :, g * N:(g + 1) * N]
        gsl = slice(g * gw, (g + 1) * gw)
        gmat = _dot_nt(cg, bg)
        state = s_scr[g]
        y_inter = _dot(cg, state.astype(BF16)) * w_in[:, gsl]
        s_scr[g] = state * w_in[C - 1:C, gsl] + _dot_tn(bg, xp[:, gsl])
        for hp in range(hpg // 2):
            h0 = g * hpg + 2 * hp
            xpair = xs[:, h0 * P:(h0 + 2) * P]
            acc = y_inter[:, 2 * hp * P:(2 * hp + 2) * P]
            for e in range(2):
                h = h0 + e
                seg = cum[:, h:h + 1] - cum_t[h:h + 1, :]
                dec = jnp.where(causal, jnp.exp(jnp.minimum(seg, 0.0)), 0.0) * dt_t[h:h + 1, :]
                m = (gmat * dec).astype(BF16)
                acc = acc + _dot(m, xpair * head_keep[e])
            ys.append(acc)
    y = jnp.concatenate(ys, axis=-1)
    outs = _ssd_gate(y, xsf, z_ref[0].astype(F32), drep_ref[...], nw_ref[...], NG)
    for g in range(NG):
        y_ref[0, :, g * gw:(g + 1) * gw] = outs[g].astype(y_ref.dtype)

    @pl.when(ci == nc - 1)
    def _():
        for h in range(NH):
            g, hl = divmod(h, hpg)
            so_ref[h] = s_scr[g, :, hl * P:(hl + 1) * P]


def _k2_ssd(z, xs, bm, cm, dt, alog_row, drep, nw, NH, P, N, NG, nl, j, prev):
    G, R, DI = xs.shape
    C = min(C_SSD, R)
    nc = R // C
    i = np.arange(C)
    tri = jnp.asarray(i[:, None] >= i[None, :], BF16)
    e = np.zeros((LANES, NH * P), np.float32)
    for h in range(NH):
        e[h, h * P:(h + 1) * P] = 1.0
    expand = jnp.asarray(e, BF16)
    gw = (NH // NG) * P
    return _call_into_stack(
        functools.partial(_k2_ssd_body, C=C, NH=NH, P=P, N=N, NG=NG, nc=nc),
        grid=(G, nc),
        inputs=[z, xs, bm, cm, dt, tri, alog_row, expand, drep, nw],
        in_specs=[_row_spec(C, DI), _row_spec(C, DI), _row_spec(C, NG * N), _row_spec(C, NG * N),
                  _row_spec(C, LANES), _const_spec(tri.shape), _const_spec(alog_row.shape),
                  _const_spec(expand.shape), _const_spec(drep.shape), _const_spec(nw.shape)],
        out_specs=[_row_spec(C, DI), _layer_state_spec((NH, N, P), j)],
        out_shape=[jax.ShapeDtypeStruct((G, R, DI), BF16), jax.ShapeDtypeStruct((nl, G, NH, N, P), F32)],
        scratch_shapes=[pltpu.VMEM((NG, N, gw), F32)],
        stack_out=1, prev=prev, name="ssd_core")


def _dec_ssd_body(xs_ref, b_ref, c_ref, dt_ref, alog_ref, s_ref, so_ref, o_ref, *, NH, P, N, NG):
    eye = lax.broadcasted_iota(jnp.int32, (N, N), 0) == lax.broadcasted_iota(jnp.int32, (N, N), 1)
    dtrow = dt_ref[0]
    arow = jnp.exp(dtrow * (-jnp.exp(alog_ref[...])))
    xrow = xs_ref[0]
    brow = b_ref[0]
    crow = c_ref[0]
    hpg = NH // NG
    per_store = LANES // P
    for g in range(NG):
        qc = _col(crow[:, g * N:(g + 1) * N], eye)
        kc = _col(brow[:, g * N:(g + 1) * N], eye)
        pieces = []
        for hl in range(hpg):
            h = g * hpg + hl
            vr = xrow[:, h * P:(h + 1) * P] * dtrow[:, h:h + 1]
            new = arow[:, h:h + 1] * s_ref[h] + kc * vr
            so_ref[h] = new
            pieces.append(jnp.sum(qc * new, axis=0, keepdims=True))
            if len(pieces) == per_store:
                lo = (h + 1) * P - LANES
                o_ref[0, :, lo:lo + LANES] = jnp.concatenate(pieces, axis=-1)
                pieces = []


def _dec_ssd(xs, bm, cm, dt, alog_row, states, NH, P, N, NG, j, prev):
    B = xs.shape[0]
    return _call_into_stack(
        functools.partial(_dec_ssd_body, NH=NH, P=P, N=N, NG=NG),
        grid=(B,),
        inputs=[_as_steps(xs), _as_steps(bm), _as_steps(cm), _as_steps(dt), alog_row, states],
        in_specs=[_step_spec(NH * P), _step_spec(NG * N), _step_spec(NG * N), _step_spec(LANES),
                  _const_spec(alog_row.shape), _layer_state_spec((NH, N, P), j)],
        out_specs=[_layer_state_spec((NH, N, P), j), _step_spec(NH * P)],
        out_shape=[jax.ShapeDtypeStruct(states.shape, F32), jax.ShapeDtypeStruct((B, 1, NH * P), F32)],
        stack_out=0, prev=prev, name="ssd_step")


def _hg_lower_bound(lbl_ref, layer):
    logits = lbl_ref[...]
    e = jnp.exp(logits - jnp.max(logits, axis=0, keepdims=True))
    sm = e / jnp.sum(e, axis=0, keepdims=True)
    lb = jnp.zeros_like(sm[0:1])
    for i in range(1, layer + 1):
        lb = lb + sm[i:i + 1]
    return lb


def _k1_hg_body(x_ref, sh_ref, sc_ref, nw_ref, w_ref, lbl_ref, *out_refs, layer, KD, VD, prompt):
    hb = _norm_mod(x_ref, sh_ref, sc_ref, nw_ref)
    lb = _hg_lower_bound(lbl_ref, layer)
    w = 512
    if prompt:
        q_ref, la_ref, v_ref, g_ref = out_refs
    else:
        q_ref, k_ref, a_ref, v_ref, g_ref = out_refs
    for c in range(KD // w):
        cols = slice(c * w, (c + 1) * w)
        q = _silu(_dot(hb, w_ref[:, c * w:(c + 1) * w])) * (HG_KEY_DIM ** -0.5)
        q_ref[0, :, cols] = q.astype(q_ref.dtype)
        lbc = lb[:, cols]
        f = lbc + (1.0 - lbc) * jax.nn.sigmoid(_dot(hb, w_ref[:, KD + c * w:KD + (c + 1) * w]))
        if prompt:
            la_ref[0, :, cols] = jnp.log(f)
        else:
            k_ref[0, :, cols] = 1.0 - f
            a_ref[0, :, cols] = f
    for c in range(VD // w):
        cols = slice(c * w, (c + 1) * w)
        v_ref[0, :, cols] = _dot(hb, w_ref[:, 2 * KD + c * w:2 * KD + (c + 1) * w]).astype(v_ref.dtype)
        g_ref[0, :, cols] = _dot(hb, w_ref[:, 2 * KD + VD + c * w:2 * KD + VD + (c + 1) * w]).astype(g_ref.dtype)


def _k1_hg(x3, mod3, nw, w, lbl, tm, layer, KD, VD, prompt):
    G, R, d = x3.shape
    if prompt:
        dts = [(KD, BF16), (KD, F32), (VD, BF16), (VD, BF16)]
    else:
        dts = [(KD, F32), (KD, F32), (KD, F32), (VD, F32), (VD, F32)]
    return pl.pallas_call(
        functools.partial(_k1_hg_body, layer=layer, KD=KD, VD=VD, prompt=prompt),
        grid=(G, R // tm),
        in_specs=_k1_specs(x3, mod3, 0, 1, tm) + [_const_spec(w.shape), _const_spec(lbl.shape)],
        out_specs=[_row_spec(tm, n) for n, _ in dts],
        out_shape=[jax.ShapeDtypeStruct((G, R, n), dt) for n, dt in dts],
        compiler_params=_params(("arbitrary", "arbitrary")),
        name="hg_inproj" if prompt else "hg_inproj_step",
    )(x3, mod3, mod3, nw, w, lbl)


def _hg_masks(C):
    i = np.arange(C)
    r, t = i[:, None], i[None, :]
    sels = [t <= r, (t <= r) & (t >= HG_SUB * (r // HG_SUB))]
    pairs = []
    s = C // 2
    while s >= HG_SUB:
        mid = 2 * s * (r // (2 * s)) + s - 1
        sels.append(np.where(r > mid, (t > mid) & (t <= r), (t > r) & (t <= mid)))
        pairs.append((r // (2 * s) == t // (2 * s)) & (r % (2 * s) >= s) & (t % (2 * s) < s))
        s //= 2
    pairs.append((r // HG_SUB == t // HG_SUB) & (t <= r))
    return (jnp.asarray(np.stack(sels), BF16), jnp.asarray(np.stack(pairs), F32))


def _hg_diag_collect(C, K):
    e = np.zeros((HG_SUB, K, C), np.float32)
    for j in range(HG_SUB):
        e[j, :, j::HG_SUB] = 1.0
    return jnp.asarray(e.reshape(HG_SUB * K, C), BF16)


def _k2_hg_body(q_ref, la_ref, v_ref, g_ref, sel_ref, pair_ref, coll_ref, nw_ref, y_ref, so_ref,
                st_scr, k_scr, c_scr, p_scr, *, C, H, K, V, nlev, nc):
    ci = pl.program_id(1)

    @pl.when(ci == 0)
    def _():
        st_scr[...] = jnp.zeros_like(st_scr)

    D = H * K
    la = la_ref[0]
    parts = _split3(la)
    cum = _sel_dot(sel_ref[0], parts)
    cw = _sel_dot(sel_ref[1], parts)
    kf = 1.0 - jnp.exp(la)
    qf = q_ref[0].astype(F32)
    vb = v_ref[0]
    tot = cum[C - 1:C, :]
    q_in = (qf * jnp.exp(cum)).astype(BF16)
    k_out = (kf * jnp.exp(tot - cum)).astype(BF16)
    e_tot = jnp.exp(tot)

    scores = [jnp.zeros((C, C), F32) for _ in range(H)]
    for lev in range(nlev):
        a = jnp.exp(_sel_dot(sel_ref[2 + lev], parts))
        ql = (qf * a).astype(BF16)
        kl = (kf * a).astype(BF16)
        pm = pair_ref[lev] > 0.5
        for h in range(H):
            sl = slice(h * K, (h + 1) * K)
            scores[h] = scores[h] + jnp.where(pm, _dot_nt(ql[:, sl], kl[:, sl]), 0.0)

    log2e = 1.4426950408889634
    cw2 = cw * log2e
    k_scr[...] = kf
    c_scr[...] = cw2
    rowmod = lax.broadcasted_iota(jnp.int32, (C, D), 0) & (HG_SUB - 1)
    nblk = C // HG_SUB
    for j in range(HG_SUB):
        kb = jnp.concatenate([jnp.broadcast_to(k_scr[b * HG_SUB + j:b * HG_SUB + j + 1, :], (HG_SUB, D))
                              for b in range(nblk)], axis=0)
        cb = jnp.concatenate([jnp.broadcast_to(c_scr[b * HG_SUB + j:b * HG_SUB + j + 1, :], (HG_SUB, D))
                              for b in range(nblk)], axis=0)
        dec = jnp.exp2(jnp.where(rowmod >= j, cw2 - cb, -1e30))
        p = (qf * kb * dec).astype(BF16)
        for h in range(H):
            p_scr[h * C:(h + 1) * C, j * K:(j + 1) * K] = p[:, h * K:(h + 1) * K]
    diag = _dot(p_scr[...], coll_ref[...])
    dm = pair_ref[nlev] > 0.5

    nw = nw_ref[...]
    for h in range(H):
        sl = slice(h * K, (h + 1) * K)
        vsl = slice(h * V, (h + 1) * V)
        st = st_scr[h]
        sc = scores[h] + jnp.where(dm, diag[h * C:(h + 1) * C, :], 0.0)
        o = _dot(sc.astype(BF16), vb[:, vsl]) + _dot_nt(q_in[:, sl], st.astype(BF16))
        st_scr[h] = st * e_tot[:, sl] + _dot_tn(vb[:, vsl], k_out[:, sl])
        gh = g_ref[0, :, vsl].astype(F32)
        y_ref[0, :, vsl] = (_rms(o) * nw[:, vsl] * _silu(gh)).astype(y_ref.dtype)

    @pl.when(ci == nc - 1)
    def _():
        for h in range(H):
            so_ref[h] = st_scr[h].T


def _k2_hg(q, la, v, g, nw_rep, H, K, V, nl, j, prev):
    G, R, D = q.shape
    C = min(C_HG, R)
    nc = R // C
    sels, pairs = _hg_masks(C)
    nlev = pairs.shape[0] - 1
    coll = _hg_diag_collect(C, K)
    return _call_into_stack(
        functools.partial(_k2_hg_body, C=C, H=H, K=K, V=V, nlev=nlev, nc=nc),
        grid=(G, nc),
        inputs=[q, la, v, g, sels, pairs, coll, nw_rep],
        in_specs=[_row_spec(C, D), _row_spec(C, D), _row_spec(C, H * V), _row_spec(C, H * V),
                  _const_spec(sels.shape), _const_spec(pairs.shape), _const_spec(coll.shape),
                  _const_spec(nw_rep.shape)],
        out_specs=[_row_spec(C, H * V), _layer_state_spec((H, K, V), j)],
        out_shape=[jax.ShapeDtypeStruct((G, R, H * V), BF16), jax.ShapeDtypeStruct((nl, G, H, K, V), F32)],
        scratch_shapes=[pltpu.VMEM((H, V, K), F32), pltpu.VMEM((C, D), F32), pltpu.VMEM((C, D), F32),
                        pltpu.VMEM((H * C, HG_SUB * K), BF16)],
        stack_out=1, prev=prev, name="hg_core")


def _dec_hg_body(q_ref, k_ref, a_ref, v_ref, s_ref, so_ref, o_ref, *, H, K, V):
    eye = lax.broadcasted_iota(jnp.int32, (K, K), 0) == lax.broadcasted_iota(jnp.int32, (K, K), 1)
    qrow = q_ref[0]
    krow = k_ref[0]
    arow = a_ref[0]
    vrow = v_ref[0]
    for h in range(H):
        sl = slice(h * K, (h + 1) * K)
        qc = _col(qrow[:, sl], eye)
        kc = _col(krow[:, sl], eye)
        ac = _col(arow[:, sl], eye)
        new = ac * s_ref[h] + kc * vrow[:, h * V:(h + 1) * V]
        so_ref[h] = new
        o_ref[0, :, h * V:(h + 1) * V] = jnp.sum(qc * new, axis=0, keepdims=True)


def _dec_hg(q, k, a, v, states, H, K, V, j, prev):
    B = q.shape[0]
    return _call_into_stack(
        functools.partial(_dec_hg_body, H=H, K=K, V=V),
        grid=(B,),
        inputs=[_as_steps(q), _as_steps(k), _as_steps(a), _as_steps(v), states],
        in_specs=[_step_spec(H * K), _step_spec(H * K), _step_spec(H * K), _step_spec(H * V),
                  _layer_state_spec((H, K, V), j)],
        out_specs=[_layer_state_spec((H, K, V), j), _step_spec(H * V)],
        out_shape=[jax.ShapeDtypeStruct(states.shape, F32), jax.ShapeDtypeStruct((B, 1, H * V), F32)],
        stack_out=0, prev=prev, name="hg_step")


def _pro_identity(y_ref):
    return y_ref[0]


def _pro_ret(o_ref, g_ref, *, H, V):
    return jnp.concatenate(_ret_gate(o_ref[0], g_ref[0], H, V), axis=-1).astype(BF16)


def _pro_ssd(o_ref, xs_ref, z_ref, drep_ref, nw_ref, *, NG):
    return jnp.concatenate(_ssd_gate(o_ref[0], xs_ref[0], z_ref[0], drep_ref[...], nw_ref[...], NG),
                           axis=-1).astype(BF16)


def _pro_hg(o_ref, g_ref, nw_ref, *, H, V):
    return jnp.concatenate(_hg_gate(o_ref[0], g_ref[0], nw_ref[...], H, V), axis=-1).astype(BF16)


def _k3_body(*refs, prologue, n_pro):
    w_ref, x_ref, gate_ref, nw_ref, o_ref = refs[n_pro:]
    y = prologue(*refs[:n_pro])
    o = _dot(y, w_ref[...])
    o_ref[0] = x_ref[0] + gate_ref[0] * (_rms(o) * nw_ref[...])


def _k3(prologue, pro_args, pro_specs, w, x3, mod3, gate_idx, nw, tm):
    G, R, d = x3.shape
    rm = mod3.shape[1]
    n_pro = len(pro_args)
    return pl.pallas_call(
        functools.partial(_k3_body, prologue=prologue, n_pro=n_pro),
        grid=(G, R // tm),
        in_specs=list(pro_specs) + [
            _const_spec(w.shape),
            pl.BlockSpec((1, tm, d), lambda g, t: (g, t, 0)),
            pl.BlockSpec((1, rm, d), lambda g, t: (g, 0, gate_idx)),
            _const_spec((1, d)),
        ],
        out_specs=pl.BlockSpec((1, tm, d), lambda g, t: (g, t, 0)),
        out_shape=jax.ShapeDtypeStruct((G, R, d), F32),
        compiler_params=_params(("arbitrary", "arbitrary")),
        name="outproj",
    )(*pro_args, w, x3, mod3, nw)


def _k4_body(x_ref, sh_ref, sc_ref, nw_ref, gate_ref, nwp_ref, win_ref, wout_ref, o_ref, *, F, bounds):
    x = x_ref[0]
    hb = _norm_mod(x_ref, sh_ref, sc_ref, nw_ref)
    acc = jnp.zeros(x.shape, F32)
    for lo, hi in bounds:
        gt = _dot(hb, win_ref[:, lo:hi])
        up = _dot(hb, win_ref[:, F + lo:F + hi])
        a = (_silu(gt) * up).astype(BF16)
        acc = acc + _dot(a, wout_ref[lo:hi, :])
    o_ref[0] = x + gate_ref[0] * (_rms(acc) * nwp_ref[...])


def _ffn_bounds(F):
    tiles, rem = divmod(F, MXU_TILE)
    assert rem == 0, F
    n = -(-tiles // FFN_MAX_CHUNK_TILES)
    sizes = [tiles // n + (1 if i < tiles % n else 0) for i in range(n)]
    edges = np.cumsum([0] + sizes) * MXU_TILE
    return tuple((int(a), int(b)) for a, b in zip(edges[:-1], edges[1:]))


def _k4(x3, mod3, nw_pre, nw_post, win, wout, tm):
    G, R, d = x3.shape
    rm = mod3.shape[1]
    F = wout.shape[0]
    return pl.pallas_call(
        functools.partial(_k4_body, F=F, bounds=_ffn_bounds(F)),
        grid=(G, R // tm),
        in_specs=_k1_specs(x3, mod3, 3, 4, tm) + [
            pl.BlockSpec((1, rm, d), lambda g, t: (g, 0, 5)),
            _const_spec((1, d)), _const_spec(win.shape), _const_spec(wout.shape)],
        out_specs=pl.BlockSpec((1, tm, d), lambda g, t: (g, t, 0)),
        out_shape=jax.ShapeDtypeStruct((G, R, d), F32),
        compiler_params=_params(("arbitrary", "arbitrary")),
        name="ffn",
    )(x3, mod3, mod3, nw_pre, mod3, nw_post, win, wout)


def _rope_table(pos, half):
    inv = 1.0 / (ROPE_BASE ** (jnp.arange(half, dtype=F32) / half))
    ang = pos.astype(F32)[:, None] * inv[None, :]
    return jnp.cos(ang), jnp.sin(ang)


def kernel(x_prompt, x_sample, c_prompt, c_sample, state_ret, state_ssd, state_conv, state_hgrn, w_ada, b_ada, norm_mix_pre, norm_mix_post, norm_ffn_pre, norm_ffn_post, ret_w_in, ret_w_out, ssd_w_in, ssd_conv_w, ssd_conv_b, ssd_dt_bias, ssd_a_log, ssd_d, ssd_norm, ssd_w_out, hg_w_in, hg_lb_logits, hg_norm, hg_w_out, ffn_w_in, ffn_w_out):
    bp, seq, d = x_prompt.shape
    bs = x_sample.shape[0]
    depth = w_ada.shape[0]
    ret_h = d // 256
    ret_k = d // ret_h
    ret_v = 2 * ret_k
    di = 2 * d
    ssd_nh = di // SSD_HEAD_DIM
    gn = SSD_GROUPS * SSD_STATE
    hg_h = d // HG_KEY_DIM
    hg_v = d // hg_h
    kd = hg_h * HG_KEY_DIM
    vd = hg_h * hg_v
    tm = min(TM_PROMPT, seq)

    mod = _adaln(jnp.concatenate([c_prompt, c_sample], axis=0), w_ada, b_ada)
    xp = x_prompt.astype(F32)
    xs = x_sample.astype(F32).reshape(1, bs, d)
    cos_p, sin_p = _rope_table(jnp.arange(seq, dtype=jnp.int32), ret_k // 2)
    cos_s, sin_s = _rope_table(jnp.full((1,), PAST_LEN, jnp.int32), ret_k // 2)

    new = {k: [] for k in ("conv_p", "conv_s")}
    stk = {k: None for k in ("ret_p", "ret_s", "ssd_p", "ssd_s", "hg_p", "hg_s")}
    n_kind = [sum(1 for l in range(depth) if l % 3 == kind) for kind in range(3)]
    counts = [0, 0, 0]
    for layer in range(depth):
        mod_p = mod[layer, :bp].reshape(bp, 1, 6 * d)
        mod_s = mod[layer, bp:].reshape(1, bs, 6 * d)
        nw_pre = norm_mix_pre[layer].reshape(1, d)
        nw_post = norm_mix_post[layer].reshape(1, d)
        kind = layer % 3
        j = counts[kind]
        counts[kind] += 1
        y_spec = lambda n: [_row_spec(tm, n)]
        s_spec = lambda n: _row_spec(bs, n)
        if kind == 0:
            w_in = ret_w_in[j].astype(BF16)
            w_out = ret_w_out[j].astype(BF16)
            q, k, v, g = _k1_ret(xp, mod_p, nw_pre, w_in, cos_p, sin_p, tm, BF16, ret_h, ret_k, ret_v)
            y, stk["ret_p"] = _k2_ret(q, k, v, g, ret_h, ret_k, ret_v, n_kind[0], j, stk["ret_p"])
            xp = _k3(_pro_identity, [y], y_spec(ret_h * ret_v), w_out, xp, mod_p, 2, nw_post, tm)
            q, k, v, g = _k1_ret(xs, mod_s, nw_pre, w_in, cos_s, sin_s, bs, F32, ret_h, ret_k, ret_v)
            stk["ret_s"], o = _dec_ret(q[0], k[0], v[0], state_ret, ret_h, ret_k, ret_v, j, stk["ret_s"])
            xs = _k3(functools.partial(_pro_ret, H=ret_h, V=ret_v), [o.reshape(1, bs, -1), g],
                     [s_spec(ret_h * ret_v)] * 2, w_out, xs, mod_s, 2, nw_post, bs)
        elif kind == 1:
            w_main = ssd_w_in[j][:, :di + di + 2 * gn].astype(BF16)
            w_dt = jnp.pad(ssd_w_in[j][:, di + di + 2 * gn:], ((0, 0), (0, LANES - ssd_nh))).astype(BF16)
            w_out = ssd_w_out[j].astype(BF16)
            cw = ssd_conv_w[j]
            cb = ssd_conv_b[j].reshape(1, -1)
            dtb = jnp.pad(ssd_dt_bias[j], (0, LANES - ssd_nh)).reshape(1, LANES)
            alog = jnp.pad(ssd_a_log[j].astype(F32), (0, LANES - ssd_nh)).reshape(1, LANES)
            drep = jnp.repeat(ssd_d[j], SSD_HEAD_DIM).reshape(1, di)
            nw_ssd = ssd_norm[j].reshape(1, di)
            z, xc, bm, cm, dt, nconv = _k1_ssd_prompt(xp, mod_p, nw_pre, w_main, w_dt, cw, cb, dtb, tm, di, gn)
            new["conv_p"].append(nconv)
            y, stk["ssd_p"] = _k2_ssd(z, xc, bm, cm, dt, alog, drep, nw_ssd, ssd_nh, SSD_HEAD_DIM, SSD_STATE,
                                      SSD_GROUPS, n_kind[1], j, stk["ssd_p"])
            xp = _k3(_pro_identity, [y], y_spec(di), w_out, xp, mod_p, 2, nw_post, tm)
            buf_t = jnp.transpose(state_conv[j].astype(F32), (1, 0, 2))
            z, xc, bm, cm, dt, nconv = _k1_ssd_sample(xs, mod_s, nw_pre, w_main, w_dt, cw, cb, dtb, buf_t, di, gn)
            new["conv_s"].append(jnp.transpose(nconv, (1, 0, 2)))
            stk["ssd_s"], o = _dec_ssd(xc[0], bm[0], cm[0], dt[0], alog, state_ssd,
                                       ssd_nh, SSD_HEAD_DIM, SSD_STATE, SSD_GROUPS, j, stk["ssd_s"])
            xs = _k3(functools.partial(_pro_ssd, NG=SSD_GROUPS), [o.reshape(1, bs, -1), xc, z, drep, nw_ssd],
                     [s_spec(di)] * 3 + [_const_spec((1, di))] * 2, w_out, xs, mod_s, 2, nw_post, bs)
        else:
            w_in = hg_w_in[j].astype(BF16)
            w_out = hg_w_out[j].astype(BF16)
            nw_hg = jnp.tile(hg_norm[j], hg_h).reshape(1, vd)
            lbl = hg_lb_logits.astype(F32)
            q, la, v, g = _k1_hg(xp, mod_p, nw_pre, w_in, lbl, tm, layer, kd, vd, True)
            y, stk["hg_p"] = _k2_hg(q, la, v, g, nw_hg, hg_h, HG_KEY_DIM, hg_v, n_kind[2], j, stk["hg_p"])
            xp = _k3(_pro_identity, [y], y_spec(vd), w_out, xp, mod_p, 2, nw_post, tm)
            q, k, a, v, g = _k1_hg(xs, mod_s, nw_pre, w_in, lbl, bs, layer, kd, vd, False)
            stk["hg_s"], o = _dec_hg(q[0], k[0], a[0], v[0], state_hgrn, hg_h, HG_KEY_DIM, hg_v, j, stk["hg_s"])
            xs = _k3(functools.partial(_pro_hg, H=hg_h, V=hg_v), [o.reshape(1, bs, -1), g, nw_hg],
                     [s_spec(vd)] * 2 + [_const_spec((1, vd))], w_out, xs, mod_s, 2, nw_post, bs)
        nf_pre = norm_ffn_pre[layer].reshape(1, d)
        nf_post = norm_ffn_post[layer].reshape(1, d)
        f_in = ffn_w_in[layer].astype(BF16)
        f_out = ffn_w_out[layer].astype(BF16)
        xp = _k4(xp, mod_p, nf_pre, nf_post, f_in, f_out, tm)
        xs = _k4(xs, mod_s, nf_pre, nf_post, f_in, f_out, bs)

    stack = lambda name, like: jnp.stack(new[name]).astype(like.dtype)
    return (xp.astype(x_prompt.dtype), xs.reshape(bs, 1, d).astype(x_sample.dtype),
            stk["ret_p"].astype(state_ret.dtype), stk["ret_s"].astype(state_ret.dtype),
            stk["ssd_p"].astype(state_ssd.dtype), stk["ssd_s"].astype(state_ssd.dtype),
            stack("conv_p", state_conv), stack("conv_s", state_conv),
            stk["hg_p"].astype(state_hgrn.dtype), stk["hg_s"].astype(state_hgrn.dtype))
```

```python
import functools

import numpy as np
import jax
import jax.numpy as jnp
from jax import lax
from jax.experimental import pallas as pl
from jax.experimental.pallas import tpu as pltpu

F32 = jnp.float32
BF16 = jnp.bfloat16
EPS = 1e-6
PAST_LEN = 16384
ROPE_BASE = 10000.0
SSD_HEAD_DIM = 64
SSD_GROUPS = 4
SSD_STATE = 128
SSD_CONV = 4
HG_KEY_DIM = 128
LANES = 128
SUBLANES = 8
VMEM_LIMIT = 56 * 1024 * 1024
TM_PROMPT = 512
MXU_TILE = 256
MXU_PAD_ROWS = 16
C_RET = 256
C_SSD = 128
C_HG = 128
HG_SUB = 8
FFN_MAX_CHUNK_TILES = 6


def _rms(x):
    return x * lax.rsqrt(jnp.mean(x * x, axis=-1, keepdims=True) + EPS)


def _silu(x):
    return x * jax.nn.sigmoid(x)


def _softplus(x):
    return jnp.maximum(x, 0.0) + jnp.log1p(jnp.exp(-jnp.abs(x)))


def _dot(a, b):
    return jnp.dot(a, b, preferred_element_type=F32)


def _dot_nt(a, b):
    return lax.dot_general(a, b, (((1,), (1,)), ((), ())), preferred_element_type=F32)


def _dot_tn(a, b):
    return lax.dot_general(a, b, (((0,), (0,)), ((), ())), preferred_element_type=F32)


def _split3(x):
    hi = x.astype(BF16)
    r = x - hi.astype(F32)
    mid = r.astype(BF16)
    lo = (r - mid.astype(F32)).astype(BF16)
    return hi, mid, lo


def _sel_dot(sel, parts):
    acc = _dot(sel, parts[0])
    for p in parts[1:]:
        acc = acc + _dot(sel, p)
    return acc


def _dot_sel(parts, sel):
    acc = _dot(parts[0], sel)
    for p in parts[1:]:
        acc = acc + _dot(p, sel)
    return acc


def _col(row, eye):
    return jnp.sum(jnp.where(eye, row, 0.0), axis=1, keepdims=True)


def _norm_mod(x_ref, sh_ref, sc_ref, nw_ref):
    x = x_ref[0]
    h = _rms(x) * nw_ref[...] * (1.0 + sc_ref[0]) + sh_ref[0]
    return h.astype(BF16)


def _params(sem):
    return pltpu.CompilerParams(dimension_semantics=sem, vmem_limit_bytes=VMEM_LIMIT)


def _const_spec(shape):
    nd = len(shape)
    return pl.BlockSpec(shape, lambda *_: (0,) * nd, pipeline_mode=pl.Buffered(1))


def _ret_gate(o, g, H, V):
    outs = []
    for h in range(H):
        outs.append(_rms(o[:, h * V:(h + 1) * V]) * _silu(g[:, h * V:(h + 1) * V]))
    return outs


def _ssd_gate(y, xs, z, d_rep, nw, groups):
    y = (y + d_rep * xs) * _silu(z)
    w = y.shape[-1] // groups
    outs = []
    for g in range(groups):
        outs.append(_rms(y[:, g * w:(g + 1) * w]) * nw[:, g * w:(g + 1) * w])
    return outs


def _hg_gate(o, g, nw, H, V):
    outs = []
    for h in range(H):
        sl = slice(h * V, (h + 1) * V)
        outs.append(_rms(o[:, sl]) * nw[:, sl] * _silu(g[:, sl]))
    return outs


def _adaln_body(c_ref, w_ref, b_ref, o_ref):
    a = _silu(c_ref[...]).astype(BF16)
    o_ref[0] = _dot(a, w_ref[0].astype(BF16)) + b_ref[0]


def _adaln(c_all, w_ada, b_ada):
    depth, d, n = w_ada.shape
    bc = c_all.shape[0]
    tn = 1024
    return pl.pallas_call(
        _adaln_body,
        grid=(depth, n // tn),
        in_specs=[
            pl.BlockSpec((bc, d), lambda l, j: (0, 0)),
            pl.BlockSpec((1, d, tn), lambda l, j: (l, 0, j)),
            pl.BlockSpec((1, 1, tn), lambda l, j: (l, 0, j)),
        ],
        out_specs=pl.BlockSpec((1, bc, tn), lambda l, j: (l, 0, j)),
        out_shape=jax.ShapeDtypeStruct((depth, bc, n), F32),
        compiler_params=_params(("arbitrary", "arbitrary")),
        name="adaln",
    )(c_all, w_ada, b_ada.reshape(depth, 1, n))


def _k1_specs(x3, mod3, sh_idx, sc_idx, tm):
    _, _, d = x3.shape
    rm = mod3.shape[1]
    return [
        pl.BlockSpec((1, tm, d), lambda g, t: (g, t, 0)),
        pl.BlockSpec((1, rm, d), lambda g, t: (g, 0, sh_idx)),
        pl.BlockSpec((1, rm, d), lambda g, t: (g, 0, sc_idx)),
        _const_spec((1, d)),
    ]


def _row_spec(tm, n):
    return pl.BlockSpec((1, tm, n), lambda g, t: (g, t, 0))


def _layer_state_spec(tail, j):
    nd = len(tail)
    return pl.BlockSpec((None, None) + tuple(tail), lambda b, *_: (j, b) + (0,) * nd)


def _call_into_stack(body, *, grid, inputs, in_specs, out_specs, out_shape, stack_out, prev, name,
                     scratch_shapes=()):
    inputs, in_specs = list(inputs), list(in_specs)
    n_in = len(inputs)
    aliases = {}
    kernel_fn = body
    if prev is not None:
        inputs.append(prev)
        in_specs.append(pl.BlockSpec(memory_space=pl.ANY))
        aliases = {n_in: stack_out}

        def kernel_fn(*refs):
            body(*refs[:n_in], *refs[n_in + 1:])

    return pl.pallas_call(
        kernel_fn, grid=grid, in_specs=in_specs, out_specs=out_specs, out_shape=out_shape,
        scratch_shapes=scratch_shapes, input_output_aliases=aliases,
        compiler_params=_params(("arbitrary",) * len(grid)), name=name)(*inputs)


def _k1_ret_body(x_ref, sh_ref, sc_ref, nw_ref, w_ref, cos_ref, sin_ref,
                 q_ref, k_ref, v_ref, g_ref, *, H, K, V):
    hb = _norm_mod(x_ref, sh_ref, sc_ref, nw_ref)
    cos = cos_ref[...]
    sin = sin_ref[...]
    half = K // 2
    for h in range(H):
        for dst, base, scale in ((q_ref, 0, None), (k_ref, H * K, K ** -0.5)):
            a = _dot(hb, w_ref[:, base + h * K:base + (h + 1) * K])
            x1 = a[:, :half]
            x2 = a[:, half:]
            r1 = x1 * cos - x2 * sin
            r2 = x2 * cos + x1 * sin
            if scale is not None:
                r1 = r1 * scale
                r2 = r2 * scale
            dst[0, :, h * K:h * K + half] = r1.astype(dst.dtype)
            dst[0, :, h * K + half:(h + 1) * K] = r2.astype(dst.dtype)
    vb = 2 * H * K
    for c in range(H):
        v_ref[0, :, c * V:(c + 1) * V] = _dot(hb, w_ref[:, vb + c * V:vb + (c + 1) * V]).astype(v_ref.dtype)
        g_ref[0, :, c * V:(c + 1) * V] = _dot(
            hb, w_ref[:, vb + H * V + c * V:vb + H * V + (c + 1) * V]).astype(g_ref.dtype)


def _k1_ret(x3, mod3, nw, w, cos, sin, tm, out_dtype, H, K, V):
    G, R, d = x3.shape
    rc = cos.shape[0]
    cs_spec = (pl.BlockSpec((tm, K // 2), lambda g, t: (t, 0)) if rc == R
               else _const_spec((1, K // 2)))
    outs = [jax.ShapeDtypeStruct((G, R, H * K), out_dtype), jax.ShapeDtypeStruct((G, R, H * K), out_dtype),
            jax.ShapeDtypeStruct((G, R, H * V), out_dtype), jax.ShapeDtypeStruct((G, R, H * V), out_dtype)]
    return pl.pallas_call(
        functools.partial(_k1_ret_body, H=H, K=K, V=V),
        grid=(G, R // tm),
        in_specs=_k1_specs(x3, mod3, 0, 1, tm) + [_const_spec(w.shape), cs_spec, cs_spec],
        out_specs=[_row_spec(tm, H * K), _row_spec(tm, H * K), _row_spec(tm, H * V), _row_spec(tm, H * V)],
        out_shape=outs,
        compiler_params=_params(("arbitrary", "arbitrary")),
        name="ret_inproj",
    )(x3, mod3, mod3, nw, w, cos, sin)


def _k2_ret_body(q_ref, k_ref, v_ref, g_ref, dmat_ref, qdec_ref, kdec_ref, y_ref, s_ref,
                 *, H, K, V, sdec):
    @pl.when(pl.program_id(1) == 0)
    def _():
        s_ref[...] = jnp.zeros_like(s_ref)

    for h in range(H):
        qh = q_ref[0, :, h * K:(h + 1) * K]
        kh = k_ref[0, :, h * K:(h + 1) * K]
        vh = v_ref[0, :, h * V:(h + 1) * V]
        state = s_ref[h]
        scores = _dot_nt(qh, kh) * dmat_ref[h]
        o = _dot(scores.astype(BF16), vh) + _dot(qh, state.astype(BF16)) * qdec_ref[h]
        k_out = (kh.astype(F32) * kdec_ref[h]).astype(BF16)
        s_ref[h] = sdec[h] * state + _dot_tn(k_out, vh)
        gh = g_ref[0, :, h * V:(h + 1) * V].astype(F32)
        y_ref[0, :, h * V:(h + 1) * V] = (_rms(o) * _silu(gh)).astype(y_ref.dtype)


def _ret_tables(H, K, V, C):
    lg = np.log1p(-np.exp2(-5.0 - np.arange(H, dtype=np.float64)))
    i = np.arange(C, dtype=np.float64)
    diff = i[:, None] - i[None, :]
    dmat = np.where(diff >= 0, np.exp(lg[:, None, None] * np.maximum(diff, 0.0)), 0.0)
    qdec = np.broadcast_to(np.exp(lg[:, None, None] * (i[None, :, None] + 1.0)), (H, C, V))
    kdec = np.broadcast_to(np.exp(lg[:, None, None] * (C - 1.0 - i[None, :, None])), (H, C, K))
    sdec = tuple(float(np.exp(l * C)) for l in lg)
    return (jnp.asarray(dmat, F32), jnp.asarray(qdec, F32), jnp.asarray(kdec, F32), sdec)


def _k2_ret(q, k, v, g, H, K, V, nl, j, prev):
    G, R, _ = q.shape
    C = min(C_RET, R)
    dmat, qdec, kdec, sdec = _ret_tables(H, K, V, C)
    return _call_into_stack(
        functools.partial(_k2_ret_body, H=H, K=K, V=V, sdec=sdec),
        grid=(G, R // C),
        inputs=[q, k, v, g, dmat, qdec, kdec],
        in_specs=[_row_spec(C, H * K), _row_spec(C, H * K), _row_spec(C, H * V), _row_spec(C, H * V),
                  _const_spec(dmat.shape), _const_spec(qdec.shape), _const_spec(kdec.shape)],
        out_specs=[_row_spec(C, H * V), _layer_state_spec((H, K, V), j)],
        out_shape=[jax.ShapeDtypeStruct((G, R, H * V), BF16), jax.ShapeDtypeStruct((nl, G, H, K, V), F32)],
        stack_out=1, prev=prev, name="ret_core")


def _dec_ret_body(q_ref, k_ref, v_ref, s_ref, so_ref, o_ref, *, H, K, V, gammas):
    eye = lax.broadcasted_iota(jnp.int32, (K, K), 0) == lax.broadcasted_iota(jnp.int32, (K, K), 1)
    qrow = q_ref[0]
    krow = k_ref[0]
    vrow = v_ref[0]
    for h in range(H):
        qc = _col(qrow[:, h * K:(h + 1) * K], eye)
        kc = _col(krow[:, h * K:(h + 1) * K], eye)
        new = gammas[h] * s_ref[h] + kc * vrow[:, h * V:(h + 1) * V]
        so_ref[h] = new
        o_ref[0, :, h * V:(h + 1) * V] = jnp.sum(qc * new, axis=0, keepdims=True)


def _step_spec(n):
    return pl.BlockSpec((1, 1, n), lambda b: (b, 0, 0))


def _as_steps(x):
    return x.reshape(x.shape[0], 1, x.shape[1])


def _dec_ret(q, k, v, states, H, K, V, j, prev):
    B = q.shape[0]
    gammas = tuple(float(1.0 - 2.0 ** (-5 - h)) for h in range(H))
    return _call_into_stack(
        functools.partial(_dec_ret_body, H=H, K=K, V=V, gammas=gammas),
        grid=(B,),
        inputs=[_as_steps(q), _as_steps(k), _as_steps(v), states],
        in_specs=[_step_spec(H * K), _step_spec(H * K), _step_spec(H * V), _layer_state_spec((H, K, V), j)],
        out_specs=[_layer_state_spec((H, K, V), j), _step_spec(H * V)],
        out_shape=[jax.ShapeDtypeStruct(states.shape, F32), jax.ShapeDtypeStruct((B, 1, H * V), F32)],
        stack_out=0, prev=prev, name="ret_step")


def _ssd_route(c, DI, xs_ref, b_ref, c_ref):
    w = 512
    if c * w < DI:
        return xs_ref, c * w
    if c * w < DI + b_ref.shape[-1]:
        return b_ref, c * w - DI
    return c_ref, c * w - DI - b_ref.shape[-1]


def _k1_ssd_prompt_body(x_ref, sh_ref, sc_ref, nw_ref, w_ref, wdt_ref, cw_ref, cb_ref, dtb_ref,
                        z_ref, xs_ref, b_ref, c_ref, dt_ref, nconv_ref, cbuf, *, tm, DI, CD, nt):
    t = pl.program_id(1)
    hb = _norm_mod(x_ref, sh_ref, sc_ref, nw_ref)
    w = 512
    for c in range(DI // w):
        z_ref[0, :, c * w:(c + 1) * w] = _dot(hb, w_ref[:, c * w:(c + 1) * w]).astype(z_ref.dtype)
    dt_ref[0] = _softplus(_dot(hb, wdt_ref[...]) + dtb_ref[...])

    @pl.when(t == 0)
    def _():
        cbuf[0:SUBLANES, :] = jnp.zeros((SUBLANES, CD), F32)

    for c in range(CD // w):
        cols = slice(c * w, (c + 1) * w)
        raw = _dot(hb, w_ref[:, DI + c * w:DI + (c + 1) * w])
        cbuf[SUBLANES:SUBLANES + tm, cols] = raw
        conv = cb_ref[:, cols] + cw_ref[3:4, cols] * raw
        for k in range(SSD_CONV - 1):
            conv = conv + cw_ref[k:k + 1, cols] * cbuf[SUBLANES - 3 + k:SUBLANES - 3 + k + tm, cols]
        dst, off = _ssd_route(c, DI, xs_ref, b_ref, c_ref)
        dst[0, :, off:off + w] = _silu(conv).astype(dst.dtype)

    @pl.when(t == nt - 1)
    def _():
        nconv_ref[0] = cbuf[tm + SUBLANES - 3:tm + SUBLANES, :]

    cbuf[0:SUBLANES, :] = cbuf[tm:tm + SUBLANES, :]


def _k1_ssd_prompt(x3, mod3, nw, w, wdt, cw, cb, dtb, tm, DI, GN):
    G, R, d = x3.shape
    CD = DI + 2 * GN
    nt = R // tm
    outs = [jax.ShapeDtypeStruct((G, R, DI), BF16), jax.ShapeDtypeStruct((G, R, DI), BF16),
            jax.ShapeDtypeStruct((G, R, GN), BF16), jax.ShapeDtypeStruct((G, R, GN), BF16),
            jax.ShapeDtypeStruct((G, R, LANES), F32), jax.ShapeDtypeStruct((G, SSD_CONV - 1, CD), F32)]
    return pl.pallas_call(
        functools.partial(_k1_ssd_prompt_body, tm=tm, DI=DI, CD=CD, nt=nt),
        grid=(G, nt),
        in_specs=_k1_specs(x3, mod3, 0, 1, tm) + [_const_spec(w.shape), _const_spec(wdt.shape),
                                                  _const_spec(cw.shape), _const_spec(cb.shape),
                                                  _const_spec(dtb.shape)],
        out_specs=[_row_spec(tm, DI), _row_spec(tm, DI), _row_spec(tm, GN), _row_spec(tm, GN),
                   _row_spec(tm, LANES), pl.BlockSpec((1, SSD_CONV - 1, CD), lambda g, t: (g, 0, 0))],
        out_shape=outs,
        scratch_shapes=[pltpu.VMEM((tm + SUBLANES, CD), F32)],
        compiler_params=_params(("arbitrary", "arbitrary")),
        name="ssd_inproj",
    )(x3, mod3, mod3, nw, w, wdt, cw, cb, dtb)


def _k1_ssd_sample_body(x_ref, sh_ref, sc_ref, nw_ref, w_ref, wdt_ref, cw_ref, cb_ref, dtb_ref, buf_ref,
                        z_ref, xs_ref, b_ref, c_ref, dt_ref, nconv_ref, *, DI, CD):
    hb = _norm_mod(x_ref, sh_ref, sc_ref, nw_ref)
    w = 512
    for c in range(DI // w):
        z_ref[0, :, c * w:(c + 1) * w] = _dot(hb, w_ref[:, c * w:(c + 1) * w])
    dt_ref[0] = _softplus(_dot(hb, wdt_ref[...]) + dtb_ref[...])
    for c in range(CD // w):
        cols = slice(c * w, (c + 1) * w)
        raw = _dot(hb, w_ref[:, DI + c * w:DI + (c + 1) * w])
        conv = cb_ref[:, cols] + cw_ref[3:4, cols] * raw
        for k in range(SSD_CONV - 1):
            conv = conv + cw_ref[k:k + 1, cols] * buf_ref[k, :, cols]
        dst, off = _ssd_route(c, DI, xs_ref, b_ref, c_ref)
        dst[0, :, off:off + w] = _silu(conv)
        nconv_ref[0, :, cols] = buf_ref[1, :, cols]
        nconv_ref[1, :, cols] = buf_ref[2, :, cols]
        nconv_ref[2, :, cols] = raw


def _k1_ssd_sample(x3, mod3, nw, w, wdt, cw, cb, dtb, buf_t, DI, GN):
    G, R, d = x3.shape
    CD = DI + 2 * GN
    tm = R
    outs = [jax.ShapeDtypeStruct((G, R, DI), F32), jax.ShapeDtypeStruct((G, R, DI), F32),
            jax.ShapeDtypeStruct((G, R, GN), F32), jax.ShapeDtypeStruct((G, R, GN), F32),
            jax.ShapeDtypeStruct((G, R, LANES), F32), jax.ShapeDtypeStruct((SSD_CONV - 1, R, CD), F32)]
    return pl.pallas_call(
        functools.partial(_k1_ssd_sample_body, DI=DI, CD=CD),
        grid=(G, 1),
        in_specs=_k1_specs(x3, mod3, 0, 1, tm) + [_const_spec(w.shape), _const_spec(wdt.shape),
                                                  _const_spec(cw.shape), _const_spec(cb.shape),
                                                  _const_spec(dtb.shape), _const_spec(buf_t.shape)],
        out_specs=[_row_spec(tm, DI), _row_spec(tm, DI), _row_spec(tm, GN), _row_spec(tm, GN),
                   _row_spec(tm, LANES), pl.BlockSpec((SSD_CONV - 1, R, CD), lambda g, t: (0, 0, 0))],
        out_shape=outs,
        compiler_params=_params(("arbitrary", "arbitrary")),
        name="ssd_inproj_step",
    )(x3, mod3, mod3, nw, w, wdt, cw, cb, dtb, buf_t)


def _k2_ssd_body(z_ref, xs_ref, b_ref, c_ref, dt_ref, tri_ref, alog_ref, exp_ref, drep_ref, nw_ref,
                 y_ref, so_ref, s_scr, *, C, NH, P, N, NG, nc):
    ci = pl.program_id(1)

    @pl.when(ci == 0)
    def _():
        s_scr[...] = jnp.zeros_like(s_scr)

    hpg = NH // NG
    gw = hpg * P
    dt = dt_ref[0]
    la = dt * (-jnp.exp(alog_ref[...]))
    cum = _sel_dot(tri_ref[...], _split3(la))
    tot = cum[C - 1:C, :]
    cum_t = cum.T
    dt_t = dt.T
    expand = exp_ref[...]
    w_state = _dot_sel(_split3(jnp.exp(tot - cum) * dt), expand)
    w_in = _dot_sel(_split3(jnp.exp(cum)), expand)
    xs = xs_ref[0]
    xsf = xs.astype(F32)
    xp = (xsf * w_state).astype(BF16)
    row = lax.broadcasted_iota(jnp.int32, (C, C), 0)
    colm = lax.broadcasted_iota(jnp.int32, (C, C), 1)
    causal = row >= colm
    lane = lax.broadcasted_iota(jnp.int32, (1, 2 * P), 1)
    head_keep = ((lane < P).astype(BF16), (lane >= P).astype(BF16))
    ys = []
    for g in range(NG):
        bg = b_ref[0, :, g * N:(g + 1) * N]
        cg = c_ref[0, :, g * N:(g + 1) * N]
        gsl = slice(g * gw, (g + 1) * gw)
        gmat = _dot_nt(cg, bg)
        state = s_scr[g]
        y_inter = _dot(cg, state.astype(BF16)) * w_in[:, gsl]
        s_scr[g] = state * w_in[C - 1:C, gsl] + _dot_tn(bg, xp[:, gsl])
        for hp in range(hpg // 2):
            h0 = g * hpg + 2 * hp
            xpair = xs[:, h0 * P:(h0 + 2) * P]
            acc = y_inter[:, 2 * hp * P:(2 * hp + 2) * P]
            for e in range(2):
                h = h0 + e
                seg = cum[:, h:h + 1] - cum_t[h:h + 1, :]
                dec = jnp.where(causal, jnp.exp(jnp.minimum(seg, 0.0)), 0.0) * dt_t[h:h + 1, :]
                m = (gmat * dec).astype(BF16)
                acc = acc + _dot(m, xpair * head_keep[e])
            ys.append(acc)
    y = jnp.concatenate(ys, axis=-1)
    outs = _ssd_gate(y, xsf, z_ref[0].astype(F32), drep_ref[...], nw_ref[...], NG)
    for g in range(NG):
        y_ref[0, :, g * gw:(g + 1) * gw] = outs[g].astype(y_ref.dtype)

    @pl.when(ci == nc - 1)
    def _():
        for g in range(NG):
            so_ref[g] = s_scr[g].T


def _ssd_expand(NH, P):
    e = np.zeros((LANES, NH * P), np.float32)
    for h in range(NH):
        e[h, h * P:(h + 1) * P] = 1.0
    return jnp.asarray(e, BF16)


def _k2_ssd(z, xs, bm, cm, dt, alog_row, expand, drep, nw, NH, P, N, NG, nl, j, prev):
    G, R, DI = xs.shape
    C = min(C_SSD, R)
    nc = R // C
    i = np.arange(C)
    tri = jnp.asarray(i[:, None] >= i[None, :], BF16)
    gw = (NH // NG) * P
    return _call_into_stack(
        functools.partial(_k2_ssd_body, C=C, NH=NH, P=P, N=N, NG=NG, nc=nc),
        grid=(G, nc),
        inputs=[z, xs, bm, cm, dt, tri, alog_row, expand, drep, nw],
        in_specs=[_row_spec(C, DI), _row_spec(C, DI), _row_spec(C, NG * N), _row_spec(C, NG * N),
                  _row_spec(C, LANES), _const_spec(tri.shape), _const_spec(alog_row.shape),
                  _const_spec(expand.shape), _const_spec(drep.shape), _const_spec(nw.shape)],
        out_specs=[_row_spec(C, DI), _layer_state_spec((NG, gw, N), j)],
        out_shape=[jax.ShapeDtypeStruct((G, R, DI), BF16), jax.ShapeDtypeStruct((nl, G, NG, gw, N), F32)],
        scratch_shapes=[pltpu.VMEM((NG, N, gw), F32)],
        stack_out=1, prev=prev, name="ssd_core")


def _pad_rows(row, keep_first_only):
    x = jnp.broadcast_to(row, (MXU_PAD_ROWS, row.shape[-1]))
    if keep_first_only:
        first = lax.broadcasted_iota(jnp.int32, (MXU_PAD_ROWS, 1), 0) == 0
        x = jnp.where(first, x, 0.0)
    return x.astype(BF16)


def _dec_ssd_body(xs_ref, b_ref, c_ref, dt_ref, alog_ref, exp_ref, s_ref, so_ref, o_ref, *, NH, P, N, NG):
    hpg = NH // NG
    gw = hpg * P
    dtrow = dt_ref[0]
    arow = jnp.exp(dtrow * (-jnp.exp(alog_ref[...])))
    dt_rep = _dot_sel(_split3(jnp.broadcast_to(dtrow, (SUBLANES, LANES))), exp_ref[...])[0:1]
    v_pad = _pad_rows(xs_ref[0] * dt_rep, True)
    k_pad = _pad_rows(b_ref[0], False)
    q_pad = _pad_rows(c_ref[0], False)
    for g in range(NG):
        upd = _dot_tn(v_pad[:, g * gw:(g + 1) * gw], k_pad[:, g * N:(g + 1) * N])
        news = []
        for hl in range(hpg):
            h = g * hpg + hl
            rows = slice(hl * P, (hl + 1) * P)
            new = arow[:, h:h + 1] * s_ref[g, rows, :] + upd[rows, :]
            so_ref[g, rows, :] = new
            news.append(new.astype(BF16))
        o = _dot_nt(q_pad[:, g * N:(g + 1) * N], jnp.concatenate(news, axis=0))
        o_ref[0, :, g * gw:(g + 1) * gw] = o[0:1]


def _dec_ssd(xs, bm, cm, dt, alog_row, expand, states_t, NH, P, N, NG, j, prev):
    B = xs.shape[0]
    tail = states_t.shape[2:]
    return _call_into_stack(
        functools.partial(_dec_ssd_body, NH=NH, P=P, N=N, NG=NG),
        grid=(B,),
        inputs=[_as_steps(xs), _as_steps(bm), _as_steps(cm), _as_steps(dt), alog_row, expand, states_t],
        in_specs=[_step_spec(NH * P), _step_spec(NG * N), _step_spec(NG * N), _step_spec(LANES),
                  _const_spec(alog_row.shape), _const_spec(expand.shape), _layer_state_spec(tail, j)],
        out_specs=[_layer_state_spec(tail, j), _step_spec(NH * P)],
        out_shape=[jax.ShapeDtypeStruct(states_t.shape, F32), jax.ShapeDtypeStruct((B, 1, NH * P), F32)],
        stack_out=0, prev=prev, name="ssd_step")


def _hg_lower_bound(lbl_ref, layer):
    logits = lbl_ref[...]
    e = jnp.exp(logits - jnp.max(logits, axis=0, keepdims=True))
    sm = e / jnp.sum(e, axis=0, keepdims=True)
    lb = jnp.zeros_like(sm[0:1])
    for i in range(1, layer + 1):
        lb = lb + sm[i:i + 1]
    return lb


def _k1_hg_body(x_ref, sh_ref, sc_ref, nw_ref, w_ref, lbl_ref, *out_refs, layer, KD, VD, prompt):
    hb = _norm_mod(x_ref, sh_ref, sc_ref, nw_ref)
    lb = _hg_lower_bound(lbl_ref, layer)
    w = 512
    if prompt:
        q_ref, la_ref, v_ref, g_ref = out_refs
    else:
        q_ref, k_ref, a_ref, v_ref, g_ref = out_refs
    for c in range(KD // w):
        cols = slice(c * w, (c + 1) * w)
        q = _silu(_dot(hb, w_ref[:, c * w:(c + 1) * w])) * (HG_KEY_DIM ** -0.5)
        q_ref[0, :, cols] = q.astype(q_ref.dtype)
        lbc = lb[:, cols]
        f = lbc + (1.0 - lbc) * jax.nn.sigmoid(_dot(hb, w_ref[:, KD + c * w:KD + (c + 1) * w]))
        if prompt:
            la_ref[0, :, cols] = jnp.log(f)
        else:
            k_ref[0, :, cols] = 1.0 - f
            a_ref[0, :, cols] = f
    for c in range(VD // w):
        cols = slice(c * w, (c + 1) * w)
        v_ref[0, :, cols] = _dot(hb, w_ref[:, 2 * KD + c * w:2 * KD + (c + 1) * w]).astype(v_ref.dtype)
        g_ref[0, :, cols] = _dot(hb, w_ref[:, 2 * KD + VD + c * w:2 * KD + VD + (c + 1) * w]).astype(g_ref.dtype)


def _k1_hg(x3, mod3, nw, w, lbl, tm, layer, KD, VD, prompt):
    G, R, d = x3.shape
    if prompt:
        dts = [(KD, BF16), (KD, F32), (VD, BF16), (VD, BF16)]
    else:
        dts = [(KD, F32), (KD, F32), (KD, F32), (VD, F32), (VD, F32)]
    return pl.pallas_call(
        functools.partial(_k1_hg_body, layer=layer, KD=KD, VD=VD, prompt=prompt),
        grid=(G, R // tm),
        in_specs=_k1_specs(x3, mod3, 0, 1, tm) + [_const_spec(w.shape), _const_spec(lbl.shape)],
        out_specs=[_row_spec(tm, n) for n, _ in dts],
        out_shape=[jax.ShapeDtypeStruct((G, R, n), dt) for n, dt in dts],
        compiler_params=_params(("arbitrary", "arbitrary")),
        name="hg_inproj" if prompt else "hg_inproj_step",
    )(x3, mod3, mod3, nw, w, lbl)


def _hg_masks(C):
    i = np.arange(C)
    r, t = i[:, None], i[None, :]
    sels = [t <= r, (t <= r) & (t >= HG_SUB * (r // HG_SUB))]
    pairs = []
    s = C // 2
    while s >= HG_SUB:
        mid = 2 * s * (r // (2 * s)) + s - 1
        sels.append(np.where(r > mid, (t > mid) & (t <= r), (t > r) & (t <= mid)))
        pairs.append((r // (2 * s) == t // (2 * s)) & (r % (2 * s) >= s) & (t % (2 * s) < s))
        s //= 2
    pairs.append((r // HG_SUB == t // HG_SUB) & (t <= r))
    return (jnp.asarray(np.stack(sels), BF16), jnp.asarray(np.stack(pairs), F32))


def _hg_diag_collect(C, K):
    e = np.zeros((HG_SUB, K, C), np.float32)
    for j in range(HG_SUB):
        e[j, :, j::HG_SUB] = 1.0
    return jnp.asarray(e.reshape(HG_SUB * K, C), BF16)


def _k2_hg_body(q_ref, la_ref, v_ref, g_ref, sel_ref, pair_ref, coll_ref, nw_ref, y_ref, so_ref,
                st_scr, k_scr, c_scr, p_scr, *, C, H, K, V, nlev, nc):
    ci = pl.program_id(1)

    @pl.when(ci == 0)
    def _():
        st_scr[...] = jnp.zeros_like(st_scr)

    D = H * K
    la = la_ref[0]
    parts = _split3(la)
    cum = _sel_dot(sel_ref[0], parts)
    cw = _sel_dot(sel_ref[1], parts)
    kf = 1.0 - jnp.exp(la)
    qf = q_ref[0].astype(F32)
    vb = v_ref[0]
    tot = cum[C - 1:C, :]
    q_in = (qf * jnp.exp(cum)).astype(BF16)
    k_out = (kf * jnp.exp(tot - cum)).astype(BF16)
    e_tot = jnp.exp(tot)

    scores = [jnp.zeros((C, C), F32) for _ in range(H)]
    for lev in range(nlev):
        a = jnp.exp(_sel_dot(sel_ref[2 + lev], parts))
        ql = (qf * a).astype(BF16)
        kl = (kf * a).astype(BF16)
        pm = pair_ref[lev] > 0.5
        for h in range(H):
            sl = slice(h * K, (h + 1) * K)
            scores[h] = scores[h] + jnp.where(pm, _dot_nt(ql[:, sl], kl[:, sl]), 0.0)

    log2e = 1.4426950408889634
    cw2 = cw * log2e
    k_scr[...] = kf
    c_scr[...] = cw2
    rowmod = lax.broadcasted_iota(jnp.int32, (C, D), 0) & (HG_SUB - 1)
    nblk = C // HG_SUB
    for j in range(HG_SUB):
        kb = jnp.concatenate([jnp.broadcast_to(k_scr[b * HG_SUB + j:b * HG_SUB + j + 1, :], (HG_SUB, D))
                              for b in range(nblk)], axis=0)
        cb = jnp.concatenate([jnp.broadcast_to(c_scr[b * HG_SUB + j:b * HG_SUB + j + 1, :], (HG_SUB, D))
                              for b in range(nblk)], axis=0)
        dec = jnp.exp2(jnp.where(rowmod >= j, cw2 - cb, -1e30))
        p = (qf * kb * dec).astype(BF16)
        for h in range(H):
            p_scr[h * C:(h + 1) * C, j * K:(j + 1) * K] = p[:, h * K:(h + 1) * K]
    diag = _dot(p_scr[...], coll_ref[...])
    dm = pair_ref[nlev] > 0.5

    nw = nw_ref[...]
    for h in range(H):
        sl = slice(h * K, (h + 1) * K)
        vsl = slice(h * V, (h + 1) * V)
        st = st_scr[h]
        sc = scores[h] + jnp.where(dm, diag[h * C:(h + 1) * C, :], 0.0)
        o = _dot(sc.astype(BF16), vb[:, vsl]) + _dot_nt(q_in[:, sl], st.astype(BF16))
        st_scr[h] = st * e_tot[:, sl] + _dot_tn(vb[:, vsl], k_out[:, sl])
        gh = g_ref[0, :, vsl].astype(F32)
        y_ref[0, :, vsl] = (_rms(o) * nw[:, vsl] * _silu(gh)).astype(y_ref.dtype)

    @pl.when(ci == nc - 1)
    def _():
        for h in range(H):
            so_ref[h] = st_scr[h].T


def _k2_hg(q, la, v, g, nw_rep, H, K, V, nl, j, prev):
    G, R, D = q.shape
    C = min(C_HG, R)
    nc = R // C
    sels, pairs = _hg_masks(C)
    nlev = pairs.shape[0] - 1
    coll = _hg_diag_collect(C, K)
    return _call_into_stack(
        functools.partial(_k2_hg_body, C=C, H=H, K=K, V=V, nlev=nlev, nc=nc),
        grid=(G, nc),
        inputs=[q, la, v, g, sels, pairs, coll, nw_rep],
        in_specs=[_row_spec(C, D), _row_spec(C, D), _row_spec(C, H * V), _row_spec(C, H * V),
                  _const_spec(sels.shape), _const_spec(pairs.shape), _const_spec(coll.shape),
                  _const_spec(nw_rep.shape)],
        out_specs=[_row_spec(C, H * V), _layer_state_spec((H, K, V), j)],
        out_shape=[jax.ShapeDtypeStruct((G, R, H * V), BF16), jax.ShapeDtypeStruct((nl, G, H, K, V), F32)],
        scratch_shapes=[pltpu.VMEM((H, V, K), F32), pltpu.VMEM((C, D), F32), pltpu.VMEM((C, D), F32),
                        pltpu.VMEM((H * C, HG_SUB * K), BF16)],
        stack_out=1, prev=prev, name="hg_core")


def _dec_hg_body(q_ref, k_ref, a_ref, v_ref, s_ref, so_ref, o_ref, *, H, K, V):
    eye = lax.broadcasted_iota(jnp.int32, (K, K), 0) == lax.broadcasted_iota(jnp.int32, (K, K), 1)
    arow = a_ref[0]
    q_pad = _pad_rows(q_ref[0], False)
    k_pad = _pad_rows(k_ref[0], True)
    v_pad = _pad_rows(v_ref[0], False)
    for h in range(H):
        sl = slice(h * K, (h + 1) * K)
        vsl = slice(h * V, (h + 1) * V)
        ac = _col(arow[:, sl], eye)
        new = ac * s_ref[h] + _dot_tn(k_pad[:, sl], v_pad[:, vsl])
        so_ref[h] = new
        o_ref[0, :, vsl] = _dot(q_pad[:, sl], new.astype(BF16))[0:1]


def _dec_hg(q, k, a, v, states, H, K, V, j, prev):
    B = q.shape[0]
    return _call_into_stack(
        functools.partial(_dec_hg_body, H=H, K=K, V=V),
        grid=(B,),
        inputs=[_as_steps(q), _as_steps(k), _as_steps(a), _as_steps(v), states],
        in_specs=[_step_spec(H * K), _step_spec(H * K), _step_spec(H * K), _step_spec(H * V),
                  _layer_state_spec((H, K, V), j)],
        out_specs=[_layer_state_spec((H, K, V), j), _step_spec(H * V)],
        out_shape=[jax.ShapeDtypeStruct(states.shape, F32), jax.ShapeDtypeStruct((B, 1, H * V), F32)],
        stack_out=0, prev=prev, name="hg_step")


def _pro_identity(y_ref):
    return y_ref[0]


def _pro_ret(o_ref, g_ref, *, H, V):
    return jnp.concatenate(_ret_gate(o_ref[0], g_ref[0], H, V), axis=-1).astype(BF16)


def _pro_ssd(o_ref, xs_ref, z_ref, drep_ref, nw_ref, *, NG):
    return jnp.concatenate(_ssd_gate(o_ref[0], xs_ref[0], z_ref[0], drep_ref[...], nw_ref[...], NG),
                           axis=-1).astype(BF16)


def _pro_hg(o_ref, g_ref, nw_ref, *, H, V):
    return jnp.concatenate(_hg_gate(o_ref[0], g_ref[0], nw_ref[...], H, V), axis=-1).astype(BF16)


def _k3_body(*refs, prologue, n_pro):
    w_ref, x_ref, gate_ref, nw_ref, o_ref = refs[n_pro:]
    y = prologue(*refs[:n_pro])
    o = _dot(y, w_ref[...])
    o_ref[0] = x_ref[0] + gate_ref[0] * (_rms(o) * nw_ref[...])


def _k3(prologue, pro_args, pro_specs, w, x3, mod3, gate_idx, nw, tm):
    G, R, d = x3.shape
    rm = mod3.shape[1]
    n_pro = len(pro_args)
    return pl.pallas_call(
        functools.partial(_k3_body, prologue=prologue, n_pro=n_pro),
        grid=(G, R // tm),
        in_specs=list(pro_specs) + [
            _const_spec(w.shape),
            pl.BlockSpec((1, tm, d), lambda g, t: (g, t, 0)),
            pl.BlockSpec((1, rm, d), lambda g, t: (g, 0, gate_idx)),
            _const_spec((1, d)),
        ],
        out_specs=pl.BlockSpec((1, tm, d), lambda g, t: (g, t, 0)),
        out_shape=jax.ShapeDtypeStruct((G, R, d), F32),
        compiler_params=_params(("arbitrary", "arbitrary")),
        name="outproj",
    )(*pro_args, w, x3, mod3, nw)


def _swiglu(hb, win_ref, wout_ref, F, bounds):
    acc = None
    for lo, hi in bounds:
        gt = _dot(hb, win_ref[:, lo:hi])
        up = _dot(hb, win_ref[:, F + lo:F + hi])
        a = (_silu(gt) * up).astype(BF16)
        part = _dot(a, wout_ref[lo:hi, :])
        acc = part if acc is None else acc + part
    return acc


def _k4_body(x_ref, sh_ref, sc_ref, nw_ref, gate_ref, nwp_ref, win_ref, wout_ref, o_ref, *, F, bounds):
    x = x_ref[0]
    hb = _norm_mod(x_ref, sh_ref, sc_ref, nw_ref)
    acc = _swiglu(hb, win_ref, wout_ref, F, bounds)
    o_ref[0] = x + gate_ref[0] * (_rms(acc) * nwp_ref[...])


def _k34_body(y_ref, wo_ref, x_ref, gm_ref, nwm_ref, sh_ref, sc_ref, nwf_ref, gf_ref, nwp_ref,
              win_ref, wout_ref, o_ref, *, F, bounds):
    x = x_ref[0] + gm_ref[0] * (_rms(_dot(y_ref[0], wo_ref[...])) * nwm_ref[...])
    hb = (_rms(x) * nwf_ref[...] * (1.0 + sc_ref[0]) + sh_ref[0]).astype(BF16)
    acc = _swiglu(hb, win_ref, wout_ref, F, bounds)
    o_ref[0] = x + gf_ref[0] * (_rms(acc) * nwp_ref[...])


def _k34(y, w_out, x3, mod3, nw_mix_post, nw_pre, nw_post, win, wout, tm):
    G, R, d = x3.shape
    rm = mod3.shape[1]
    F = wout.shape[0]
    mod_spec = lambda idx: pl.BlockSpec((1, rm, d), lambda g, t: (g, 0, idx))
    row = _const_spec((1, d))
    return pl.pallas_call(
        functools.partial(_k34_body, F=F, bounds=_ffn_bounds(F)),
        grid=(G, R // tm),
        in_specs=[_row_spec(tm, y.shape[-1]), _const_spec(w_out.shape), _row_spec(tm, d), mod_spec(2), row,
                  mod_spec(3), mod_spec(4), row, mod_spec(5), row, _const_spec(win.shape),
                  _const_spec(wout.shape)],
        out_specs=_row_spec(tm, d),
        out_shape=jax.ShapeDtypeStruct((G, R, d), F32),
        compiler_params=_params(("arbitrary", "arbitrary")),
        name="outproj_ffn",
    )(y, w_out, x3, mod3, nw_mix_post, mod3, mod3, nw_pre, mod3, nw_post, win, wout)


def _ffn_bounds(F):
    tiles, rem = divmod(F, MXU_TILE)
    assert rem == 0, F
    n = -(-tiles // FFN_MAX_CHUNK_TILES)
    sizes = [tiles // n + (1 if i < tiles % n else 0) for i in range(n)]
    edges = np.cumsum([0] + sizes) * MXU_TILE
    return tuple((int(a), int(b)) for a, b in zip(edges[:-1], edges[1:]))


def _k4(x3, mod3, nw_pre, nw_post, win, wout, tm):
    G, R, d = x3.shape
    rm = mod3.shape[1]
    F = wout.shape[0]
    return pl.pallas_call(
        functools.partial(_k4_body, F=F, bounds=_ffn_bounds(F)),
        grid=(G, R // tm),
        in_specs=_k1_specs(x3, mod3, 3, 4, tm) + [
            pl.BlockSpec((1, rm, d), lambda g, t: (g, 0, 5)),
            _const_spec((1, d)), _const_spec(win.shape), _const_spec(wout.shape)],
        out_specs=pl.BlockSpec((1, tm, d), lambda g, t: (g, t, 0)),
        out_shape=jax.ShapeDtypeStruct((G, R, d), F32),
        compiler_params=_params(("arbitrary", "arbitrary")),
        name="ffn",
    )(x3, mod3, mod3, nw_pre, mod3, nw_post, win, wout)


def _rope_table(pos, half):
    inv = 1.0 / (ROPE_BASE ** (jnp.arange(half, dtype=F32) / half))
    ang = pos.astype(F32)[:, None] * inv[None, :]
    return jnp.cos(ang), jnp.sin(ang)


def kernel(x_prompt, x_sample, c_prompt, c_sample, state_ret, state_ssd, state_conv, state_hgrn, w_ada, b_ada, norm_mix_pre, norm_mix_post, norm_ffn_pre, norm_ffn_post, ret_w_in, ret_w_out, ssd_w_in, ssd_conv_w, ssd_conv_b, ssd_dt_bias, ssd_a_log, ssd_d, ssd_norm, ssd_w_out, hg_w_in, hg_lb_logits, hg_norm, hg_w_out, ffn_w_in, ffn_w_out):
    bp, seq, d = x_prompt.shape
    bs = x_sample.shape[0]
    depth = w_ada.shape[0]
    ret_h = d // 256
    ret_k = d // ret_h
    ret_v = 2 * ret_k
    di = 2 * d
    ssd_nh = di // SSD_HEAD_DIM
    gn = SSD_GROUPS * SSD_STATE
    hg_h = d // HG_KEY_DIM
    hg_v = d // hg_h
    kd = hg_h * HG_KEY_DIM
    vd = hg_h * hg_v
    tm = min(TM_PROMPT, seq)

    mod = _adaln(jnp.concatenate([c_prompt, c_sample], axis=0), w_ada, b_ada)
    xp = x_prompt.astype(F32)
    xs = x_sample.astype(F32).reshape(1, bs, d)
    cos_p, sin_p = _rope_table(jnp.arange(seq, dtype=jnp.int32), ret_k // 2)
    cos_s, sin_s = _rope_table(jnp.full((1,), PAST_LEN, jnp.int32), ret_k // 2)

    hpg = ssd_nh // SSD_GROUPS
    ssd_states_t = jnp.swapaxes(state_ssd.astype(F32), 3, 4).reshape(
        state_ssd.shape[0], bs, SSD_GROUPS, hpg * SSD_HEAD_DIM, SSD_STATE)

    def ssd_untranspose(s):
        s = s.reshape(s.shape[0], s.shape[1], ssd_nh, SSD_HEAD_DIM, SSD_STATE)
        return jnp.swapaxes(s, 3, 4).astype(state_ssd.dtype)

    new = {k: [] for k in ("conv_p", "conv_s")}
    stk = {k: None for k in ("ret_p", "ret_s", "ssd_p", "ssd_s", "hg_p", "hg_s")}
    n_kind = [sum(1 for l in range(depth) if l % 3 == kind) for kind in range(3)]
    counts = [0, 0, 0]
    for layer in range(depth):
        mod_p = mod[layer, :bp].reshape(bp, 1, 6 * d)
        mod_s = mod[layer, bp:].reshape(1, bs, 6 * d)
        nw_pre = norm_mix_pre[layer].reshape(1, d)
        nw_post = norm_mix_post[layer].reshape(1, d)
        kind = layer % 3
        j = counts[kind]
        counts[kind] += 1
        s_spec = lambda n: _row_spec(bs, n)
        if kind == 0:
            w_in = ret_w_in[j].astype(BF16)
            w_out = ret_w_out[j].astype(BF16)
            q, k, v, g = _k1_ret(xp, mod_p, nw_pre, w_in, cos_p, sin_p, tm, BF16, ret_h, ret_k, ret_v)
            y, stk["ret_p"] = _k2_ret(q, k, v, g, ret_h, ret_k, ret_v, n_kind[0], j, stk["ret_p"])
            q, k, v, g = _k1_ret(xs, mod_s, nw_pre, w_in, cos_s, sin_s, bs, F32, ret_h, ret_k, ret_v)
            stk["ret_s"], o = _dec_ret(q[0], k[0], v[0], state_ret, ret_h, ret_k, ret_v, j, stk["ret_s"])
            xs = _k3(functools.partial(_pro_ret, H=ret_h, V=ret_v), [o.reshape(1, bs, -1), g],
                     [s_spec(ret_h * ret_v)] * 2, w_out, xs, mod_s, 2, nw_post, bs)
        elif kind == 1:
            w_main = ssd_w_in[j][:, :di + di + 2 * gn].astype(BF16)
            w_dt = jnp.pad(ssd_w_in[j][:, di + di + 2 * gn:], ((0, 0), (0, LANES - ssd_nh))).astype(BF16)
            w_out = ssd_w_out[j].astype(BF16)
            cw = ssd_conv_w[j]
            cb = ssd_conv_b[j].reshape(1, -1)
            dtb = jnp.pad(ssd_dt_bias[j], (0, LANES - ssd_nh)).reshape(1, LANES)
            alog = jnp.pad(ssd_a_log[j].astype(F32), (0, LANES - ssd_nh)).reshape(1, LANES)
            drep = jnp.repeat(ssd_d[j], SSD_HEAD_DIM).reshape(1, di)
            nw_ssd = ssd_norm[j].reshape(1, di)
            z, xc, bm, cm, dt, nconv = _k1_ssd_prompt(xp, mod_p, nw_pre, w_main, w_dt, cw, cb, dtb, tm, di, gn)
            new["conv_p"].append(nconv)
            expand = _ssd_expand(ssd_nh, SSD_HEAD_DIM)
            y, stk["ssd_p"] = _k2_ssd(z, xc, bm, cm, dt, alog, expand, drep, nw_ssd, ssd_nh, SSD_HEAD_DIM,
                                      SSD_STATE, SSD_GROUPS, n_kind[1], j, stk["ssd_p"])
            buf_t = jnp.transpose(state_conv[j].astype(F32), (1, 0, 2))
            z, xc, bm, cm, dt, nconv = _k1_ssd_sample(xs, mod_s, nw_pre, w_main, w_dt, cw, cb, dtb, buf_t, di, gn)
            new["conv_s"].append(jnp.transpose(nconv, (1, 0, 2)))
            stk["ssd_s"], o = _dec_ssd(xc[0], bm[0], cm[0], dt[0], alog, expand, ssd_states_t,
                                       ssd_nh, SSD_HEAD_DIM, SSD_STATE, SSD_GROUPS, j, stk["ssd_s"])
            xs = _k3(functools.partial(_pro_ssd, NG=SSD_GROUPS), [o.reshape(1, bs, -1), xc, z, drep, nw_ssd],
                     [s_spec(di)] * 3 + [_const_spec((1, di))] * 2, w_out, xs, mod_s, 2, nw_post, bs)
        else:
            w_in = hg_w_in[j].astype(BF16)
            w_out = hg_w_out[j].astype(BF16)
            nw_hg = jnp.tile(hg_norm[j], hg_h).reshape(1, vd)
            lbl = hg_lb_logits.astype(F32)
            q, la, v, g = _k1_hg(xp, mod_p, nw_pre, w_in, lbl, tm, layer, kd, vd, True)
            y, stk["hg_p"] = _k2_hg(q, la, v, g, nw_hg, hg_h, HG_KEY_DIM, hg_v, n_kind[2], j, stk["hg_p"])
            q, k, a, v, g = _k1_hg(xs, mod_s, nw_pre, w_in, lbl, bs, layer, kd, vd, False)
            stk["hg_s"], o = _dec_hg(q[0], k[0], a[0], v[0], state_hgrn, hg_h, HG_KEY_DIM, hg_v, j, stk["hg_s"])
            xs = _k3(functools.partial(_pro_hg, H=hg_h, V=hg_v), [o.reshape(1, bs, -1), g, nw_hg],
                     [s_spec(vd)] * 2 + [_const_spec((1, vd))], w_out, xs, mod_s, 2, nw_post, bs)
        nf_pre = norm_ffn_pre[layer].reshape(1, d)
        nf_post = norm_ffn_post[layer].reshape(1, d)
        f_in = ffn_w_in[layer].astype(BF16)
        f_out = ffn_w_out[layer].astype(BF16)
        xp = _k34(y, w_out, xp, mod_p, nw_post, nf_pre, nf_post, f_in, f_out, tm)
        xs = _k4(xs, mod_s, nf_pre, nf_post, f_in, f_out, bs)

    stack = lambda name, like: jnp.stack(new[name]).astype(like.dtype)
    return (xp.astype(x_prompt.dtype), xs.reshape(bs, 1, d).astype(x_sample.dtype),
            stk["ret_p"].astype(state_ret.dtype), stk["ret_s"].astype(state_ret.dtype),
            ssd_untranspose(stk["ssd_p"]), ssd_untranspose(stk["ssd_s"]),
            stack("conv_p", state_conv), stack("conv_s", state_conv),
            stk["hg_p"].astype(state_hgrn.dtype), stk["hg_s"].astype(state_hgrn.dtype))
```

```python
import functools

import numpy as np
import jax
import jax.numpy as jnp
from jax import lax
from jax.experimental import pallas as pl
from jax.experimental.pallas import tpu as pltpu

F32 = jnp.float32
BF16 = jnp.bfloat16
EPS = 1e-6
PAST_LEN = 16384
ROPE_BASE = 10000.0
SSD_HEAD_DIM = 64
SSD_GROUPS = 4
SSD_STATE = 128
SSD_CONV = 4
HG_KEY_DIM = 128
LANES = 128
SUBLANES = 8
VMEM_LIMIT = 56 * 1024 * 1024
TM_PROMPT = 512
MXU_TILE = 256
MXU_PAD_ROWS = 16
C_RET = 256
C_SSD = 128
C_HG = 128
HG_SUB = 8
FFN_MAX_CHUNK_TILES = 6
LOG2E = 1.4426950408889634
MASKED_EXPONENT = -1e30
BT_RET = 2
BT_SSD = 2
BT_HG = 4


def _rms(x):
    return x * lax.rsqrt(jnp.mean(x * x, axis=-1, keepdims=True) + EPS)


def _silu(x):
    h = 0.5 * x
    return h + h * jnp.tanh(h)


def _softplus(x):
    return jnp.maximum(x, 0.0) + jnp.log1p(jnp.exp(-jnp.abs(x)))


def _dot(a, b):
    return jnp.dot(a, b, preferred_element_type=F32)


def _dot_nt(a, b):
    return lax.dot_general(a, b, (((1,), (1,)), ((), ())), preferred_element_type=F32)


def _dot_tn(a, b):
    return lax.dot_general(a, b, (((0,), (0,)), ((), ())), preferred_element_type=F32)


def _split3(x):
    hi = x.astype(BF16)
    r = x - hi.astype(F32)
    mid = r.astype(BF16)
    lo = (r - mid.astype(F32)).astype(BF16)
    return hi, mid, lo


def _sel_dot(sel, parts):
    acc = _dot(sel, parts[0])
    for p in parts[1:]:
        acc = acc + _dot(sel, p)
    return acc


def _dot_sel(parts, sel):
    acc = _dot(parts[0], sel)
    for p in parts[1:]:
        acc = acc + _dot(p, sel)
    return acc


def _col(row, eye):
    return jnp.sum(jnp.where(eye, row, 0.0), axis=1, keepdims=True)


def _norm_mod(x_ref, sh_ref, sc_ref, nw_ref):
    x = x_ref[0]
    h = _rms(x) * nw_ref[...] * (1.0 + sc_ref[0]) + sh_ref[0]
    return h.astype(BF16)


def _params(sem):
    return pltpu.CompilerParams(dimension_semantics=sem, vmem_limit_bytes=VMEM_LIMIT)


def _const_spec(shape):
    nd = len(shape)
    return pl.BlockSpec(shape, lambda *_: (0,) * nd, pipeline_mode=pl.Buffered(1))


def _ret_gate(o, g, H, V):
    outs = []
    for h in range(H):
        outs.append(_rms(o[:, h * V:(h + 1) * V]) * _silu(g[:, h * V:(h + 1) * V]))
    return outs


def _ssd_gate(y, xs, z, d_rep, nw, groups):
    y = (y + d_rep * xs) * _silu(z)
    w = y.shape[-1] // groups
    outs = []
    for g in range(groups):
        outs.append(_rms(y[:, g * w:(g + 1) * w]) * nw[:, g * w:(g + 1) * w])
    return outs


def _hg_gate(o, g, nw, H, V):
    outs = []
    for h in range(H):
        sl = slice(h * V, (h + 1) * V)
        outs.append(_rms(o[:, sl]) * nw[:, sl] * _silu(g[:, sl]))
    return outs


def _adaln_body(c_ref, w_ref, b_ref, o_ref):
    a = _silu(c_ref[...]).astype(BF16)
    o_ref[0] = _dot(a, w_ref[0].astype(BF16)) + b_ref[0]


def _adaln(c_all, w_ada, b_ada):
    depth, d, n = w_ada.shape
    bc = c_all.shape[0]
    tn = 1024
    return pl.pallas_call(
        _adaln_body,
        grid=(depth, n // tn),
        in_specs=[
            pl.BlockSpec((bc, d), lambda l, j: (0, 0)),
            pl.BlockSpec((1, d, tn), lambda l, j: (l, 0, j)),
            pl.BlockSpec((1, 1, tn), lambda l, j: (l, 0, j)),
        ],
        out_specs=pl.BlockSpec((1, bc, tn), lambda l, j: (l, 0, j)),
        out_shape=jax.ShapeDtypeStruct((depth, bc, n), F32),
        compiler_params=_params(("arbitrary", "arbitrary")),
        name="adaln",
    )(c_all, w_ada, b_ada.reshape(depth, 1, n))


def _k1_specs(x3, mod3, sh_idx, sc_idx, tm):
    _, _, d = x3.shape
    rm = mod3.shape[1]
    return [
        pl.BlockSpec((1, tm, d), lambda g, t: (g, t, 0)),
        pl.BlockSpec((1, rm, d), lambda g, t: (g, 0, sh_idx)),
        pl.BlockSpec((1, rm, d), lambda g, t: (g, 0, sc_idx)),
        _const_spec((1, d)),
    ]


def _row_spec(tm, n):
    return pl.BlockSpec((1, tm, n), lambda g, t: (g, t, 0))


def _layer_state_spec(tail, j, bt=None):
    nd = len(tail)
    return pl.BlockSpec((None, bt) + tuple(tail), lambda b, *_: (j, b) + (0,) * nd)


def _call_into_stack(body, *, grid, inputs, in_specs, out_specs, out_shape, stack_out, prev, name,
                     scratch_shapes=()):
    inputs, in_specs = list(inputs), list(in_specs)
    n_in = len(inputs)
    aliases = {}
    kernel_fn = body
    if prev is not None:
        inputs.append(prev)
        in_specs.append(pl.BlockSpec(memory_space=pl.ANY))
        aliases = {n_in: stack_out}

        def kernel_fn(*refs):
            body(*refs[:n_in], *refs[n_in + 1:])

    return pl.pallas_call(
        kernel_fn, grid=grid, in_specs=in_specs, out_specs=out_specs, out_shape=out_shape,
        scratch_shapes=scratch_shapes, input_output_aliases=aliases,
        compiler_params=_params(("arbitrary",) * len(grid)), name=name)(*inputs)


def _k1_ret_body(x_ref, sh_ref, sc_ref, nw_ref, w_ref, cos_ref, sin_ref,
                 q_ref, k_ref, v_ref, g_ref, *, H, K, V):
    hb = _norm_mod(x_ref, sh_ref, sc_ref, nw_ref)
    cos = cos_ref[...]
    sin = sin_ref[...]
    half = K // 2
    for h in range(H):
        for dst, base, scale in ((q_ref, 0, None), (k_ref, H * K, K ** -0.5)):
            a = _dot(hb, w_ref[:, base + h * K:base + (h + 1) * K])
            x1 = a[:, :half]
            x2 = a[:, half:]
            r1 = x1 * cos - x2 * sin
            r2 = x2 * cos + x1 * sin
            if scale is not None:
                r1 = r1 * scale
                r2 = r2 * scale
            dst[0, :, h * K:h * K + half] = r1.astype(dst.dtype)
            dst[0, :, h * K + half:(h + 1) * K] = r2.astype(dst.dtype)
    vb = 2 * H * K
    for c in range(H):
        v_ref[0, :, c * V:(c + 1) * V] = _dot(hb, w_ref[:, vb + c * V:vb + (c + 1) * V]).astype(v_ref.dtype)
        g_ref[0, :, c * V:(c + 1) * V] = _dot(
            hb, w_ref[:, vb + H * V + c * V:vb + H * V + (c + 1) * V]).astype(g_ref.dtype)


def _k1_ret(x3, mod3, nw, w, cos, sin, tm, out_dtype, H, K, V):
    G, R, d = x3.shape
    rc = cos.shape[0]
    cs_spec = (pl.BlockSpec((tm, K // 2), lambda g, t: (t, 0)) if rc == R
               else _const_spec((1, K // 2)))
    outs = [jax.ShapeDtypeStruct((G, R, H * K), out_dtype), jax.ShapeDtypeStruct((G, R, H * K), out_dtype),
            jax.ShapeDtypeStruct((G, R, H * V), out_dtype), jax.ShapeDtypeStruct((G, R, H * V), out_dtype)]
    return pl.pallas_call(
        functools.partial(_k1_ret_body, H=H, K=K, V=V),
        grid=(G, R // tm),
        in_specs=_k1_specs(x3, mod3, 0, 1, tm) + [_const_spec(w.shape), cs_spec, cs_spec],
        out_specs=[_row_spec(tm, H * K), _row_spec(tm, H * K), _row_spec(tm, H * V), _row_spec(tm, H * V)],
        out_shape=outs,
        compiler_params=_params(("arbitrary", "arbitrary")),
        name="ret_inproj",
    )(x3, mod3, mod3, nw, w, cos, sin)


def _k2_ret_body(q_ref, k_ref, v_ref, g_ref, dmat_ref, qdec_ref, kdec_ref, y_ref, s_ref, *, H, K, V, sdec):
    @pl.when(pl.program_id(1) == 0)
    def _():
        s_ref[...] = jnp.zeros_like(s_ref)

    for h in range(H):
        qh = q_ref[0, :, h * K:(h + 1) * K]
        kh = k_ref[0, :, h * K:(h + 1) * K]
        vh = v_ref[0, :, h * V:(h + 1) * V]
        state = s_ref[h]
        scores = _dot_nt(qh, kh) * dmat_ref[h]
        o = _dot(scores.astype(BF16), vh) + _dot(qh, state.astype(BF16)) * qdec_ref[h]
        k_out = (kh.astype(F32) * kdec_ref[h]).astype(BF16)
        s_ref[h] = sdec[h] * state + _dot_tn(k_out, vh)
        gh = g_ref[0, :, h * V:(h + 1) * V].astype(F32)
        y_ref[0, :, h * V:(h + 1) * V] = (_rms(o) * _silu(gh)).astype(y_ref.dtype)


def _ret_tables(H, K, V, C):
    lg = np.log1p(-np.exp2(-5.0 - np.arange(H, dtype=np.float64)))
    i = np.arange(C, dtype=np.float64)
    diff = i[:, None] - i[None, :]
    dmat = np.where(diff >= 0, np.exp(lg[:, None, None] * np.maximum(diff, 0.0)), 0.0)
    qdec = np.broadcast_to(np.exp(lg[:, None, None] * (i[None, :, None] + 1.0)), (H, C, V))
    kdec = np.broadcast_to(np.exp(lg[:, None, None] * (C - 1.0 - i[None, :, None])), (H, C, K))
    sdec = tuple(float(np.exp(l * C)) for l in lg)
    return (jnp.asarray(dmat, F32), jnp.asarray(qdec, F32), jnp.asarray(kdec, F32), sdec)


def _k2_ret(q, k, v, g, H, K, V, nl, j, prev):
    G, R, _ = q.shape
    C = min(C_RET, R)
    dmat, qdec, kdec, sdec = _ret_tables(H, K, V, C)
    return _call_into_stack(
        functools.partial(_k2_ret_body, H=H, K=K, V=V, sdec=sdec),
        grid=(G, R // C),
        inputs=[q, k, v, g, dmat, qdec, kdec],
        in_specs=[_row_spec(C, H * K), _row_spec(C, H * K), _row_spec(C, H * V), _row_spec(C, H * V),
                  _const_spec(dmat.shape), _const_spec(qdec.shape), _const_spec(kdec.shape)],
        out_specs=[_row_spec(C, H * V), _layer_state_spec((H, K, V), j)],
        out_shape=[jax.ShapeDtypeStruct((G, R, H * V), BF16), jax.ShapeDtypeStruct((nl, G, H, K, V), F32)],
        stack_out=1, prev=prev, name="ret_core")


def _dec_ret_body(q_ref, k_ref, v_ref, s_ref, so_ref, o_ref, *, H, K, V, gammas, bt):
    eye = lax.broadcasted_iota(jnp.int32, (K, K), 0) == lax.broadcasted_iota(jnp.int32, (K, K), 1)
    for i in range(bt):
        qrow = q_ref[i]
        krow = k_ref[i]
        vrow = v_ref[i]
        for h in range(H):
            qc = _col(qrow[:, h * K:(h + 1) * K], eye)
            kc = _col(krow[:, h * K:(h + 1) * K], eye)
            new = gammas[h] * s_ref[i, h] + kc * vrow[:, h * V:(h + 1) * V]
            so_ref[i, h] = new
            o_ref[i, :, h * V:(h + 1) * V] = jnp.sum(qc * new, axis=0, keepdims=True)


def _step_spec(n, bt=1):
    return pl.BlockSpec((bt, 1, n), lambda b: (b, 0, 0))


def _as_steps(x):
    return x.reshape(x.shape[0], 1, x.shape[1])


def _step_batch(B, want):
    return want if B % want == 0 else 1


def _dec_ret(q, k, v, states, H, K, V, j, prev):
    B = q.shape[0]
    gammas = tuple(float(1.0 - 2.0 ** (-5 - h)) for h in range(H))
    bt = _step_batch(B, BT_RET)
    return _call_into_stack(
        functools.partial(_dec_ret_body, H=H, K=K, V=V, gammas=gammas, bt=bt),
        grid=(B // bt,),
        inputs=[_as_steps(q), _as_steps(k), _as_steps(v), states],
        in_specs=[_step_spec(H * K, bt), _step_spec(H * K, bt), _step_spec(H * V, bt),
                  _layer_state_spec((H, K, V), j, bt)],
        out_specs=[_layer_state_spec((H, K, V), j, bt), _step_spec(H * V, bt)],
        out_shape=[jax.ShapeDtypeStruct(states.shape, F32), jax.ShapeDtypeStruct((B, 1, H * V), F32)],
        stack_out=0, prev=prev, name="ret_step")


def _ssd_route(c, DI, xs_ref, b_ref, c_ref):
    w = 512
    if c * w < DI:
        return xs_ref, c * w
    if c * w < DI + b_ref.shape[-1]:
        return b_ref, c * w - DI
    return c_ref, c * w - DI - b_ref.shape[-1]


def _k1_ssd_prompt_body(x_ref, sh_ref, sc_ref, nw_ref, w_ref, wdt_ref, cw_ref, cb_ref, dtb_ref,
                        z_ref, xs_ref, b_ref, c_ref, dt_ref, nconv_ref, cbuf, *, tm, DI, CD, nt):
    t = pl.program_id(1)
    hb = _norm_mod(x_ref, sh_ref, sc_ref, nw_ref)
    w = 512
    for c in range(DI // w):
        z_ref[0, :, c * w:(c + 1) * w] = _dot(hb, w_ref[:, c * w:(c + 1) * w]).astype(z_ref.dtype)
    dt_ref[0] = _softplus(_dot(hb, wdt_ref[...]) + dtb_ref[...])

    @pl.when(t == 0)
    def _():
        cbuf[0:SUBLANES, :] = jnp.zeros((SUBLANES, CD), F32)

    for c in range(CD // w):
        cols = slice(c * w, (c + 1) * w)
        raw = _dot(hb, w_ref[:, DI + c * w:DI + (c + 1) * w])
        cbuf[SUBLANES:SUBLANES + tm, cols] = raw
        conv = cb_ref[:, cols] + cw_ref[3:4, cols] * raw
        for k in range(SSD_CONV - 1):
            conv = conv + cw_ref[k:k + 1, cols] * cbuf[SUBLANES - 3 + k:SUBLANES - 3 + k + tm, cols]
        dst, off = _ssd_route(c, DI, xs_ref, b_ref, c_ref)
        dst[0, :, off:off + w] = _silu(conv).astype(dst.dtype)

    @pl.when(t == nt - 1)
    def _():
        nconv_ref[0] = cbuf[tm + SUBLANES - 3:tm + SUBLANES, :]

    cbuf[0:SUBLANES, :] = cbuf[tm:tm + SUBLANES, :]


def _k1_ssd_prompt(x3, mod3, nw, w, wdt, cw, cb, dtb, tm, DI, GN):
    G, R, d = x3.shape
    CD = DI + 2 * GN
    nt = R // tm
    outs = [jax.ShapeDtypeStruct((G, R, DI), BF16), jax.ShapeDtypeStruct((G, R, DI), BF16),
            jax.ShapeDtypeStruct((G, R, GN), BF16), jax.ShapeDtypeStruct((G, R, GN), BF16),
            jax.ShapeDtypeStruct((G, R, LANES), F32), jax.ShapeDtypeStruct((G, SSD_CONV - 1, CD), F32)]
    return pl.pallas_call(
        functools.partial(_k1_ssd_prompt_body, tm=tm, DI=DI, CD=CD, nt=nt),
        grid=(G, nt),
        in_specs=_k1_specs(x3, mod3, 0, 1, tm) + [_const_spec(w.shape), _const_spec(wdt.shape),
                                                  _const_spec(cw.shape), _const_spec(cb.shape),
                                                  _const_spec(dtb.shape)],
        out_specs=[_row_spec(tm, DI), _row_spec(tm, DI), _row_spec(tm, GN), _row_spec(tm, GN),
                   _row_spec(tm, LANES), pl.BlockSpec((1, SSD_CONV - 1, CD), lambda g, t: (g, 0, 0))],
        out_shape=outs,
        scratch_shapes=[pltpu.VMEM((tm + SUBLANES, CD), F32)],
        compiler_params=_params(("arbitrary", "arbitrary")),
        name="ssd_inproj",
    )(x3, mod3, mod3, nw, w, wdt, cw, cb, dtb)


def _k1_ssd_sample_body(x_ref, sh_ref, sc_ref, nw_ref, w_ref, wdt_ref, cw_ref, cb_ref, dtb_ref, buf_ref,
                        alog_ref, exp_ref, z_ref, xs_ref, xdt_ref, b_ref, c_ref, dec_ref, nconv_ref,
                        *, DI, CD):
    hb = _norm_mod(x_ref, sh_ref, sc_ref, nw_ref)
    w = 512
    for c in range(DI // w):
        z_ref[0, :, c * w:(c + 1) * w] = _dot(hb, w_ref[:, c * w:(c + 1) * w])
    dt = _softplus(_dot(hb, wdt_ref[...]) + dtb_ref[...])
    dec_ref[0] = jnp.exp(dt * (-jnp.exp(alog_ref[...])))
    dt_rep = _dot_sel(_split3(dt), exp_ref[...])
    for c in range(CD // w):
        cols = slice(c * w, (c + 1) * w)
        raw = _dot(hb, w_ref[:, DI + c * w:DI + (c + 1) * w])
        conv = cb_ref[:, cols] + cw_ref[3:4, cols] * raw
        for k in range(SSD_CONV - 1):
            conv = conv + cw_ref[k:k + 1, cols] * buf_ref[k, :, cols]
        act = _silu(conv)
        dst, off = _ssd_route(c, DI, xs_ref, b_ref, c_ref)
        dst[0, :, off:off + w] = act
        if dst is xs_ref:
            xdt_ref[0, :, cols] = act * dt_rep[:, cols]
        nconv_ref[0, :, cols] = buf_ref[1, :, cols]
        nconv_ref[1, :, cols] = buf_ref[2, :, cols]
        nconv_ref[2, :, cols] = raw


def _k1_ssd_sample(x3, mod3, nw, w, wdt, cw, cb, dtb, buf_t, alog_row, expand, DI, GN):
    G, R, d = x3.shape
    CD = DI + 2 * GN
    tm = R
    outs = [jax.ShapeDtypeStruct((G, R, DI), F32), jax.ShapeDtypeStruct((G, R, DI), F32),
            jax.ShapeDtypeStruct((G, R, DI), F32),
            jax.ShapeDtypeStruct((G, R, GN), F32), jax.ShapeDtypeStruct((G, R, GN), F32),
            jax.ShapeDtypeStruct((G, R, LANES), F32), jax.ShapeDtypeStruct((SSD_CONV - 1, R, CD), F32)]
    return pl.pallas_call(
        functools.partial(_k1_ssd_sample_body, DI=DI, CD=CD),
        grid=(G, 1),
        in_specs=_k1_specs(x3, mod3, 0, 1, tm) + [_const_spec(w.shape), _const_spec(wdt.shape),
                                                  _const_spec(cw.shape), _const_spec(cb.shape),
                                                  _const_spec(dtb.shape), _const_spec(buf_t.shape),
                                                  _const_spec(alog_row.shape), _const_spec(expand.shape)],
        out_specs=[_row_spec(tm, DI), _row_spec(tm, DI), _row_spec(tm, DI), _row_spec(tm, GN),
                   _row_spec(tm, GN), _row_spec(tm, LANES),
                   pl.BlockSpec((SSD_CONV - 1, R, CD), lambda g, t: (0, 0, 0))],
        out_shape=outs,
        compiler_params=_params(("arbitrary", "arbitrary")),
        name="ssd_inproj_step",
    )(x3, mod3, mod3, nw, w, wdt, cw, cb, dtb, buf_t, alog_row, expand)


def _k2_ssd_body(z_ref, xs_ref, b_ref, c_ref, dt_ref, tri_ref, alog_ref, exp_ref, drep_ref, nw_ref,
                 y_ref, so_ref, s_scr, *, C, NH, P, N, NG, nc):
    ci = pl.program_id(1)

    @pl.when(ci == 0)
    def _():
        s_scr[...] = jnp.zeros_like(s_scr)

    hpg = NH // NG
    gw = hpg * P
    dt = dt_ref[0]
    la = dt * (-jnp.exp(alog_ref[...]))
    cum = _sel_dot(tri_ref[...], _split3(la))
    tot = cum[C - 1:C, :]
    c2 = cum * LOG2E
    r2_t = (c2 - jnp.log(dt) * LOG2E).T
    expand = exp_ref[...]
    w_state = _dot_sel(_split3(jnp.exp(tot - cum) * dt), expand)
    w_in = _dot_sel(_split3(jnp.exp(cum)), expand)
    xs = xs_ref[0]
    xsf = xs.astype(F32)
    xp = (xsf * w_state).astype(BF16)
    row = lax.broadcasted_iota(jnp.int32, (C, C), 0)
    colm = lax.broadcasted_iota(jnp.int32, (C, C), 1)
    causal = row >= colm
    lane = lax.broadcasted_iota(jnp.int32, (1, 2 * P), 1)
    head_keep = ((lane < P).astype(BF16), (lane >= P).astype(BF16))
    ys = []
    for g in range(NG):
        bg = b_ref[0, :, g * N:(g + 1) * N]
        cg = c_ref[0, :, g * N:(g + 1) * N]
        gsl = slice(g * gw, (g + 1) * gw)
        gmat = _dot_nt(cg, bg)
        state = s_scr[g]
        y_inter = _dot(cg, state.astype(BF16)) * w_in[:, gsl]
        s_scr[g] = state * w_in[C - 1:C, gsl] + _dot_tn(bg, xp[:, gsl])
        for hp in range(hpg // 2):
            h0 = g * hpg + 2 * hp
            xpair = xs[:, h0 * P:(h0 + 2) * P]
            acc = y_inter[:, 2 * hp * P:(2 * hp + 2) * P]
            for e in range(2):
                h = h0 + e
                seg2 = jnp.where(causal, c2[:, h:h + 1] - r2_t[h:h + 1, :], MASKED_EXPONENT)
                m = (gmat * jnp.exp2(seg2)).astype(BF16)
                acc = acc + _dot(m, xpair * head_keep[e])
            ys.append(acc)
    outs = _ssd_gate(jnp.concatenate(ys, axis=-1), xsf, z_ref[0].astype(F32), drep_ref[...], nw_ref[...], NG)
    for g in range(NG):
        y_ref[0, :, g * gw:(g + 1) * gw] = outs[g].astype(y_ref.dtype)

    @pl.when(ci == nc - 1)
    def _():
        for g in range(NG):
            so_ref[g] = s_scr[g].T


def _ssd_expand(NH, P):
    e = np.zeros((LANES, NH * P), np.float32)
    for h in range(NH):
        e[h, h * P:(h + 1) * P] = 1.0
    return jnp.asarray(e, BF16)


def _k2_ssd(z, xs, bm, cm, dt, alog_row, expand, drep, nw, NH, P, N, NG, nl, j, prev):
    G, R, DI = xs.shape
    C = min(C_SSD, R)
    nc = R // C
    i = np.arange(C)
    tri = jnp.asarray(i[:, None] >= i[None, :], BF16)
    gw = (NH // NG) * P
    return _call_into_stack(
        functools.partial(_k2_ssd_body, C=C, NH=NH, P=P, N=N, NG=NG, nc=nc),
        grid=(G, nc),
        inputs=[z, xs, bm, cm, dt, tri, alog_row, expand, drep, nw],
        in_specs=[_row_spec(C, DI), _row_spec(C, DI), _row_spec(C, NG * N), _row_spec(C, NG * N),
                  _row_spec(C, LANES), _const_spec(tri.shape), _const_spec(alog_row.shape),
                  _const_spec(expand.shape), _const_spec(drep.shape), _const_spec(nw.shape)],
        out_specs=[_row_spec(C, DI), _layer_state_spec((NG, gw, N), j)],
        out_shape=[jax.ShapeDtypeStruct((G, R, DI), BF16), jax.ShapeDtypeStruct((nl, G, NG, gw, N), F32)],
        scratch_shapes=[pltpu.VMEM((NG, N, gw), F32)],
        stack_out=1, prev=prev, name="ssd_core")


def _pad_rows(row, keep_first_only):
    x = jnp.broadcast_to(row, (MXU_PAD_ROWS, row.shape[-1]))
    if keep_first_only:
        first = lax.broadcasted_iota(jnp.int32, (MXU_PAD_ROWS, 1), 0) == 0
        x = jnp.where(first, x, 0.0)
    return x.astype(BF16)


def _dec_ssd_body(xdt_ref, b_ref, c_ref, dec_ref, s_ref, so_ref, o_ref, *, NH, P, N, NG, bt):
    hpg = NH // NG
    gw = hpg * P
    for i in range(bt):
        arow = dec_ref[i]
        v_pad = _pad_rows(xdt_ref[i], True)
        k_pad = _pad_rows(b_ref[i], False)
        q_pad = _pad_rows(c_ref[i], False)
        for g in range(NG):
            upd = _dot_tn(v_pad[:, g * gw:(g + 1) * gw], k_pad[:, g * N:(g + 1) * N])
            news = []
            for hl in range(hpg):
                h = g * hpg + hl
                rows = slice(hl * P, (hl + 1) * P)
                new = arow[:, h:h + 1] * s_ref[i, g, rows, :] + upd[rows, :]
                so_ref[i, g, rows, :] = new
                news.append(new.astype(BF16))
            o = _dot_nt(q_pad[:, g * N:(g + 1) * N], jnp.concatenate(news, axis=0))
            o_ref[i, :, g * gw:(g + 1) * gw] = o[0:1]


def _dec_ssd(xdt, bm, cm, dec, states_t, NH, P, N, NG, j, prev):
    B = xdt.shape[0]
    tail = states_t.shape[2:]
    bt = _step_batch(B, BT_SSD)
    return _call_into_stack(
        functools.partial(_dec_ssd_body, NH=NH, P=P, N=N, NG=NG, bt=bt),
        grid=(B // bt,),
        inputs=[_as_steps(xdt), _as_steps(bm), _as_steps(cm), _as_steps(dec), states_t],
        in_specs=[_step_spec(NH * P, bt), _step_spec(NG * N, bt), _step_spec(NG * N, bt),
                  _step_spec(LANES, bt), _layer_state_spec(tail, j, bt)],
        out_specs=[_layer_state_spec(tail, j, bt), _step_spec(NH * P, bt)],
        out_shape=[jax.ShapeDtypeStruct(states_t.shape, F32), jax.ShapeDtypeStruct((B, 1, NH * P), F32)],
        stack_out=0, prev=prev, name="ssd_step")


def _hg_lower_bound(lbl_ref, layer):
    logits = lbl_ref[...]
    e = jnp.exp(logits - jnp.max(logits, axis=0, keepdims=True))
    sm = e / jnp.sum(e, axis=0, keepdims=True)
    lb = jnp.zeros_like(sm[0:1])
    for i in range(1, layer + 1):
        lb = lb + sm[i:i + 1]
    return lb


def _k1_hg_body(x_ref, sh_ref, sc_ref, nw_ref, w_ref, lbl_ref, *out_refs, layer, KD, VD, prompt):
    hb = _norm_mod(x_ref, sh_ref, sc_ref, nw_ref)
    lb = _hg_lower_bound(lbl_ref, layer)
    w = 512
    if prompt:
        q_ref, la_ref, v_ref, g_ref = out_refs
    else:
        q_ref, k_ref, a_ref, v_ref, g_ref = out_refs
    for c in range(KD // w):
        cols = slice(c * w, (c + 1) * w)
        q = _silu(_dot(hb, w_ref[:, c * w:(c + 1) * w])) * (HG_KEY_DIM ** -0.5)
        q_ref[0, :, cols] = q.astype(q_ref.dtype)
        lbc = lb[:, cols]
        f = lbc + (1.0 - lbc) * jax.nn.sigmoid(_dot(hb, w_ref[:, KD + c * w:KD + (c + 1) * w]))
        if prompt:
            la_ref[0, :, cols] = jnp.log(f)
        else:
            k_ref[0, :, cols] = 1.0 - f
            a_ref[0, :, cols] = f
    for c in range(VD // w):
        cols = slice(c * w, (c + 1) * w)
        v_ref[0, :, cols] = _dot(hb, w_ref[:, 2 * KD + c * w:2 * KD + (c + 1) * w]).astype(v_ref.dtype)
        g_ref[0, :, cols] = _dot(hb, w_ref[:, 2 * KD + VD + c * w:2 * KD + VD + (c + 1) * w]).astype(g_ref.dtype)


def _k1_hg(x3, mod3, nw, w, lbl, tm, layer, KD, VD, prompt):
    G, R, d = x3.shape
    if prompt:
        dts = [(KD, BF16), (KD, F32), (VD, BF16), (VD, BF16)]
    else:
        dts = [(KD, F32), (KD, F32), (KD, F32), (VD, F32), (VD, F32)]
    return pl.pallas_call(
        functools.partial(_k1_hg_body, layer=layer, KD=KD, VD=VD, prompt=prompt),
        grid=(G, R // tm),
        in_specs=_k1_specs(x3, mod3, 0, 1, tm) + [_const_spec(w.shape), _const_spec(lbl.shape)],
        out_specs=[_row_spec(tm, n) for n, _ in dts],
        out_shape=[jax.ShapeDtypeStruct((G, R, n), dt) for n, dt in dts],
        compiler_params=_params(("arbitrary", "arbitrary")),
        name="hg_inproj" if prompt else "hg_inproj_step",
    )(x3, mod3, mod3, nw, w, lbl)


def _hg_masks(C):
    i = np.arange(C)
    r, t = i[:, None], i[None, :]
    sels = [t <= r, (t <= r) & (t >= HG_SUB * (r // HG_SUB))]
    pairs = []
    s = C // 2
    while s >= HG_SUB:
        mid = 2 * s * (r // (2 * s)) + s - 1
        sels.append(np.where(r > mid, (t > mid) & (t <= r), (t > r) & (t <= mid)))
        pairs.append((r // (2 * s) == t // (2 * s)) & (r % (2 * s) >= s) & (t % (2 * s) < s))
        s //= 2
    pairs.append((r // HG_SUB == t // HG_SUB) & (t <= r))
    return (jnp.asarray(np.stack(sels), BF16), jnp.asarray(np.stack(pairs), F32))


def _hg_diag_collect(C, K):
    e = np.zeros((HG_SUB, K, C), np.float32)
    for j in range(HG_SUB):
        e[j, :, j::HG_SUB] = 1.0
    return jnp.asarray(e.reshape(HG_SUB * K, C), BF16)


def _k2_hg_body(q_ref, la_ref, v_ref, g_ref, sel_ref, pair_ref, coll_ref, nw_ref, y_ref, so_ref,
                st_scr, k_scr, c_scr, p_scr, *, C, H, K, V, nlev, nc):
    ci = pl.program_id(1)

    @pl.when(ci == 0)
    def _():
        st_scr[...] = jnp.zeros_like(st_scr)

    D = H * K
    la = la_ref[0]
    parts = _split3(la)
    cum = _sel_dot(sel_ref[0], parts)
    cw = _sel_dot(sel_ref[1], parts)
    kf = 1.0 - jnp.exp(la)
    qf = q_ref[0].astype(F32)
    vb = v_ref[0]
    tot = cum[C - 1:C, :]
    q_in = (qf * jnp.exp(cum)).astype(BF16)
    k_out = (kf * jnp.exp(tot - cum)).astype(BF16)
    e_tot = jnp.exp(tot)

    scores = [jnp.zeros((C, C), F32) for _ in range(H)]
    for lev in range(nlev):
        a = jnp.exp(_sel_dot(sel_ref[2 + lev], parts))
        ql = (qf * a).astype(BF16)
        kl = (kf * a).astype(BF16)
        pm = pair_ref[lev] > 0.5
        for h in range(H):
            sl = slice(h * K, (h + 1) * K)
            scores[h] = scores[h] + jnp.where(pm, _dot_nt(ql[:, sl], kl[:, sl]), 0.0)

    cw2 = cw * LOG2E
    k_scr[...] = kf
    c_scr[...] = cw2
    rowmod = lax.broadcasted_iota(jnp.int32, (C, D), 0) & (HG_SUB - 1)
    nblk = C // HG_SUB
    for j in range(HG_SUB):
        kb = jnp.concatenate([jnp.broadcast_to(k_scr[b * HG_SUB + j:b * HG_SUB + j + 1, :], (HG_SUB, D))
                              for b in range(nblk)], axis=0)
        cb = jnp.concatenate([jnp.broadcast_to(c_scr[b * HG_SUB + j:b * HG_SUB + j + 1, :], (HG_SUB, D))
                              for b in range(nblk)], axis=0)
        dec = jnp.exp2(jnp.where(rowmod >= j, cw2 - cb, MASKED_EXPONENT))
        p = (qf * kb * dec).astype(BF16)
        for h in range(H):
            p_scr[h * C:(h + 1) * C, j * K:(j + 1) * K] = p[:, h * K:(h + 1) * K]
    diag = _dot(p_scr[...], coll_ref[...])
    dm = pair_ref[nlev] > 0.5

    nw = nw_ref[...]
    for h in range(H):
        sl = slice(h * K, (h + 1) * K)
        vsl = slice(h * V, (h + 1) * V)
        st = st_scr[h]
        sc = scores[h] + jnp.where(dm, diag[h * C:(h + 1) * C, :], 0.0)
        o = _dot(sc.astype(BF16), vb[:, vsl]) + _dot_nt(q_in[:, sl], st.astype(BF16))
        st_scr[h] = st * e_tot[:, sl] + _dot_tn(vb[:, vsl], k_out[:, sl])
        gh = g_ref[0, :, vsl].astype(F32)
        y_ref[0, :, vsl] = (_rms(o) * nw[:, vsl] * _silu(gh)).astype(y_ref.dtype)

    @pl.when(ci == nc - 1)
    def _():
        for h in range(H):
            so_ref[h] = st_scr[h].T


def _k2_hg(q, la, v, g, nw_rep, H, K, V, nl, j, prev):
    G, R, D = q.shape
    C = min(C_HG, R)
    nc = R // C
    sels, pairs = _hg_masks(C)
    nlev = pairs.shape[0] - 1
    coll = _hg_diag_collect(C, K)
    return _call_into_stack(
        functools.partial(_k2_hg_body, C=C, H=H, K=K, V=V, nlev=nlev, nc=nc),
        grid=(G, nc),
        inputs=[q, la, v, g, sels, pairs, coll, nw_rep],
        in_specs=[_row_spec(C, D), _row_spec(C, D), _row_spec(C, H * V), _row_spec(C, H * V),
                  _const_spec(sels.shape), _const_spec(pairs.shape), _const_spec(coll.shape),
                  _const_spec(nw_rep.shape)],
        out_specs=[_row_spec(C, H * V), _layer_state_spec((H, K, V), j)],
        out_shape=[jax.ShapeDtypeStruct((G, R, H * V), BF16), jax.ShapeDtypeStruct((nl, G, H, K, V), F32)],
        scratch_shapes=[pltpu.VMEM((H, V, K), F32), pltpu.VMEM((C, D), F32), pltpu.VMEM((C, D), F32),
                        pltpu.VMEM((H * C, HG_SUB * K), BF16)],
        stack_out=1, prev=prev, name="hg_core")


def _dec_hg_body(q_ref, k_ref, a_ref, v_ref, s_ref, so_ref, o_ref, *, H, K, V, bt):
    eye = lax.broadcasted_iota(jnp.int32, (K, K), 0) == lax.broadcasted_iota(jnp.int32, (K, K), 1)
    for i in range(bt):
        arow = a_ref[i]
        q_pad = _pad_rows(q_ref[i], False)
        k_pad = _pad_rows(k_ref[i], True)
        v_pad = _pad_rows(v_ref[i], False)
        for h in range(H):
            sl = slice(h * K, (h + 1) * K)
            vsl = slice(h * V, (h + 1) * V)
            ac = _col(arow[:, sl], eye)
            new = ac * s_ref[i, h] + _dot_tn(k_pad[:, sl], v_pad[:, vsl])
            so_ref[i, h] = new
            o_ref[i, :, vsl] = _dot(q_pad[:, sl], new.astype(BF16))[0:1]


def _dec_hg(q, k, a, v, states, H, K, V, j, prev):
    B = q.shape[0]
    bt = _step_batch(B, BT_HG)
    return _call_into_stack(
        functools.partial(_dec_hg_body, H=H, K=K, V=V, bt=bt),
        grid=(B // bt,),
        inputs=[_as_steps(q), _as_steps(k), _as_steps(a), _as_steps(v), states],
        in_specs=[_step_spec(H * K, bt), _step_spec(H * K, bt), _step_spec(H * K, bt), _step_spec(H * V, bt),
                  _layer_state_spec((H, K, V), j, bt)],
        out_specs=[_layer_state_spec((H, K, V), j, bt), _step_spec(H * V, bt)],
        out_shape=[jax.ShapeDtypeStruct(states.shape, F32), jax.ShapeDtypeStruct((B, 1, H * V), F32)],
        stack_out=0, prev=prev, name="hg_step")


def _f32_tile(ref):
    return ref[0].astype(F32)


def _pro_identity(y_ref):
    return y_ref[0]


def _pro_ret(o_ref, g_ref, *, H, V):
    return jnp.concatenate(_ret_gate(_f32_tile(o_ref), _f32_tile(g_ref), H, V), axis=-1).astype(BF16)


def _pro_ssd(o_ref, xs_ref, z_ref, drep_ref, nw_ref, *, NG):
    return jnp.concatenate(_ssd_gate(_f32_tile(o_ref), _f32_tile(xs_ref), _f32_tile(z_ref), drep_ref[...],
                                     nw_ref[...], NG), axis=-1).astype(BF16)


def _pro_hg(o_ref, g_ref, nw_ref, *, H, V):
    return jnp.concatenate(_hg_gate(_f32_tile(o_ref), _f32_tile(g_ref), nw_ref[...], H, V),
                           axis=-1).astype(BF16)


def _swiglu(hb, win_ref, wout_ref, F, bounds):
    acc = None
    for lo, hi in bounds:
        gt = _dot(hb, win_ref[:, lo:hi])
        up = _dot(hb, win_ref[:, F + lo:F + hi])
        a = (_silu(gt) * up).astype(BF16)
        part = _dot(a, wout_ref[lo:hi, :])
        acc = part if acc is None else acc + part
    return acc


def _k34_body(*refs, prologue, n_tile, n_const, F, bounds):
    n_pro = n_tile + n_const
    (wo_ref, x_ref, gm_ref, nwm_ref, sh_ref, sc_ref, nwf_ref, gf_ref, nwp_ref,
     win_ref, wout_ref, o_ref) = refs[n_pro:]
    y = prologue(*refs[:n_pro])
    x = x_ref[0] + gm_ref[0] * (_rms(_dot(y, wo_ref[...])) * nwm_ref[...])
    hb = (_rms(x) * nwf_ref[...] * (1.0 + sc_ref[0]) + sh_ref[0]).astype(BF16)
    acc = _swiglu(hb, win_ref, wout_ref, F, bounds)
    o_ref[0] = x + gf_ref[0] * (_rms(acc) * nwp_ref[...])


def _k34(prologue, tile_args, const_args, w_out, x3, mod3, nw_mix_post, nw_pre, nw_post, win, wout, tm):
    G, R, d = x3.shape
    rm = mod3.shape[1]
    F = wout.shape[0]
    mod_spec = lambda idx: pl.BlockSpec((1, rm, d), lambda g, t: (g, 0, idx))
    row = _const_spec((1, d))
    return pl.pallas_call(
        functools.partial(_k34_body, prologue=prologue, n_tile=len(tile_args), n_const=len(const_args),
                          F=F, bounds=_ffn_bounds(F)),
        grid=(G, R // tm),
        in_specs=[_row_spec(tm, a.shape[-1]) for a in tile_args] + [_const_spec(c.shape) for c in const_args]
        + [_const_spec(w_out.shape), _row_spec(tm, d), mod_spec(2), row, mod_spec(3), mod_spec(4), row,
           mod_spec(5), row, _const_spec(win.shape), _const_spec(wout.shape)],
        out_specs=_row_spec(tm, d),
        out_shape=jax.ShapeDtypeStruct((G, R, d), F32),
        compiler_params=_params(("arbitrary", "arbitrary")),
        name="outproj_ffn",
    )(*tile_args, *const_args, w_out, x3, mod3, nw_mix_post, mod3, mod3, nw_pre, mod3, nw_post, win, wout)


def _ffn_bounds(F):
    tiles, rem = divmod(F, MXU_TILE)
    assert rem == 0, F
    n = -(-tiles // FFN_MAX_CHUNK_TILES)
    sizes = [tiles // n + (1 if i < tiles % n else 0) for i in range(n)]
    edges = np.cumsum([0] + sizes) * MXU_TILE
    return tuple((int(a), int(b)) for a, b in zip(edges[:-1], edges[1:]))


def _rope_table(pos, half):
    inv = 1.0 / (ROPE_BASE ** (jnp.arange(half, dtype=F32) / half))
    ang = pos.astype(F32)[:, None] * inv[None, :]
    return jnp.cos(ang), jnp.sin(ang)


def kernel(x_prompt, x_sample, c_prompt, c_sample, state_ret, state_ssd, state_conv, state_hgrn, w_ada, b_ada, norm_mix_pre, norm_mix_post, norm_ffn_pre, norm_ffn_post, ret_w_in, ret_w_out, ssd_w_in, ssd_conv_w, ssd_conv_b, ssd_dt_bias, ssd_a_log, ssd_d, ssd_norm, ssd_w_out, hg_w_in, hg_lb_logits, hg_norm, hg_w_out, ffn_w_in, ffn_w_out):
    bp, seq, d = x_prompt.shape
    bs = x_sample.shape[0]
    depth = w_ada.shape[0]
    ret_h = d // 256
    ret_k = d // ret_h
    ret_v = 2 * ret_k
    di = 2 * d
    ssd_nh = di // SSD_HEAD_DIM
    gn = SSD_GROUPS * SSD_STATE
    hg_h = d // HG_KEY_DIM
    hg_v = d // hg_h
    kd = hg_h * HG_KEY_DIM
    vd = hg_h * hg_v
    tm = min(TM_PROMPT, seq)

    mod = _adaln(jnp.concatenate([c_prompt, c_sample], axis=0), w_ada, b_ada)
    xp = x_prompt.astype(F32)
    xs = x_sample.astype(F32).reshape(1, bs, d)
    cos_p, sin_p = _rope_table(jnp.arange(seq, dtype=jnp.int32), ret_k // 2)
    cos_s, sin_s = _rope_table(jnp.full((1,), PAST_LEN, jnp.int32), ret_k // 2)

    hpg = ssd_nh // SSD_GROUPS
    ssd_states_t = jnp.swapaxes(state_ssd.astype(F32), 3, 4).reshape(
        state_ssd.shape[0], bs, SSD_GROUPS, hpg * SSD_HEAD_DIM, SSD_STATE)

    def ssd_untranspose(s):
        s = s.reshape(s.shape[0], s.shape[1], ssd_nh, SSD_HEAD_DIM, SSD_STATE)
        return jnp.swapaxes(s, 3, 4).astype(state_ssd.dtype)

    new = {k: [] for k in ("conv_p", "conv_s")}
    stk = {k: None for k in ("ret_p", "ret_s", "ssd_p", "ssd_s", "hg_p", "hg_s")}
    n_kind = [sum(1 for l in range(depth) if l % 3 == kind) for kind in range(3)]
    counts = [0, 0, 0]
    for layer in range(depth):
        mod_p = mod[layer, :bp].reshape(bp, 1, 6 * d)
        mod_s = mod[layer, bp:].reshape(1, bs, 6 * d)
        nw_pre = norm_mix_pre[layer].reshape(1, d)
        nw_post = norm_mix_post[layer].reshape(1, d)
        kind = layer % 3
        j = counts[kind]
        counts[kind] += 1
        if kind == 0:
            w_in = ret_w_in[j].astype(BF16)
            w_out = ret_w_out[j].astype(BF16)
            prologue = functools.partial(_pro_ret, H=ret_h, V=ret_v)
            q, k, v, g = _k1_ret(xp, mod_p, nw_pre, w_in, cos_p, sin_p, tm, BF16, ret_h, ret_k, ret_v)
            y, stk["ret_p"] = _k2_ret(q, k, v, g, ret_h, ret_k, ret_v, n_kind[0], j, stk["ret_p"])
            q, k, v, g = _k1_ret(xs, mod_s, nw_pre, w_in, cos_s, sin_s, bs, F32, ret_h, ret_k, ret_v)
            stk["ret_s"], o = _dec_ret(q[0], k[0], v[0], state_ret, ret_h, ret_k, ret_v, j, stk["ret_s"])
            pro_s = ([o.reshape(1, bs, -1), g], [])
        elif kind == 1:
            w_main = ssd_w_in[j][:, :di + di + 2 * gn].astype(BF16)
            w_dt = jnp.pad(ssd_w_in[j][:, di + di + 2 * gn:], ((0, 0), (0, LANES - ssd_nh))).astype(BF16)
            w_out = ssd_w_out[j].astype(BF16)
            cw = ssd_conv_w[j]
            cb = ssd_conv_b[j].reshape(1, -1)
            dtb = jnp.pad(ssd_dt_bias[j], (0, LANES - ssd_nh)).reshape(1, LANES)
            alog = jnp.pad(ssd_a_log[j].astype(F32), (0, LANES - ssd_nh)).reshape(1, LANES)
            drep = jnp.repeat(ssd_d[j], SSD_HEAD_DIM).reshape(1, di)
            nw_ssd = ssd_norm[j].reshape(1, di)
            z, xc, bm, cm, dt, nconv = _k1_ssd_prompt(xp, mod_p, nw_pre, w_main, w_dt, cw, cb, dtb, tm, di, gn)
            new["conv_p"].append(nconv)
            expand = _ssd_expand(ssd_nh, SSD_HEAD_DIM)
            prologue = functools.partial(_pro_ssd, NG=SSD_GROUPS)
            y, stk["ssd_p"] = _k2_ssd(z, xc, bm, cm, dt, alog, expand, drep, nw_ssd, ssd_nh, SSD_HEAD_DIM,
                                      SSD_STATE, SSD_GROUPS, n_kind[1], j, stk["ssd_p"])
            buf_t = jnp.transpose(state_conv[j].astype(F32), (1, 0, 2))
            z, xc, xdt, bm, cm, dec, nconv = _k1_ssd_sample(xs, mod_s, nw_pre, w_main, w_dt, cw, cb, dtb, buf_t,
                                                            alog, expand, di, gn)
            new["conv_s"].append(jnp.transpose(nconv, (1, 0, 2)))
            stk["ssd_s"], o = _dec_ssd(xdt[0], bm[0], cm[0], dec[0], ssd_states_t,
                                       ssd_nh, SSD_HEAD_DIM, SSD_STATE, SSD_GROUPS, j, stk["ssd_s"])
            pro_s = ([o.reshape(1, bs, -1), xc, z], [drep, nw_ssd])
        else:
            w_in = hg_w_in[j].astype(BF16)
            w_out = hg_w_out[j].astype(BF16)
            nw_hg = jnp.tile(hg_norm[j], hg_h).reshape(1, vd)
            lbl = hg_lb_logits.astype(F32)
            prologue = functools.partial(_pro_hg, H=hg_h, V=hg_v)
            q, la, v, g = _k1_hg(xp, mod_p, nw_pre, w_in, lbl, tm, layer, kd, vd, True)
            y, stk["hg_p"] = _k2_hg(q, la, v, g, nw_hg, hg_h, HG_KEY_DIM, hg_v, n_kind[2], j, stk["hg_p"])
            q, k, a, v, g = _k1_hg(xs, mod_s, nw_pre, w_in, lbl, bs, layer, kd, vd, False)
            stk["hg_s"], o = _dec_hg(q[0], k[0], a[0], v[0], state_hgrn, hg_h, HG_KEY_DIM, hg_v, j, stk["hg_s"])
            pro_s = ([o.reshape(1, bs, -1), g], [nw_hg])
        nf_pre = norm_ffn_pre[layer].reshape(1, d)
        nf_post = norm_ffn_post[layer].reshape(1, d)
        f_in = ffn_w_in[layer].astype(BF16)
        f_out = ffn_w_out[layer].astype(BF16)
        xp = _k34(_pro_identity, [y], [], w_out, xp, mod_p, nw_post, nf_pre, nf_post, f_in, f_out, tm)
        xs = _k34(prologue, *pro_s, w_out, xs, mod_s, nw_post, nf_pre, nf_post, f_in, f_out, bs)

    stack = lambda name, like: jnp.stack(new[name]).astype(like.dtype)
    return (xp.astype(x_prompt.dtype), xs.reshape(bs, 1, d).astype(x_sample.dtype),
            stk["ret_p"].astype(state_ret.dtype), stk["ret_s"].astype(state_ret.dtype),
            ssd_untranspose(stk["ssd_p"]), ssd_untranspose(stk["ssd_s"]),
            stack("conv_p", state_conv), stack("conv_s", state_conv),
            stk["hg_p"].astype(state_hgrn.dtype), stk["hg_s"].astype(state_hgrn.dtype))
```

```python
import functools

import numpy as np
import jax
import jax.numpy as jnp
from jax import lax
from jax.experimental import pallas as pl
from jax.experimental.pallas import tpu as pltpu

F32 = jnp.float32
BF16 = jnp.bfloat16
EPS = 1e-6
PAST_LEN = 16384
ROPE_BASE = 10000.0
SSD_HEAD_DIM = 64
SSD_GROUPS = 4
SSD_STATE = 128
SSD_CONV = 4
HG_KEY_DIM = 128
LANES = 128
SUBLANES = 8
VMEM_LIMIT = 56 * 1024 * 1024
TM_PROMPT = 512
MXU_TILE = 256
MXU_PAD_ROWS = 16
C_RET = 256
C_SSD = 128
C_HG = 128
HG_SUB = 8
FFN_MAX_CHUNK_TILES = 6
LOG2E = 1.4426950408889634
MASKED_EXPONENT = -1e30
BT_RET = 2
BT_SSD = 2
BT_HG = 4


def _rms(x):
    return x * lax.rsqrt(jnp.mean(x * x, axis=-1, keepdims=True) + EPS)


def _silu(x):
    h = 0.5 * x
    return h + h * jnp.tanh(h)


def _softplus(x):
    return jnp.maximum(x, 0.0) + jnp.log1p(jnp.exp(-jnp.abs(x)))


def _dot(a, b):
    return jnp.dot(a, b, preferred_element_type=F32)


def _dot_nt(a, b):
    return lax.dot_general(a, b, (((1,), (1,)), ((), ())), preferred_element_type=F32)


def _dot_tn(a, b):
    return lax.dot_general(a, b, (((0,), (0,)), ((), ())), preferred_element_type=F32)


def _split3(x):
    hi = x.astype(BF16)
    r = x - hi.astype(F32)
    mid = r.astype(BF16)
    lo = (r - mid.astype(F32)).astype(BF16)
    return hi, mid, lo


def _sel_dot(sel, parts):
    acc = _dot(sel, parts[0])
    for p in parts[1:]:
        acc = acc + _dot(sel, p)
    return acc


def _dot_sel(parts, sel):
    acc = _dot(parts[0], sel)
    for p in parts[1:]:
        acc = acc + _dot(p, sel)
    return acc


def _col(row, eye):
    return jnp.sum(jnp.where(eye, row, 0.0), axis=1, keepdims=True)


def _norm_mod(x_ref, sh_ref, sc_ref, nw_ref):
    x = x_ref[0]
    h = _rms(x) * nw_ref[...] * (1.0 + sc_ref[0]) + sh_ref[0]
    return h.astype(BF16)


def _params(sem):
    return pltpu.CompilerParams(dimension_semantics=sem, vmem_limit_bytes=VMEM_LIMIT)


def _const_spec(shape):
    nd = len(shape)
    return pl.BlockSpec(shape, lambda *_: (0,) * nd, pipeline_mode=pl.Buffered(1))


def _w_spec(wl):
    stack, j = wl
    return pl.BlockSpec((None,) + stack.shape[1:], lambda *_: (j, 0, 0), pipeline_mode=pl.Buffered(1))


def _ret_gate(o, g, H, V):
    outs = []
    for h in range(H):
        outs.append(_rms(o[:, h * V:(h + 1) * V]) * _silu(g[:, h * V:(h + 1) * V]))
    return outs


def _ssd_gate(y, xs, z, d_rep, nw, groups):
    y = (y + d_rep * xs) * _silu(z)
    w = y.shape[-1] // groups
    outs = []
    for g in range(groups):
        outs.append(_rms(y[:, g * w:(g + 1) * w]) * nw[:, g * w:(g + 1) * w])
    return outs


def _hg_gate(o, g, nw, H, V):
    outs = []
    for h in range(H):
        sl = slice(h * V, (h + 1) * V)
        outs.append(_rms(o[:, sl]) * nw[:, sl] * _silu(g[:, sl]))
    return outs


def _adaln_body(c_ref, w_ref, b_ref, o_ref):
    a = _silu(c_ref[...]).astype(BF16)
    o_ref[0] = _dot(a, w_ref[0].astype(BF16)) + b_ref[0]


def _adaln(c_all, w_ada, b_ada):
    depth, d, n = w_ada.shape
    bc = c_all.shape[0]
    tn = 1024
    return pl.pallas_call(
        _adaln_body,
        grid=(depth, n // tn),
        in_specs=[
            pl.BlockSpec((bc, d), lambda l, j: (0, 0)),
            pl.BlockSpec((1, d, tn), lambda l, j: (l, 0, j)),
            pl.BlockSpec((1, 1, tn), lambda l, j: (l, 0, j)),
        ],
        out_specs=pl.BlockSpec((1, bc, tn), lambda l, j: (l, 0, j)),
        out_shape=jax.ShapeDtypeStruct((depth, bc, n), F32),
        compiler_params=_params(("arbitrary", "arbitrary")),
        name="adaln",
    )(c_all, w_ada, b_ada.reshape(depth, 1, n))


def _k1_specs(x3, mod3, sh_idx, sc_idx, tm):
    _, _, d = x3.shape
    rm = mod3.shape[1]
    return [
        pl.BlockSpec((1, tm, d), lambda g, t: (g, t, 0)),
        pl.BlockSpec((1, rm, d), lambda g, t: (g, 0, sh_idx)),
        pl.BlockSpec((1, rm, d), lambda g, t: (g, 0, sc_idx)),
        _const_spec((1, d)),
    ]


def _row_spec(tm, n):
    return pl.BlockSpec((1, tm, n), lambda g, t: (g, t, 0))


def _layer_state_spec(tail, j, bt=None):
    nd = len(tail)
    return pl.BlockSpec((None, bt) + tuple(tail), lambda b, *_: (j, b) + (0,) * nd)


def _call_into_stack(body, *, grid, inputs, in_specs, out_specs, out_shape, stack_out, prev, name,
                     scratch_shapes=()):
    inputs, in_specs = list(inputs), list(in_specs)
    n_in = len(inputs)
    aliases = {}
    kernel_fn = body
    if prev is not None:
        inputs.append(prev)
        in_specs.append(pl.BlockSpec(memory_space=pl.ANY))
        aliases = {n_in: stack_out}

        def kernel_fn(*refs):
            body(*refs[:n_in], *refs[n_in + 1:])

    return pl.pallas_call(
        kernel_fn, grid=grid, in_specs=in_specs, out_specs=out_specs, out_shape=out_shape,
        scratch_shapes=scratch_shapes, input_output_aliases=aliases,
        compiler_params=_params(("arbitrary",) * len(grid)), name=name)(*inputs)


def _k1_ret_body(x_ref, sh_ref, sc_ref, nw_ref, w_ref, cos_ref, sin_ref,
                 q_ref, k_ref, v_ref, g_ref, *, H, K, V):
    hb = _norm_mod(x_ref, sh_ref, sc_ref, nw_ref)
    cos = cos_ref[...]
    sin = sin_ref[...]
    half = K // 2
    for h in range(H):
        for dst, base, scale in ((q_ref, 0, None), (k_ref, H * K, K ** -0.5)):
            a = _dot(hb, w_ref[:, base + h * K:base + (h + 1) * K])
            x1 = a[:, :half]
            x2 = a[:, half:]
            r1 = x1 * cos - x2 * sin
            r2 = x2 * cos + x1 * sin
            if scale is not None:
                r1 = r1 * scale
                r2 = r2 * scale
            dst[0, :, h * K:h * K + half] = r1.astype(dst.dtype)
            dst[0, :, h * K + half:(h + 1) * K] = r2.astype(dst.dtype)
    vb = 2 * H * K
    for c in range(H):
        v_ref[0, :, c * V:(c + 1) * V] = _dot(hb, w_ref[:, vb + c * V:vb + (c + 1) * V]).astype(v_ref.dtype)
        g_ref[0, :, c * V:(c + 1) * V] = _dot(
            hb, w_ref[:, vb + H * V + c * V:vb + H * V + (c + 1) * V]).astype(g_ref.dtype)


def _k1_ret(x3, mod3, nw, w, cos, sin, tm, out_dtype, H, K, V):
    G, R, d = x3.shape
    rc = cos.shape[0]
    cs_spec = (pl.BlockSpec((tm, K // 2), lambda g, t: (t, 0)) if rc == R
               else _const_spec((1, K // 2)))
    outs = [jax.ShapeDtypeStruct((G, R, H * K), out_dtype), jax.ShapeDtypeStruct((G, R, H * K), out_dtype),
            jax.ShapeDtypeStruct((G, R, H * V), out_dtype), jax.ShapeDtypeStruct((G, R, H * V), out_dtype)]
    return pl.pallas_call(
        functools.partial(_k1_ret_body, H=H, K=K, V=V),
        grid=(G, R // tm),
        in_specs=_k1_specs(x3, mod3, 0, 1, tm) + [_w_spec(w), cs_spec, cs_spec],
        out_specs=[_row_spec(tm, H * K), _row_spec(tm, H * K), _row_spec(tm, H * V), _row_spec(tm, H * V)],
        out_shape=outs,
        compiler_params=_params(("arbitrary", "arbitrary")),
        name="ret_inproj",
    )(x3, mod3, mod3, nw, w[0], cos, sin)


def _k2_ret_body(q_ref, k_ref, v_ref, g_ref, dmat_ref, qdec_ref, kdec_ref, y_ref, s_ref, *, H, K, V, sdec):
    @pl.when(pl.program_id(1) == 0)
    def _():
        s_ref[...] = jnp.zeros_like(s_ref)

    for h in range(H):
        qh = q_ref[0, :, h * K:(h + 1) * K]
        kh = k_ref[0, :, h * K:(h + 1) * K]
        vh = v_ref[0, :, h * V:(h + 1) * V]
        state = s_ref[h]
        scores = _dot_nt(qh, kh) * dmat_ref[h]
        o = _dot(scores.astype(BF16), vh) + _dot(qh, state.astype(BF16)) * qdec_ref[h]
        k_out = (kh.astype(F32) * kdec_ref[h]).astype(BF16)
        s_ref[h] = sdec[h] * state + _dot_tn(k_out, vh)
        gh = g_ref[0, :, h * V:(h + 1) * V].astype(F32)
        y_ref[0, :, h * V:(h + 1) * V] = (_rms(o) * _silu(gh)).astype(y_ref.dtype)


def _ret_tables(H, K, V, C):
    lg = np.log1p(-np.exp2(-5.0 - np.arange(H, dtype=np.float64)))
    i = np.arange(C, dtype=np.float64)
    diff = i[:, None] - i[None, :]
    dmat = np.where(diff >= 0, np.exp(lg[:, None, None] * np.maximum(diff, 0.0)), 0.0)
    qdec = np.broadcast_to(np.exp(lg[:, None, None] * (i[None, :, None] + 1.0)), (H, C, V))
    kdec = np.broadcast_to(np.exp(lg[:, None, None] * (C - 1.0 - i[None, :, None])), (H, C, K))
    sdec = tuple(float(np.exp(l * C)) for l in lg)
    return (jnp.asarray(dmat, F32), jnp.asarray(qdec, F32), jnp.asarray(kdec, F32), sdec)


def _k2_ret(q, k, v, g, H, K, V, nl, j, prev):
    G, R, _ = q.shape
    C = min(C_RET, R)
    dmat, qdec, kdec, sdec = _ret_tables(H, K, V, C)
    return _call_into_stack(
        functools.partial(_k2_ret_body, H=H, K=K, V=V, sdec=sdec),
        grid=(G, R // C),
        inputs=[q, k, v, g, dmat, qdec, kdec],
        in_specs=[_row_spec(C, H * K), _row_spec(C, H * K), _row_spec(C, H * V), _row_spec(C, H * V),
                  _const_spec(dmat.shape), _const_spec(qdec.shape), _const_spec(kdec.shape)],
        out_specs=[_row_spec(C, H * V), _layer_state_spec((H, K, V), j)],
        out_shape=[jax.ShapeDtypeStruct((G, R, H * V), BF16), jax.ShapeDtypeStruct((nl, G, H, K, V), F32)],
        stack_out=1, prev=prev, name="ret_core")


def _dec_ret_body(q_ref, k_ref, v_ref, s_ref, so_ref, o_ref, *, H, K, V, gammas, bt):
    eye = lax.broadcasted_iota(jnp.int32, (K, K), 0) == lax.broadcasted_iota(jnp.int32, (K, K), 1)
    for i in range(bt):
        qrow = q_ref[i]
        krow = k_ref[i]
        vrow = v_ref[i]
        for h in range(H):
            qc = _col(qrow[:, h * K:(h + 1) * K], eye)
            kc = _col(krow[:, h * K:(h + 1) * K], eye)
            new = gammas[h] * s_ref[i, h] + kc * vrow[:, h * V:(h + 1) * V]
            so_ref[i, h] = new
            o_ref[i, :, h * V:(h + 1) * V] = jnp.sum(qc * new, axis=0, keepdims=True)


def _step_spec(n, bt=1):
    return pl.BlockSpec((bt, 1, n), lambda b: (b, 0, 0))


def _as_steps(x):
    return x.reshape(x.shape[0], 1, x.shape[1])


def _step_batch(B, want):
    return want if B % want == 0 else 1


def _dec_ret(q, k, v, states, H, K, V, j, prev):
    B = q.shape[0]
    gammas = tuple(float(1.0 - 2.0 ** (-5 - h)) for h in range(H))
    bt = _step_batch(B, BT_RET)
    return _call_into_stack(
        functools.partial(_dec_ret_body, H=H, K=K, V=V, gammas=gammas, bt=bt),
        grid=(B // bt,),
        inputs=[_as_steps(q), _as_steps(k), _as_steps(v), states],
        in_specs=[_step_spec(H * K, bt), _step_spec(H * K, bt), _step_spec(H * V, bt),
                  _layer_state_spec((H, K, V), j, bt)],
        out_specs=[_layer_state_spec((H, K, V), j, bt), _step_spec(H * V, bt)],
        out_shape=[jax.ShapeDtypeStruct(states.shape, F32), jax.ShapeDtypeStruct((B, 1, H * V), F32)],
        stack_out=0, prev=prev, name="ret_step")


def _ssd_route(c, DI, xs_ref, b_ref, c_ref):
    w = 512
    if c * w < DI:
        return xs_ref, c * w
    if c * w < DI + b_ref.shape[-1]:
        return b_ref, c * w - DI
    return c_ref, c * w - DI - b_ref.shape[-1]


def _k1_ssd_prompt_body(x_ref, sh_ref, sc_ref, nw_ref, w_ref, wdt_ref, cw_ref, cb_ref, dtb_ref,
                        z_ref, xs_ref, b_ref, c_ref, dt_ref, nconv_ref, cbuf, *, tm, DI, CD, nt):
    t = pl.program_id(1)
    hb = _norm_mod(x_ref, sh_ref, sc_ref, nw_ref)
    w = 512
    for c in range(DI // w):
        z_ref[0, :, c * w:(c + 1) * w] = _dot(hb, w_ref[:, c * w:(c + 1) * w]).astype(z_ref.dtype)
    dt_ref[0] = _softplus(_dot(hb, wdt_ref[...]) + dtb_ref[...])

    @pl.when(t == 0)
    def _():
        cbuf[0:SUBLANES, :] = jnp.zeros((SUBLANES, CD), F32)

    for c in range(CD // w):
        cols = slice(c * w, (c + 1) * w)
        raw = _dot(hb, w_ref[:, DI + c * w:DI + (c + 1) * w])
        cbuf[SUBLANES:SUBLANES + tm, cols] = raw
        conv = cb_ref[:, cols] + cw_ref[3:4, cols] * raw
        for k in range(SSD_CONV - 1):
            conv = conv + cw_ref[k:k + 1, cols] * cbuf[SUBLANES - 3 + k:SUBLANES - 3 + k + tm, cols]
        dst, off = _ssd_route(c, DI, xs_ref, b_ref, c_ref)
        dst[0, :, off:off + w] = _silu(conv).astype(dst.dtype)

    @pl.when(t == nt - 1)
    def _():
        nconv_ref[0] = cbuf[tm + SUBLANES - 3:tm + SUBLANES, :]

    cbuf[0:SUBLANES, :] = cbuf[tm:tm + SUBLANES, :]


def _k1_ssd_prompt(x3, mod3, nw, w, wdt, cw, cb, dtb, tm, DI, GN):
    G, R, d = x3.shape
    CD = DI + 2 * GN
    nt = R // tm
    outs = [jax.ShapeDtypeStruct((G, R, DI), BF16), jax.ShapeDtypeStruct((G, R, DI), BF16),
            jax.ShapeDtypeStruct((G, R, GN), BF16), jax.ShapeDtypeStruct((G, R, GN), BF16),
            jax.ShapeDtypeStruct((G, R, LANES), F32), jax.ShapeDtypeStruct((G, SSD_CONV - 1, CD), F32)]
    return pl.pallas_call(
        functools.partial(_k1_ssd_prompt_body, tm=tm, DI=DI, CD=CD, nt=nt),
        grid=(G, nt),
        in_specs=_k1_specs(x3, mod3, 0, 1, tm) + [_w_spec(w), _const_spec(wdt.shape),
                                                  _const_spec(cw.shape), _const_spec(cb.shape),
                                                  _const_spec(dtb.shape)],
        out_specs=[_row_spec(tm, DI), _row_spec(tm, DI), _row_spec(tm, GN), _row_spec(tm, GN),
                   _row_spec(tm, LANES), pl.BlockSpec((1, SSD_CONV - 1, CD), lambda g, t: (g, 0, 0))],
        out_shape=outs,
        scratch_shapes=[pltpu.VMEM((tm + SUBLANES, CD), F32)],
        compiler_params=_params(("arbitrary", "arbitrary")),
        name="ssd_inproj",
    )(x3, mod3, mod3, nw, w[0], wdt, cw, cb, dtb)


def _k1_ssd_sample_body(x_ref, sh_ref, sc_ref, nw_ref, w_ref, wdt_ref, cw_ref, cb_ref, dtb_ref, buf_ref,
                        alog_ref, exp_ref, z_ref, xs_ref, xdt_ref, b_ref, c_ref, dec_ref, nconv_ref,
                        *, DI, CD):
    hb = _norm_mod(x_ref, sh_ref, sc_ref, nw_ref)
    w = 512
    for c in range(DI // w):
        z_ref[0, :, c * w:(c + 1) * w] = _dot(hb, w_ref[:, c * w:(c + 1) * w])
    dt = _softplus(_dot(hb, wdt_ref[...]) + dtb_ref[...])
    dec_ref[0] = jnp.exp(dt * (-jnp.exp(alog_ref[...])))
    dt_rep = _dot_sel(_split3(dt), exp_ref[...])
    for c in range(CD // w):
        cols = slice(c * w, (c + 1) * w)
        raw = _dot(hb, w_ref[:, DI + c * w:DI + (c + 1) * w])
        conv = cb_ref[:, cols] + cw_ref[3:4, cols] * raw
        for k in range(SSD_CONV - 1):
            conv = conv + cw_ref[k:k + 1, cols] * buf_ref[k, :, cols]
        act = _silu(conv)
        dst, off = _ssd_route(c, DI, xs_ref, b_ref, c_ref)
        dst[0, :, off:off + w] = act
        if dst is xs_ref:
            xdt_ref[0, :, cols] = act * dt_rep[:, cols]
        nconv_ref[0, :, cols] = buf_ref[1, :, cols]
        nconv_ref[1, :, cols] = buf_ref[2, :, cols]
        nconv_ref[2, :, cols] = raw


def _k1_ssd_sample(x3, mod3, nw, w, wdt, cw, cb, dtb, buf_t, alog_row, expand, DI, GN):
    G, R, d = x3.shape
    CD = DI + 2 * GN
    tm = R
    outs = [jax.ShapeDtypeStruct((G, R, DI), F32), jax.ShapeDtypeStruct((G, R, DI), F32),
            jax.ShapeDtypeStruct((G, R, DI), F32),
            jax.ShapeDtypeStruct((G, R, GN), F32), jax.ShapeDtypeStruct((G, R, GN), F32),
            jax.ShapeDtypeStruct((G, R, LANES), F32), jax.ShapeDtypeStruct((SSD_CONV - 1, R, CD), F32)]
    return pl.pallas_call(
        functools.partial(_k1_ssd_sample_body, DI=DI, CD=CD),
        grid=(G, 1),
        in_specs=_k1_specs(x3, mod3, 0, 1, tm) + [_w_spec(w), _const_spec(wdt.shape),
                                                  _const_spec(cw.shape), _const_spec(cb.shape),
                                                  _const_spec(dtb.shape), _const_spec(buf_t.shape),
                                                  _const_spec(alog_row.shape), _const_spec(expand.shape)],
        out_specs=[_row_spec(tm, DI), _row_spec(tm, DI), _row_spec(tm, DI), _row_spec(tm, GN),
                   _row_spec(tm, GN), _row_spec(tm, LANES),
                   pl.BlockSpec((SSD_CONV - 1, R, CD), lambda g, t: (0, 0, 0))],
        out_shape=outs,
        compiler_params=_params(("arbitrary", "arbitrary")),
        name="ssd_inproj_step",
    )(x3, mod3, mod3, nw, w[0], wdt, cw, cb, dtb, buf_t, alog_row, expand)


def _k2_ssd_body(z_ref, xs_ref, b_ref, c_ref, dt_ref, tri_ref, alog_ref, exp_ref, drep_ref, nw_ref,
                 y_ref, so_ref, s_scr, *, C, NH, P, N, NG, nc):
    ci = pl.program_id(1)

    @pl.when(ci == 0)
    def _():
        s_scr[...] = jnp.zeros_like(s_scr)

    hpg = NH // NG
    gw = hpg * P
    dt = dt_ref[0]
    la = dt * (-jnp.exp(alog_ref[...]))
    cum = _sel_dot(tri_ref[...], _split3(la))
    tot = cum[C - 1:C, :]
    c2 = cum * LOG2E
    r2_t = (c2 - jnp.log(dt) * LOG2E).T
    w_state_parts = _split3(jnp.exp(tot - cum) * dt)
    w_in_parts = _split3(jnp.exp(cum))
    row = lax.broadcasted_iota(jnp.int32, (C, C), 0)
    colm = lax.broadcasted_iota(jnp.int32, (C, C), 1)
    causal = row >= colm
    lane = lax.broadcasted_iota(jnp.int32, (1, 2 * P), 1)
    head_keep = ((lane < P).astype(BF16), (lane >= P).astype(BF16))
    for g in range(NG):
        gsl = slice(g * gw, (g + 1) * gw)
        bg = b_ref[0, :, g * N:(g + 1) * N]
        cg = c_ref[0, :, g * N:(g + 1) * N]
        expand = exp_ref[:, gsl]
        w_state = _dot_sel(w_state_parts, expand)
        w_in = _dot_sel(w_in_parts, expand)
        xs = xs_ref[0, :, gsl]
        xsf = xs.astype(F32)
        xp = (xsf * w_state).astype(BF16)
        gmat = _dot_nt(cg, bg)
        state = s_scr[g]
        y_inter = _dot(cg, state.astype(BF16)) * w_in
        s_scr[g] = state * w_in[C - 1:C, :] + _dot_tn(bg, xp)
        ys = []
        for hp in range(hpg // 2):
            h0 = g * hpg + 2 * hp
            xpair = xs[:, 2 * hp * P:(2 * hp + 2) * P]
            acc = y_inter[:, 2 * hp * P:(2 * hp + 2) * P]
            for e in range(2):
                h = h0 + e
                seg2 = jnp.where(causal, c2[:, h:h + 1] - r2_t[h:h + 1, :], MASKED_EXPONENT)
                m = (gmat * jnp.exp2(seg2)).astype(BF16)
                acc = acc + _dot(m, xpair * head_keep[e])
            ys.append(acc)
        out = _ssd_gate(jnp.concatenate(ys, axis=-1), xsf, z_ref[0, :, gsl].astype(F32), drep_ref[:, gsl],
                        nw_ref[:, gsl], 1)[0]
        y_ref[0, :, gsl] = out.astype(y_ref.dtype)

    @pl.when(ci == nc - 1)
    def _():
        for g in range(NG):
            so_ref[g] = s_scr[g].T


def _ssd_expand(NH, P):
    e = np.zeros((LANES, NH * P), np.float32)
    for h in range(NH):
        e[h, h * P:(h + 1) * P] = 1.0
    return jnp.asarray(e, BF16)


def _k2_ssd(z, xs, bm, cm, dt, alog_row, expand, drep, nw, NH, P, N, NG, nl, j, prev):
    G, R, DI = xs.shape
    C = min(C_SSD, R)
    nc = R // C
    i = np.arange(C)
    tri = jnp.asarray(i[:, None] >= i[None, :], BF16)
    gw = (NH // NG) * P
    return _call_into_stack(
        functools.partial(_k2_ssd_body, C=C, NH=NH, P=P, N=N, NG=NG, nc=nc),
        grid=(G, nc),
        inputs=[z, xs, bm, cm, dt, tri, alog_row, expand, drep, nw],
        in_specs=[_row_spec(C, DI), _row_spec(C, DI), _row_spec(C, NG * N), _row_spec(C, NG * N),
                  _row_spec(C, LANES), _const_spec(tri.shape), _const_spec(alog_row.shape),
                  _const_spec(expand.shape), _const_spec(drep.shape), _const_spec(nw.shape)],
        out_specs=[_row_spec(C, DI), _layer_state_spec((NG, gw, N), j)],
        out_shape=[jax.ShapeDtypeStruct((G, R, DI), BF16), jax.ShapeDtypeStruct((nl, G, NG, gw, N), F32)],
        scratch_shapes=[pltpu.VMEM((NG, N, gw), F32)],
        stack_out=1, prev=prev, name="ssd_core")


def _pad_rows(row, keep_first_only):
    x = jnp.broadcast_to(row, (MXU_PAD_ROWS, row.shape[-1]))
    if keep_first_only:
        first = lax.broadcasted_iota(jnp.int32, (MXU_PAD_ROWS, 1), 0) == 0
        x = jnp.where(first, x, 0.0)
    return x.astype(BF16)


def _dec_ssd_body(xdt_ref, b_ref, c_ref, dec_ref, s_ref, so_ref, o_ref, *, NH, P, N, NG, bt):
    hpg = NH // NG
    gw = hpg * P
    for i in range(bt):
        arow = dec_ref[i]
        v_pad = _pad_rows(xdt_ref[i], True)
        k_pad = _pad_rows(b_ref[i], False)
        q_pad = _pad_rows(c_ref[i], False)
        for g in range(NG):
            upd = _dot_tn(v_pad[:, g * gw:(g + 1) * gw], k_pad[:, g * N:(g + 1) * N])
            news = []
            for hl in range(hpg):
                h = g * hpg + hl
                rows = slice(hl * P, (hl + 1) * P)
                new = arow[:, h:h + 1] * s_ref[i, g, rows, :] + upd[rows, :]
                so_ref[i, g, rows, :] = new
                news.append(new.astype(BF16))
            o = _dot_nt(q_pad[:, g * N:(g + 1) * N], jnp.concatenate(news, axis=0))
            o_ref[i, :, g * gw:(g + 1) * gw] = o[0:1]


def _dec_ssd(xdt, bm, cm, dec, states_t, NH, P, N, NG, j, prev):
    B = xdt.shape[0]
    tail = states_t.shape[2:]
    bt = _step_batch(B, BT_SSD)
    return _call_into_stack(
        functools.partial(_dec_ssd_body, NH=NH, P=P, N=N, NG=NG, bt=bt),
        grid=(B // bt,),
        inputs=[_as_steps(xdt), _as_steps(bm), _as_steps(cm), _as_steps(dec), states_t],
        in_specs=[_step_spec(NH * P, bt), _step_spec(NG * N, bt), _step_spec(NG * N, bt),
                  _step_spec(LANES, bt), _layer_state_spec(tail, j, bt)],
        out_specs=[_layer_state_spec(tail, j, bt), _step_spec(NH * P, bt)],
        out_shape=[jax.ShapeDtypeStruct(states_t.shape, F32), jax.ShapeDtypeStruct((B, 1, NH * P), F32)],
        stack_out=0, prev=prev, name="ssd_step")


def _hg_lower_bound(lbl_ref, layer):
    logits = lbl_ref[...]
    e = jnp.exp(logits - jnp.max(logits, axis=0, keepdims=True))
    sm = e / jnp.sum(e, axis=0, keepdims=True)
    lb = jnp.zeros_like(sm[0:1])
    for i in range(1, layer + 1):
        lb = lb + sm[i:i + 1]
    return lb


def _k1_hg_body(x_ref, sh_ref, sc_ref, nw_ref, w_ref, lbl_ref, *out_refs, layer, KD, VD, prompt):
    hb = _norm_mod(x_ref, sh_ref, sc_ref, nw_ref)
    lb = _hg_lower_bound(lbl_ref, layer)
    w = 512
    if prompt:
        q_ref, la_ref, v_ref, g_ref = out_refs
    else:
        q_ref, k_ref, a_ref, v_ref, g_ref = out_refs
    for c in range(KD // w):
        cols = slice(c * w, (c + 1) * w)
        q = _silu(_dot(hb, w_ref[:, c * w:(c + 1) * w])) * (HG_KEY_DIM ** -0.5)
        q_ref[0, :, cols] = q.astype(q_ref.dtype)
        lbc = lb[:, cols]
        f = lbc + (1.0 - lbc) * jax.nn.sigmoid(_dot(hb, w_ref[:, KD + c * w:KD + (c + 1) * w]))
        if prompt:
            la_ref[0, :, cols] = jnp.log(f)
        else:
            k_ref[0, :, cols] = 1.0 - f
            a_ref[0, :, cols] = f
    for c in range(VD // w):
        cols = slice(c * w, (c + 1) * w)
        v_ref[0, :, cols] = _dot(hb, w_ref[:, 2 * KD + c * w:2 * KD + (c + 1) * w]).astype(v_ref.dtype)
        g_ref[0, :, cols] = _dot(hb, w_ref[:, 2 * KD + VD + c * w:2 * KD + VD + (c + 1) * w]).astype(g_ref.dtype)


def _k1_hg(x3, mod3, nw, w, lbl, tm, layer, KD, VD, prompt):
    G, R, d = x3.shape
    if prompt:
        dts = [(KD, BF16), (KD, F32), (VD, BF16), (VD, BF16)]
    else:
        dts = [(KD, F32), (KD, F32), (KD, F32), (VD, F32), (VD, F32)]
    return pl.pallas_call(
        functools.partial(_k1_hg_body, layer=layer, KD=KD, VD=VD, prompt=prompt),
        grid=(G, R // tm),
        in_specs=_k1_specs(x3, mod3, 0, 1, tm) + [_w_spec(w), _const_spec(lbl.shape)],
        out_specs=[_row_spec(tm, n) for n, _ in dts],
        out_shape=[jax.ShapeDtypeStruct((G, R, n), dt) for n, dt in dts],
        compiler_params=_params(("arbitrary", "arbitrary")),
        name="hg_inproj" if prompt else "hg_inproj_step",
    )(x3, mod3, mod3, nw, w[0], lbl)


def _hg_masks(C):
    i = np.arange(C)
    r, t = i[:, None], i[None, :]
    sels = [t <= r, (t <= r) & (t >= HG_SUB * (r // HG_SUB))]
    pairs = []
    s = C // 2
    while s >= HG_SUB:
        mid = 2 * s * (r // (2 * s)) + s - 1
        sels.append(np.where(r > mid, (t > mid) & (t <= r), (t > r) & (t <= mid)))
        pairs.append((r // (2 * s) == t // (2 * s)) & (r % (2 * s) >= s) & (t % (2 * s) < s))
        s //= 2
    pairs.append((r // HG_SUB == t // HG_SUB) & (t <= r))
    return (jnp.asarray(np.stack(sels), BF16), jnp.asarray(np.stack(pairs), F32))


def _hg_diag_collect(C, K):
    e = np.zeros((HG_SUB, K, C), np.float32)
    for j in range(HG_SUB):
        e[j, :, j::HG_SUB] = 1.0
    return jnp.asarray(e.reshape(HG_SUB * K, C), BF16)


def _k2_hg_body(q_ref, la_ref, v_ref, g_ref, sel_ref, pair_ref, coll_ref, nw_ref, y_ref, so_ref,
                st_scr, k_scr, c_scr, p_scr, *, C, H, K, V, nlev, nc):
    ci = pl.program_id(1)

    @pl.when(ci == 0)
    def _():
        st_scr[...] = jnp.zeros_like(st_scr)

    D = H * K
    la2 = la_ref[0] * LOG2E
    parts = _split3(la2)
    cum2 = _sel_dot(sel_ref[0], parts)
    cw2 = _sel_dot(sel_ref[1], parts)
    kf = 1.0 - jnp.exp2(la2)
    qf = q_ref[0].astype(F32)
    vb = v_ref[0]
    tot2 = cum2[C - 1:C, :]
    q_in = (qf * jnp.exp2(cum2)).astype(BF16)
    k_out = (kf * jnp.exp2(tot2 - cum2)).astype(BF16)
    e_tot = jnp.exp2(tot2)

    scores = [jnp.zeros((C, C), F32) for _ in range(H)]
    for lev in range(nlev):
        a = jnp.exp2(_sel_dot(sel_ref[2 + lev], parts))
        ql = (qf * a).astype(BF16)
        kl = (kf * a).astype(BF16)
        pm = pair_ref[lev] > 0.5
        for h in range(H):
            sl = slice(h * K, (h + 1) * K)
            scores[h] = scores[h] + jnp.where(pm, _dot_nt(ql[:, sl], kl[:, sl]), 0.0)

    k_scr[...] = kf
    c_scr[...] = cw2
    rowmod = lax.broadcasted_iota(jnp.int32, (C, D), 0) & (HG_SUB - 1)
    nblk = C // HG_SUB
    for j in range(HG_SUB):
        kb = jnp.concatenate([jnp.broadcast_to(k_scr[b * HG_SUB + j:b * HG_SUB + j + 1, :], (HG_SUB, D))
                              for b in range(nblk)], axis=0)
        cb = jnp.concatenate([jnp.broadcast_to(c_scr[b * HG_SUB + j:b * HG_SUB + j + 1, :], (HG_SUB, D))
                              for b in range(nblk)], axis=0)
        dec = jnp.exp2(jnp.where(rowmod >= j, cw2 - cb, MASKED_EXPONENT))
        p = (qf * kb * dec).astype(BF16)
        for h in range(H):
            p_scr[h * C:(h + 1) * C, j * K:(j + 1) * K] = p[:, h * K:(h + 1) * K]
    diag = _dot(p_scr[...], coll_ref[...])
    dm = pair_ref[nlev] > 0.5

    nw = nw_ref[...]
    for h in range(H):
        sl = slice(h * K, (h + 1) * K)
        vsl = slice(h * V, (h + 1) * V)
        st = st_scr[h]
        sc = scores[h] + jnp.where(dm, diag[h * C:(h + 1) * C, :], 0.0)
        o = _dot(sc.astype(BF16), vb[:, vsl]) + _dot_nt(q_in[:, sl], st.astype(BF16))
        st_scr[h] = st * e_tot[:, sl] + _dot_tn(vb[:, vsl], k_out[:, sl])
        gh = g_ref[0, :, vsl].astype(F32)
        y_ref[0, :, vsl] = (_rms(o) * nw[:, vsl] * _silu(gh)).astype(y_ref.dtype)

    @pl.when(ci == nc - 1)
    def _():
        for h in range(H):
            so_ref[h] = st_scr[h].T


def _k2_hg(q, la, v, g, nw_rep, H, K, V, nl, j, prev):
    G, R, D = q.shape
    C = min(C_HG, R)
    nc = R // C
    sels, pairs = _hg_masks(C)
    nlev = pairs.shape[0] - 1
    coll = _hg_diag_collect(C, K)
    return _call_into_stack(
        functools.partial(_k2_hg_body, C=C, H=H, K=K, V=V, nlev=nlev, nc=nc),
        grid=(G, nc),
        inputs=[q, la, v, g, sels, pairs, coll, nw_rep],
        in_specs=[_row_spec(C, D), _row_spec(C, D), _row_spec(C, H * V), _row_spec(C, H * V),
                  _const_spec(sels.shape), _const_spec(pairs.shape), _const_spec(coll.shape),
                  _const_spec(nw_rep.shape)],
        out_specs=[_row_spec(C, H * V), _layer_state_spec((H, K, V), j)],
        out_shape=[jax.ShapeDtypeStruct((G, R, H * V), BF16), jax.ShapeDtypeStruct((nl, G, H, K, V), F32)],
        scratch_shapes=[pltpu.VMEM((H, V, K), F32), pltpu.VMEM((C, D), F32), pltpu.VMEM((C, D), F32),
                        pltpu.VMEM((H * C, HG_SUB * K), BF16)],
        stack_out=1, prev=prev, name="hg_core")


def _dec_hg_body(q_ref, k_ref, a_ref, v_ref, s_ref, so_ref, o_ref, *, H, K, V, bt):
    eye = lax.broadcasted_iota(jnp.int32, (K, K), 0) == lax.broadcasted_iota(jnp.int32, (K, K), 1)
    for i in range(bt):
        arow = a_ref[i]
        q_pad = _pad_rows(q_ref[i], False)
        k_pad = _pad_rows(k_ref[i], True)
        v_pad = _pad_rows(v_ref[i], False)
        for h in range(H):
            sl = slice(h * K, (h + 1) * K)
            vsl = slice(h * V, (h + 1) * V)
            ac = _col(arow[:, sl], eye)
            new = ac * s_ref[i, h] + _dot_tn(k_pad[:, sl], v_pad[:, vsl])
            so_ref[i, h] = new
            o_ref[i, :, vsl] = _dot(q_pad[:, sl], new.astype(BF16))[0:1]


def _dec_hg(q, k, a, v, states, H, K, V, j, prev):
    B = q.shape[0]
    bt = _step_batch(B, BT_HG)
    return _call_into_stack(
        functools.partial(_dec_hg_body, H=H, K=K, V=V, bt=bt),
        grid=(B // bt,),
        inputs=[_as_steps(q), _as_steps(k), _as_steps(a), _as_steps(v), states],
        in_specs=[_step_spec(H * K, bt), _step_spec(H * K, bt), _step_spec(H * K, bt), _step_spec(H * V, bt),
                  _layer_state_spec((H, K, V), j, bt)],
        out_specs=[_layer_state_spec((H, K, V), j, bt), _step_spec(H * V, bt)],
        out_shape=[jax.ShapeDtypeStruct(states.shape, F32), jax.ShapeDtypeStruct((B, 1, H * V), F32)],
        stack_out=0, prev=prev, name="hg_step")


def _f32_tile(ref):
    return ref[0].astype(F32)


def _pro_identity(y_ref):
    return y_ref[0]


def _pro_ret(o_ref, g_ref, *, H, V):
    return jnp.concatenate(_ret_gate(_f32_tile(o_ref), _f32_tile(g_ref), H, V), axis=-1).astype(BF16)


def _pro_ssd(o_ref, xs_ref, z_ref, drep_ref, nw_ref, *, NG):
    return jnp.concatenate(_ssd_gate(_f32_tile(o_ref), _f32_tile(xs_ref), _f32_tile(z_ref), drep_ref[...],
                                     nw_ref[...], NG), axis=-1).astype(BF16)


def _pro_hg(o_ref, g_ref, nw_ref, *, H, V):
    return jnp.concatenate(_hg_gate(_f32_tile(o_ref), _f32_tile(g_ref), nw_ref[...], H, V),
                           axis=-1).astype(BF16)


def _swiglu(hb, win_ref, wout_ref, F, bounds):
    acc = None
    for lo, hi in bounds:
        gt = _dot(hb, win_ref[:, lo:hi])
        up = _dot(hb, win_ref[:, F + lo:F + hi])
        a = (_silu(gt) * up).astype(BF16)
        part = _dot(a, wout_ref[lo:hi, :])
        acc = part if acc is None else acc + part
    return acc


def _k34_body(*refs, prologue, n_tile, n_const, F, bounds):
    n_pro = n_tile + n_const
    (wo_ref, x_ref, gm_ref, nwm_ref, sh_ref, sc_ref, nwf_ref, gf_ref, nwp_ref,
     win_ref, wout_ref, o_ref) = refs[n_pro:]
    y = prologue(*refs[:n_pro])
    x = x_ref[0] + gm_ref[0] * (_rms(_dot(y, wo_ref[...])) * nwm_ref[...])
    hb = (_rms(x) * nwf_ref[...] * (1.0 + sc_ref[0]) + sh_ref[0]).astype(BF16)
    acc = _swiglu(hb, win_ref, wout_ref, F, bounds)
    o_ref[0] = x + gf_ref[0] * (_rms(acc) * nwp_ref[...])


def _k34(prologue, tile_args, const_args, w_out, x3, mod3, nw_mix_post, nw_pre, nw_post, win, wout, tm):
    G, R, d = x3.shape
    rm = mod3.shape[1]
    F = wout[0].shape[1]
    mod_spec = lambda idx: pl.BlockSpec((1, rm, d), lambda g, t: (g, 0, idx))
    row = _const_spec((1, d))
    return pl.pallas_call(
        functools.partial(_k34_body, prologue=prologue, n_tile=len(tile_args), n_const=len(const_args),
                          F=F, bounds=_ffn_bounds(F)),
        grid=(G, R // tm),
        in_specs=[_row_spec(tm, a.shape[-1]) for a in tile_args] + [_const_spec(c.shape) for c in const_args]
        + [_w_spec(w_out), _row_spec(tm, d), mod_spec(2), row, mod_spec(3), mod_spec(4), row,
           mod_spec(5), row, _w_spec(win), _w_spec(wout)],
        out_specs=_row_spec(tm, d),
        out_shape=jax.ShapeDtypeStruct((G, R, d), F32),
        compiler_params=_params(("arbitrary", "arbitrary")),
        name="outproj_ffn",
    )(*tile_args, *const_args, w_out[0], x3, mod3, nw_mix_post, mod3, mod3, nw_pre, mod3, nw_post, win[0],
      wout[0])


def _ffn_bounds(F):
    tiles, rem = divmod(F, MXU_TILE)
    assert rem == 0, F
    n = -(-tiles // FFN_MAX_CHUNK_TILES)
    sizes = [tiles // n + (1 if i < tiles % n else 0) for i in range(n)]
    edges = np.cumsum([0] + sizes) * MXU_TILE
    return tuple((int(a), int(b)) for a, b in zip(edges[:-1], edges[1:]))


def _rope_table(pos, half):
    inv = 1.0 / (ROPE_BASE ** (jnp.arange(half, dtype=F32) / half))
    ang = pos.astype(F32)[:, None] * inv[None, :]
    return jnp.cos(ang), jnp.sin(ang)


def kernel(x_prompt, x_sample, c_prompt, c_sample, state_ret, state_ssd, state_conv, state_hgrn, w_ada, b_ada, norm_mix_pre, norm_mix_post, norm_ffn_pre, norm_ffn_post, ret_w_in, ret_w_out, ssd_w_in, ssd_conv_w, ssd_conv_b, ssd_dt_bias, ssd_a_log, ssd_d, ssd_norm, ssd_w_out, hg_w_in, hg_lb_logits, hg_norm, hg_w_out, ffn_w_in, ffn_w_out):
    bp, seq, d = x_prompt.shape
    bs = x_sample.shape[0]
    depth = w_ada.shape[0]
    ret_h = d // 256
    ret_k = d // ret_h
    ret_v = 2 * ret_k
    di = 2 * d
    ssd_nh = di // SSD_HEAD_DIM
    gn = SSD_GROUPS * SSD_STATE
    hg_h = d // HG_KEY_DIM
    hg_v = d // hg_h
    kd = hg_h * HG_KEY_DIM
    vd = hg_h * hg_v
    tm = min(TM_PROMPT, seq)

    mod = _adaln(jnp.concatenate([c_prompt, c_sample], axis=0), w_ada, b_ada)
    xp = x_prompt.astype(F32)
    xs = x_sample.astype(F32).reshape(1, bs, d)
    cos_p, sin_p = _rope_table(jnp.arange(seq, dtype=jnp.int32), ret_k // 2)
    cos_s, sin_s = _rope_table(jnp.full((1,), PAST_LEN, jnp.int32), ret_k // 2)

    hpg = ssd_nh // SSD_GROUPS
    ssd_states_t = jnp.swapaxes(state_ssd.astype(F32), 3, 4).reshape(
        state_ssd.shape[0], bs, SSD_GROUPS, hpg * SSD_HEAD_DIM, SSD_STATE)

    def ssd_untranspose(s):
        s = s.reshape(s.shape[0], s.shape[1], ssd_nh, SSD_HEAD_DIM, SSD_STATE)
        return jnp.swapaxes(s, 3, 4).astype(state_ssd.dtype)

    wb = {name: w.astype(BF16) for name, w in (
        ("ret_in", ret_w_in), ("ret_out", ret_w_out), ("ssd_in", ssd_w_in), ("ssd_out", ssd_w_out),
        ("hg_in", hg_w_in), ("hg_out", hg_w_out), ("ffn_in", ffn_w_in), ("ffn_out", ffn_w_out))}

    new = {k: [] for k in ("conv_p", "conv_s")}
    stk = {k: None for k in ("ret_p", "ret_s", "ssd_p", "ssd_s", "hg_p", "hg_s")}
    n_kind = [sum(1 for l in range(depth) if l % 3 == kind) for kind in range(3)]
    counts = [0, 0, 0]
    for layer in range(depth):
        mod_p = mod[layer, :bp].reshape(bp, 1, 6 * d)
        mod_s = mod[layer, bp:].reshape(1, bs, 6 * d)
        nw_pre = norm_mix_pre[layer].reshape(1, d)
        nw_post = norm_mix_post[layer].reshape(1, d)
        kind = layer % 3
        j = counts[kind]
        counts[kind] += 1
        if kind == 0:
            w_in = (wb["ret_in"], j)
            w_out = (wb["ret_out"], j)
            prologue = functools.partial(_pro_ret, H=ret_h, V=ret_v)
            q, k, v, g = _k1_ret(xp, mod_p, nw_pre, w_in, cos_p, sin_p, tm, BF16, ret_h, ret_k, ret_v)
            y, stk["ret_p"] = _k2_ret(q, k, v, g, ret_h, ret_k, ret_v, n_kind[0], j, stk["ret_p"])
            q, k, v, g = _k1_ret(xs, mod_s, nw_pre, w_in, cos_s, sin_s, bs, F32, ret_h, ret_k, ret_v)
            stk["ret_s"], o = _dec_ret(q[0], k[0], v[0], state_ret, ret_h, ret_k, ret_v, j, stk["ret_s"])
            pro_s = ([o.reshape(1, bs, -1), g], [])
        elif kind == 1:
            w_main = (wb["ssd_in"], j)
            w_dt = jnp.pad(ssd_w_in[j][:, di + di + 2 * gn:], ((0, 0), (0, LANES - ssd_nh))).astype(BF16)
            w_out = (wb["ssd_out"], j)
            cw = ssd_conv_w[j]
            cb = ssd_conv_b[j].reshape(1, -1)
            dtb = jnp.pad(ssd_dt_bias[j], (0, LANES - ssd_nh)).reshape(1, LANES)
            alog = jnp.pad(ssd_a_log[j].astype(F32), (0, LANES - ssd_nh)).reshape(1, LANES)
            drep = jnp.repeat(ssd_d[j], SSD_HEAD_DIM).reshape(1, di)
            nw_ssd = ssd_norm[j].reshape(1, di)
            z, xc, bm, cm, dt, nconv = _k1_ssd_prompt(xp, mod_p, nw_pre, w_main, w_dt, cw, cb, dtb, tm, di, gn)
            new["conv_p"].append(nconv)
            expand = _ssd_expand(ssd_nh, SSD_HEAD_DIM)
            prologue = functools.partial(_pro_ssd, NG=SSD_GROUPS)
            y, stk["ssd_p"] = _k2_ssd(z, xc, bm, cm, dt, alog, expand, drep, nw_ssd, ssd_nh, SSD_HEAD_DIM,
                                      SSD_STATE, SSD_GROUPS, n_kind[1], j, stk["ssd_p"])
            buf_t = jnp.transpose(state_conv[j].astype(F32), (1, 0, 2))
            z, xc, xdt, bm, cm, dec, nconv = _k1_ssd_sample(xs, mod_s, nw_pre, w_main, w_dt, cw, cb, dtb, buf_t,
                                                            alog, expand, di, gn)
            new["conv_s"].append(jnp.transpose(nconv, (1, 0, 2)))
            stk["ssd_s"], o = _dec_ssd(xdt[0], bm[0], cm[0], dec[0], ssd_states_t,
                                       ssd_nh, SSD_HEAD_DIM, SSD_STATE, SSD_GROUPS, j, stk["ssd_s"])
            pro_s = ([o.reshape(1, bs, -1), xc, z], [drep, nw_ssd])
        else:
            w_in = (wb["hg_in"], j)
            w_out = (wb["hg_out"], j)
            nw_hg = jnp.tile(hg_norm[j], hg_h).reshape(1, vd)
            lbl = hg_lb_logits.astype(F32)
            prologue = functools.partial(_pro_hg, H=hg_h, V=hg_v)
            q, la, v, g = _k1_hg(xp, mod_p, nw_pre, w_in, lbl, tm, layer, kd, vd, True)
            y, stk["hg_p"] = _k2_hg(q, la, v, g, nw_hg, hg_h, HG_KEY_DIM, hg_v, n_kind[2], j, stk["hg_p"])
            q, k, a, v, g = _k1_hg(xs, mod_s, nw_pre, w_in, lbl, bs, layer, kd, vd, False)
            stk["hg_s"], o = _dec_hg(q[0], k[0], a[0], v[0], state_hgrn, hg_h, HG_KEY_DIM, hg_v, j, stk["hg_s"])
            pro_s = ([o.reshape(1, bs, -1), g], [nw_hg])
        nf_pre = norm_ffn_pre[layer].reshape(1, d)
        nf_post = norm_ffn_post[layer].reshape(1, d)
        f_in = (wb["ffn_in"], layer)
        f_out = (wb["ffn_out"], layer)
        xp = _k34(_pro_identity, [y], [], w_out, xp, mod_p, nw_post, nf_pre, nf_post, f_in, f_out, tm)
        xs = _k34(prologue, *pro_s, w_out, xs, mod_s, nw_post, nf_pre, nf_post, f_in, f_out, bs)

    stack = lambda name, like: jnp.stack(new[name]).astype(like.dtype)
    return (xp.astype(x_prompt.dtype), xs.reshape(bs, 1, d).astype(x_sample.dtype),
            stk["ret_p"].astype(state_ret.dtype), stk["ret_s"].astype(state_ret.dtype),
            ssd_untranspose(stk["ssd_p"]), ssd_untranspose(stk["ssd_s"]),
            stack("conv_p", state_conv), stack("conv_s", state_conv),
            stk["hg_p"].astype(state_hgrn.dtype), stk["hg_s"].astype(state_hgrn.dtype))
```

```python
import functools

import numpy as np
import jax
import jax.numpy as jnp
from jax import lax
from jax.experimental import pallas as pl
from jax.experimental.pallas import tpu as pltpu

F32 = jnp.float32
BF16 = jnp.bfloat16
EPS = 1e-6
PAST_LEN = 16384
ROPE_BASE = 10000.0
SSD_HEAD_DIM = 64
SSD_GROUPS = 4
SSD_STATE = 128
SSD_CONV = 4
HG_KEY_DIM = 128
LANES = 128
SUBLANES = 8
VMEM_LIMIT = 56 * 1024 * 1024
TM_PROMPT = 512
MXU_TILE = 256
MXU_PAD_ROWS = 16
C_RET = 256
C_SSD = 128
C_HG = 128
HG_SUB = 8
SSD_CHUNKS_PER_STEP = 4
HG_CHUNKS_PER_STEP = 1
RET_CHUNKS_PER_STEP = 2
FFN_MAX_CHUNK_TILES = 6
LOG2E = 1.4426950408889634
MASKED_EXPONENT = -1e30
BT_RET = 2
BT_SSD = 2
BT_HG = 4


def _rms(x):
    return x * lax.rsqrt(jnp.mean(x * x, axis=-1, keepdims=True) + EPS)


def _silu(x):
    h = 0.5 * x
    return h + h * jnp.tanh(h)


def _softplus(x):
    return jnp.maximum(x, 0.0) + jnp.log1p(jnp.exp(-jnp.abs(x)))


def _dot(a, b):
    return jnp.dot(a, b, preferred_element_type=F32)


def _dot_nt(a, b):
    return lax.dot_general(a, b, (((1,), (1,)), ((), ())), preferred_element_type=F32)


def _dot_tn(a, b):
    return lax.dot_general(a, b, (((0,), (0,)), ((), ())), preferred_element_type=F32)


def _split3(x):
    hi = x.astype(BF16)
    r = x - hi.astype(F32)
    mid = r.astype(BF16)
    lo = (r - mid.astype(F32)).astype(BF16)
    return hi, mid, lo


def _sel_dot(sel, parts):
    acc = _dot(sel, parts[0])
    for p in parts[1:]:
        acc = acc + _dot(sel, p)
    return acc


def _dot_sel(parts, sel):
    acc = _dot(parts[0], sel)
    for p in parts[1:]:
        acc = acc + _dot(p, sel)
    return acc


def _col(row, eye):
    return jnp.sum(jnp.where(eye, row, 0.0), axis=1, keepdims=True)


def _norm_mod(x_ref, sh_ref, sc_ref, nw_ref):
    x = x_ref[0]
    h = _rms(x) * nw_ref[...] * (1.0 + sc_ref[0]) + sh_ref[0]
    return h.astype(BF16)


def _params(sem):
    return pltpu.CompilerParams(dimension_semantics=sem, vmem_limit_bytes=VMEM_LIMIT)


def _const_spec(shape):
    nd = len(shape)
    return pl.BlockSpec(shape, lambda *_: (0,) * nd, pipeline_mode=pl.Buffered(1))


def _w_spec(wl):
    stack, j = wl
    return pl.BlockSpec((None,) + stack.shape[1:], lambda *_: (j, 0, 0), pipeline_mode=pl.Buffered(1))


def _ret_gate(o, g, H, V):
    outs = []
    for h in range(H):
        outs.append(_rms(o[:, h * V:(h + 1) * V]) * _silu(g[:, h * V:(h + 1) * V]))
    return outs


def _ssd_gate(y, xs, z, d_rep, nw, groups):
    y = (y + d_rep * xs) * _silu(z)
    w = y.shape[-1] // groups
    outs = []
    for g in range(groups):
        outs.append(_rms(y[:, g * w:(g + 1) * w]) * nw[:, g * w:(g + 1) * w])
    return outs


def _hg_gate(o, g, nw, H, V):
    outs = []
    for h in range(H):
        sl = slice(h * V, (h + 1) * V)
        outs.append(_rms(o[:, sl]) * nw[:, sl] * _silu(g[:, sl]))
    return outs


def _adaln_body(c_ref, w_ref, b_ref, o_ref):
    a = _silu(c_ref[...]).astype(BF16)
    o_ref[0] = _dot(a, w_ref[0].astype(BF16)) + b_ref[0]


def _adaln(c_all, w_ada, b_ada):
    depth, d, n = w_ada.shape
    bc = c_all.shape[0]
    tn = 1024
    return pl.pallas_call(
        _adaln_body,
        grid=(depth, n // tn),
        in_specs=[
            pl.BlockSpec((bc, d), lambda l, j: (0, 0)),
            pl.BlockSpec((1, d, tn), lambda l, j: (l, 0, j)),
            pl.BlockSpec((1, 1, tn), lambda l, j: (l, 0, j)),
        ],
        out_specs=pl.BlockSpec((1, bc, tn), lambda l, j: (l, 0, j)),
        out_shape=jax.ShapeDtypeStruct((depth, bc, n), F32),
        compiler_params=_params(("arbitrary", "arbitrary")),
        name="adaln",
    )(c_all, w_ada, b_ada.reshape(depth, 1, n))


def _k1_specs(x3, mod3, sh_idx, sc_idx, tm):
    _, _, d = x3.shape
    rm = mod3.shape[1]
    return [
        pl.BlockSpec((1, tm, d), lambda g, t: (g, t, 0)),
        pl.BlockSpec((1, rm, d), lambda g, t: (g, 0, sh_idx)),
        pl.BlockSpec((1, rm, d), lambda g, t: (g, 0, sc_idx)),
        _const_spec((1, d)),
    ]


def _row_spec(tm, n):
    return pl.BlockSpec((1, tm, n), lambda g, t: (g, t, 0))


def _layer_state_spec(tail, j, bt=None):
    nd = len(tail)
    return pl.BlockSpec((None, bt) + tuple(tail), lambda b, *_: (j, b) + (0,) * nd)


def _call_into_stack(body, *, grid, inputs, in_specs, out_specs, out_shape, stack_out, prev, name,
                     scratch_shapes=()):
    inputs, in_specs = list(inputs), list(in_specs)
    n_in = len(inputs)
    aliases = {}
    kernel_fn = body
    if prev is not None:
        inputs.append(prev)
        in_specs.append(pl.BlockSpec(memory_space=pl.ANY))
        aliases = {n_in: stack_out}

        def kernel_fn(*refs):
            body(*refs[:n_in], *refs[n_in + 1:])

    return pl.pallas_call(
        kernel_fn, grid=grid, in_specs=in_specs, out_specs=out_specs, out_shape=out_shape,
        scratch_shapes=scratch_shapes, input_output_aliases=aliases,
        compiler_params=_params(("arbitrary",) * len(grid)), name=name)(*inputs)


def _k1_ret_body(x_ref, sh_ref, sc_ref, nw_ref, w_ref, cos_ref, sin_ref,
                 q_ref, k_ref, v_ref, g_ref, *, H, K, V):
    hb = _norm_mod(x_ref, sh_ref, sc_ref, nw_ref)
    cos = cos_ref[...]
    sin = sin_ref[...]
    half = K // 2
    for h in range(H):
        for dst, base, scale in ((q_ref, 0, None), (k_ref, H * K, K ** -0.5)):
            a = _dot(hb, w_ref[:, base + h * K:base + (h + 1) * K])
            x1 = a[:, :half]
            x2 = a[:, half:]
            r1 = x1 * cos - x2 * sin
            r2 = x2 * cos + x1 * sin
            if scale is not None:
                r1 = r1 * scale
                r2 = r2 * scale
            dst[0, :, h * K:h * K + half] = r1.astype(dst.dtype)
            dst[0, :, h * K + half:(h + 1) * K] = r2.astype(dst.dtype)
    vb = 2 * H * K
    for c in range(H):
        v_ref[0, :, c * V:(c + 1) * V] = _dot(hb, w_ref[:, vb + c * V:vb + (c + 1) * V]).astype(v_ref.dtype)
        g_ref[0, :, c * V:(c + 1) * V] = _dot(
            hb, w_ref[:, vb + H * V + c * V:vb + H * V + (c + 1) * V]).astype(g_ref.dtype)


def _k1_ret(x3, mod3, nw, w, cos, sin, tm, out_dtype, H, K, V):
    G, R, d = x3.shape
    rc = cos.shape[0]
    cs_spec = (pl.BlockSpec((tm, K // 2), lambda g, t: (t, 0)) if rc == R
               else _const_spec((1, K // 2)))
    outs = [jax.ShapeDtypeStruct((G, R, H * K), out_dtype), jax.ShapeDtypeStruct((G, R, H * K), out_dtype),
            jax.ShapeDtypeStruct((G, R, H * V), out_dtype), jax.ShapeDtypeStruct((G, R, H * V), out_dtype)]
    return pl.pallas_call(
        functools.partial(_k1_ret_body, H=H, K=K, V=V),
        grid=(G, R // tm),
        in_specs=_k1_specs(x3, mod3, 0, 1, tm) + [_w_spec(w), cs_spec, cs_spec],
        out_specs=[_row_spec(tm, H * K), _row_spec(tm, H * K), _row_spec(tm, H * V), _row_spec(tm, H * V)],
        out_shape=outs,
        compiler_params=_params(("arbitrary", "arbitrary")),
        name="ret_inproj",
    )(x3, mod3, mod3, nw, w[0], cos, sin)


def _k2_ret_body(q_ref, k_ref, v_ref, g_ref, dmat_ref, qdec_ref, kdec_ref, y_ref, s_ref,
                 *, H, K, V, C, sdec, n_sub):
    @pl.when(pl.program_id(1) == 0)
    def _():
        s_ref[...] = jnp.zeros_like(s_ref)

    for sub in range(n_sub):
        rows = slice(sub * C, (sub + 1) * C)
        for h in range(H):
            qh = q_ref[0, rows, h * K:(h + 1) * K]
            kh = k_ref[0, rows, h * K:(h + 1) * K]
            vh = v_ref[0, rows, h * V:(h + 1) * V]
            state = s_ref[h]
            scores = _dot_nt(qh, kh) * dmat_ref[h]
            o = _dot(scores.astype(BF16), vh) + _dot(qh, state.astype(BF16)) * qdec_ref[h]
            k_out = (kh.astype(F32) * kdec_ref[h]).astype(BF16)
            s_ref[h] = sdec[h] * state + _dot_tn(k_out, vh)
            gh = g_ref[0, rows, h * V:(h + 1) * V].astype(F32)
            y_ref[0, rows, h * V:(h + 1) * V] = (_rms(o) * _silu(gh)).astype(y_ref.dtype)


def _ret_tables(H, K, V, C):
    lg = np.log1p(-np.exp2(-5.0 - np.arange(H, dtype=np.float64)))
    i = np.arange(C, dtype=np.float64)
    diff = i[:, None] - i[None, :]
    dmat = np.where(diff >= 0, np.exp(lg[:, None, None] * np.maximum(diff, 0.0)), 0.0)
    qdec = np.broadcast_to(np.exp(lg[:, None, None] * (i[None, :, None] + 1.0)), (H, C, V))
    kdec = np.broadcast_to(np.exp(lg[:, None, None] * (C - 1.0 - i[None, :, None])), (H, C, K))
    sdec = tuple(float(np.exp(l * C)) for l in lg)
    return (jnp.asarray(dmat, F32), jnp.asarray(qdec, F32), jnp.asarray(kdec, F32), sdec)


def _k2_ret(q, k, v, g, H, K, V, nl, j, prev):
    G, R, _ = q.shape
    C = min(C_RET, R)
    n_sub = RET_CHUNKS_PER_STEP if R % (C * RET_CHUNKS_PER_STEP) == 0 else 1
    rows = C * n_sub
    dmat, qdec, kdec, sdec = _ret_tables(H, K, V, C)
    return _call_into_stack(
        functools.partial(_k2_ret_body, H=H, K=K, V=V, C=C, sdec=sdec, n_sub=n_sub),
        grid=(G, R // rows),
        inputs=[q, k, v, g, dmat, qdec, kdec],
        in_specs=[_row_spec(rows, H * K), _row_spec(rows, H * K), _row_spec(rows, H * V),
                  _row_spec(rows, H * V), _const_spec(dmat.shape), _const_spec(qdec.shape),
                  _const_spec(kdec.shape)],
        out_specs=[_row_spec(rows, H * V), _layer_state_spec((H, K, V), j)],
        out_shape=[jax.ShapeDtypeStruct((G, R, H * V), BF16), jax.ShapeDtypeStruct((nl, G, H, K, V), F32)],
        stack_out=1, prev=prev, name="ret_core")


def _dec_ret_body(q_ref, k_ref, v_ref, s_ref, so_ref, o_ref, *, H, K, V, gammas, bt):
    eye = lax.broadcasted_iota(jnp.int32, (K, K), 0) == lax.broadcasted_iota(jnp.int32, (K, K), 1)
    for i in range(bt):
        qrow = q_ref[i]
        krow = k_ref[i]
        vrow = v_ref[i]
        for h in range(H):
            qc = _col(qrow[:, h * K:(h + 1) * K], eye)
            kc = _col(krow[:, h * K:(h + 1) * K], eye)
            new = gammas[h] * s_ref[i, h] + kc * vrow[:, h * V:(h + 1) * V]
            so_ref[i, h] = new
            o_ref[i, :, h * V:(h + 1) * V] = jnp.sum(qc * new, axis=0, keepdims=True)


def _step_spec(n, bt=1):
    return pl.BlockSpec((bt, 1, n), lambda b: (b, 0, 0))


def _as_steps(x):
    return x.reshape(x.shape[0], 1, x.shape[1])


def _step_batch(B, want):
    return want if B % want == 0 else 1


def _dec_ret(q, k, v, states, H, K, V, j, prev):
    B = q.shape[0]
    gammas = tuple(float(1.0 - 2.0 ** (-5 - h)) for h in range(H))
    bt = _step_batch(B, BT_RET)
    return _call_into_stack(
        functools.partial(_dec_ret_body, H=H, K=K, V=V, gammas=gammas, bt=bt),
        grid=(B // bt,),
        inputs=[_as_steps(q), _as_steps(k), _as_steps(v), states],
        in_specs=[_step_spec(H * K, bt), _step_spec(H * K, bt), _step_spec(H * V, bt),
                  _layer_state_spec((H, K, V), j, bt)],
        out_specs=[_layer_state_spec((H, K, V), j, bt), _step_spec(H * V, bt)],
        out_shape=[jax.ShapeDtypeStruct(states.shape, F32), jax.ShapeDtypeStruct((B, 1, H * V), F32)],
        stack_out=0, prev=prev, name="ret_step")


def _ssd_route(c, DI, xs_ref, b_ref, c_ref):
    w = 512
    if c * w < DI:
        return xs_ref, c * w
    if c * w < DI + b_ref.shape[-1]:
        return b_ref, c * w - DI
    return c_ref, c * w - DI - b_ref.shape[-1]


def _k1_ssd_prompt_body(x_ref, sh_ref, sc_ref, nw_ref, w_ref, wdt_ref, cw_ref, cb_ref, dtb_ref,
                        z_ref, xs_ref, b_ref, c_ref, dt_ref, nconv_ref, cbuf, *, tm, DI, CD, nt):
    t = pl.program_id(1)
    hb = _norm_mod(x_ref, sh_ref, sc_ref, nw_ref)
    w = 512
    for c in range(DI // w):
        z_ref[0, :, c * w:(c + 1) * w] = _dot(hb, w_ref[:, c * w:(c + 1) * w]).astype(z_ref.dtype)
    dt_ref[0] = _softplus(_dot(hb, wdt_ref[...]) + dtb_ref[...])

    @pl.when(t == 0)
    def _():
        cbuf[0:SUBLANES, :] = jnp.zeros((SUBLANES, CD), F32)

    for c in range(CD // w):
        cols = slice(c * w, (c + 1) * w)
        raw = _dot(hb, w_ref[:, DI + c * w:DI + (c + 1) * w])
        cbuf[SUBLANES:SUBLANES + tm, cols] = raw
        conv = cb_ref[:, cols] + cw_ref[3:4, cols] * raw
        for k in range(SSD_CONV - 1):
            conv = conv + cw_ref[k:k + 1, cols] * cbuf[SUBLANES - 3 + k:SUBLANES - 3 + k + tm, cols]
        dst, off = _ssd_route(c, DI, xs_ref, b_ref, c_ref)
        dst[0, :, off:off + w] = _silu(conv).astype(dst.dtype)

    @pl.when(t == nt - 1)
    def _():
        nconv_ref[0] = cbuf[tm + SUBLANES - 3:tm + SUBLANES, :]

    cbuf[0:SUBLANES, :] = cbuf[tm:tm + SUBLANES, :]


def _k1_ssd_prompt(x3, mod3, nw, w, wdt, cw, cb, dtb, tm, DI, GN):
    G, R, d = x3.shape
    CD = DI + 2 * GN
    nt = R // tm
    outs = [jax.ShapeDtypeStruct((G, R, DI), BF16), jax.ShapeDtypeStruct((G, R, DI), BF16),
            jax.ShapeDtypeStruct((G, R, GN), BF16), jax.ShapeDtypeStruct((G, R, GN), BF16),
            jax.ShapeDtypeStruct((G, R, LANES), F32), jax.ShapeDtypeStruct((G, SSD_CONV - 1, CD), F32)]
    return pl.pallas_call(
        functools.partial(_k1_ssd_prompt_body, tm=tm, DI=DI, CD=CD, nt=nt),
        grid=(G, nt),
        in_specs=_k1_specs(x3, mod3, 0, 1, tm) + [_w_spec(w), _const_spec(wdt.shape),
                                                  _const_spec(cw.shape), _const_spec(cb.shape),
                                                  _const_spec(dtb.shape)],
        out_specs=[_row_spec(tm, DI), _row_spec(tm, DI), _row_spec(tm, GN), _row_spec(tm, GN),
                   _row_spec(tm, LANES), pl.BlockSpec((1, SSD_CONV - 1, CD), lambda g, t: (g, 0, 0))],
        out_shape=outs,
        scratch_shapes=[pltpu.VMEM((tm + SUBLANES, CD), F32)],
        compiler_params=_params(("arbitrary", "arbitrary")),
        name="ssd_inproj",
    )(x3, mod3, mod3, nw, w[0], wdt, cw, cb, dtb)


def _k1_ssd_sample_body(x_ref, sh_ref, sc_ref, nw_ref, w_ref, wdt_ref, cw_ref, cb_ref, dtb_ref, buf_ref,
                        alog_ref, exp_ref, z_ref, xs_ref, xdt_ref, b_ref, c_ref, dec_ref, nconv_ref,
                        *, DI, CD):
    hb = _norm_mod(x_ref, sh_ref, sc_ref, nw_ref)
    w = 512
    for c in range(DI // w):
        z_ref[0, :, c * w:(c + 1) * w] = _dot(hb, w_ref[:, c * w:(c + 1) * w])
    dt = _softplus(_dot(hb, wdt_ref[...]) + dtb_ref[...])
    dec_ref[0] = jnp.exp(dt * (-jnp.exp(alog_ref[...])))
    dt_rep = _dot_sel(_split3(dt), exp_ref[...])
    for c in range(CD // w):
        cols = slice(c * w, (c + 1) * w)
        raw = _dot(hb, w_ref[:, DI + c * w:DI + (c + 1) * w])
        conv = cb_ref[:, cols] + cw_ref[3:4, cols] * raw
        for k in range(SSD_CONV - 1):
            conv = conv + cw_ref[k:k + 1, cols] * buf_ref[k, :, cols]
        act = _silu(conv)
        dst, off = _ssd_route(c, DI, xs_ref, b_ref, c_ref)
        dst[0, :, off:off + w] = act
        if dst is xs_ref:
            xdt_ref[0, :, cols] = act * dt_rep[:, cols]
        nconv_ref[0, :, cols] = buf_ref[1, :, cols]
        nconv_ref[1, :, cols] = buf_ref[2, :, cols]
        nconv_ref[2, :, cols] = raw


def _k1_ssd_sample(x3, mod3, nw, w, wdt, cw, cb, dtb, buf_t, alog_row, expand, DI, GN):
    G, R, d = x3.shape
    CD = DI + 2 * GN
    tm = R
    outs = [jax.ShapeDtypeStruct((G, R, DI), F32), jax.ShapeDtypeStruct((G, R, DI), F32),
            jax.ShapeDtypeStruct((G, R, DI), F32),
            jax.ShapeDtypeStruct((G, R, GN), F32), jax.ShapeDtypeStruct((G, R, GN), F32),
            jax.ShapeDtypeStruct((G, R, LANES), F32), jax.ShapeDtypeStruct((SSD_CONV - 1, R, CD), F32)]
    return pl.pallas_call(
        functools.partial(_k1_ssd_sample_body, DI=DI, CD=CD),
        grid=(G, 1),
        in_specs=_k1_specs(x3, mod3, 0, 1, tm) + [_w_spec(w), _const_spec(wdt.shape),
                                                  _const_spec(cw.shape), _const_spec(cb.shape),
                                                  _const_spec(dtb.shape), _const_spec(buf_t.shape),
                                                  _const_spec(alog_row.shape), _const_spec(expand.shape)],
        out_specs=[_row_spec(tm, DI), _row_spec(tm, DI), _row_spec(tm, DI), _row_spec(tm, GN),
                   _row_spec(tm, GN), _row_spec(tm, LANES),
                   pl.BlockSpec((SSD_CONV - 1, R, CD), lambda g, t: (0, 0, 0))],
        out_shape=outs,
        compiler_params=_params(("arbitrary", "arbitrary")),
        name="ssd_inproj_step",
    )(x3, mod3, mod3, nw, w[0], wdt, cw, cb, dtb, buf_t, alog_row, expand)


def _k2_ssd_body(z_ref, xs_ref, b_ref, c_ref, dt_ref, tri_ref, alog_ref, exp_ref, drep_ref, nw_ref,
                 y_ref, so_ref, s_scr, *, C, NH, P, N, NG, nc, n_sub):
    ci = pl.program_id(1)

    @pl.when(ci == 0)
    def _():
        s_scr[...] = jnp.zeros_like(s_scr)

    for sub in range(n_sub):
        _ssd_chunk(slice(sub * C, (sub + 1) * C), z_ref, xs_ref, b_ref, c_ref, dt_ref, tri_ref, alog_ref,
                   exp_ref, drep_ref, nw_ref, y_ref, s_scr, C=C, NH=NH, P=P, N=N, NG=NG)

    @pl.when(ci == nc - 1)
    def _():
        for g in range(NG):
            so_ref[g] = s_scr[g].T


def _ssd_chunk(rows, z_ref, xs_ref, b_ref, c_ref, dt_ref, tri_ref, alog_ref, exp_ref, drep_ref, nw_ref,
               y_ref, s_scr, *, C, NH, P, N, NG):
    hpg = NH // NG
    gw = hpg * P
    dt = dt_ref[0, rows, :]
    la = dt * (-jnp.exp(alog_ref[...]))
    cum = _sel_dot(tri_ref[...], _split3(la))
    tot = cum[C - 1:C, :]
    c2 = cum * LOG2E
    r2_t = (c2 - jnp.log(dt) * LOG2E).T
    w_state_parts = _split3(jnp.exp(tot - cum) * dt)
    w_in_parts = _split3(jnp.exp(cum))
    row = lax.broadcasted_iota(jnp.int32, (C, C), 0)
    colm = lax.broadcasted_iota(jnp.int32, (C, C), 1)
    causal = row >= colm
    lane = lax.broadcasted_iota(jnp.int32, (1, 2 * P), 1)
    head_keep = ((lane < P).astype(BF16), (lane >= P).astype(BF16))
    for g in range(NG):
        gsl = slice(g * gw, (g + 1) * gw)
        bg = b_ref[0, rows, g * N:(g + 1) * N]
        cg = c_ref[0, rows, g * N:(g + 1) * N]
        expand = exp_ref[:, gsl]
        w_state = _dot_sel(w_state_parts, expand)
        w_in = _dot_sel(w_in_parts, expand)
        xs = xs_ref[0, rows, gsl]
        xsf = xs.astype(F32)
        xp = (xsf * w_state).astype(BF16)
        gmat = _dot_nt(cg, bg)
        state = s_scr[g]
        y_inter = _dot(cg, state.astype(BF16)) * w_in
        s_scr[g] = state * w_in[C - 1:C, :] + _dot_tn(bg, xp)
        ys = []
        for hp in range(hpg // 2):
            h0 = g * hpg + 2 * hp
            xpair = xs[:, 2 * hp * P:(2 * hp + 2) * P]
            acc = y_inter[:, 2 * hp * P:(2 * hp + 2) * P]
            for e in range(2):
                h = h0 + e
                seg2 = jnp.where(causal, c2[:, h:h + 1] - r2_t[h:h + 1, :], MASKED_EXPONENT)
                m = (gmat * jnp.exp2(seg2)).astype(BF16)
                acc = acc + _dot(m, xpair * head_keep[e])
            ys.append(acc)
        out = _ssd_gate(jnp.concatenate(ys, axis=-1), xsf, z_ref[0, rows, gsl].astype(F32), drep_ref[:, gsl],
                        nw_ref[:, gsl], 1)[0]
        y_ref[0, rows, gsl] = out.astype(y_ref.dtype)


def _ssd_expand(NH, P):
    e = np.zeros((LANES, NH * P), np.float32)
    for h in range(NH):
        e[h, h * P:(h + 1) * P] = 1.0
    return jnp.asarray(e, BF16)


def _k2_ssd(z, xs, bm, cm, dt, alog_row, expand, drep, nw, NH, P, N, NG, nl, j, prev):
    G, R, DI = xs.shape
    C = min(C_SSD, R)
    n_sub = SSD_CHUNKS_PER_STEP if R % (C * SSD_CHUNKS_PER_STEP) == 0 else 1
    nc = R // (C * n_sub)
    rows = C * n_sub
    i = np.arange(C)
    tri = jnp.asarray(i[:, None] >= i[None, :], BF16)
    gw = (NH // NG) * P
    return _call_into_stack(
        functools.partial(_k2_ssd_body, C=C, NH=NH, P=P, N=N, NG=NG, nc=nc, n_sub=n_sub),
        grid=(G, nc),
        inputs=[z, xs, bm, cm, dt, tri, alog_row, expand, drep, nw],
        in_specs=[_row_spec(rows, DI), _row_spec(rows, DI), _row_spec(rows, NG * N), _row_spec(rows, NG * N),
                  _row_spec(rows, LANES), _const_spec(tri.shape), _const_spec(alog_row.shape),
                  _const_spec(expand.shape), _const_spec(drep.shape), _const_spec(nw.shape)],
        out_specs=[_row_spec(rows, DI), _layer_state_spec((NG, gw, N), j)],
        out_shape=[jax.ShapeDtypeStruct((G, R, DI), BF16), jax.ShapeDtypeStruct((nl, G, NG, gw, N), F32)],
        scratch_shapes=[pltpu.VMEM((NG, N, gw), F32)],
        stack_out=1, prev=prev, name="ssd_core")


def _pad_rows(row, keep_first_only):
    x = jnp.broadcast_to(row, (MXU_PAD_ROWS, row.shape[-1]))
    if keep_first_only:
        first = lax.broadcasted_iota(jnp.int32, (MXU_PAD_ROWS, 1), 0) == 0
        x = jnp.where(first, x, 0.0)
    return x.astype(BF16)


def _dec_ssd_body(xdt_ref, b_ref, c_ref, dec_ref, s_ref, so_ref, o_ref, *, NH, P, N, NG, bt):
    hpg = NH // NG
    gw = hpg * P
    for i in range(bt):
        arow = dec_ref[i]
        v_pad = _pad_rows(xdt_ref[i], True)
        k_pad = _pad_rows(b_ref[i], False)
        q_pad = _pad_rows(c_ref[i], False)
        for g in range(NG):
            upd = _dot_tn(v_pad[:, g * gw:(g + 1) * gw], k_pad[:, g * N:(g + 1) * N])
            news = []
            for hl in range(hpg):
                h = g * hpg + hl
                rows = slice(hl * P, (hl + 1) * P)
                new = arow[:, h:h + 1] * s_ref[i, g, rows, :] + upd[rows, :]
                so_ref[i, g, rows, :] = new
                news.append(new.astype(BF16))
            o = _dot_nt(q_pad[:, g * N:(g + 1) * N], jnp.concatenate(news, axis=0))
            o_ref[i, :, g * gw:(g + 1) * gw] = o[0:1]


def _dec_ssd(xdt, bm, cm, dec, states_t, NH, P, N, NG, j, prev):
    B = xdt.shape[0]
    tail = states_t.shape[2:]
    bt = _step_batch(B, BT_SSD)
    return _call_into_stack(
        functools.partial(_dec_ssd_body, NH=NH, P=P, N=N, NG=NG, bt=bt),
        grid=(B // bt,),
        inputs=[_as_steps(xdt), _as_steps(bm), _as_steps(cm), _as_steps(dec), states_t],
        in_specs=[_step_spec(NH * P, bt), _step_spec(NG * N, bt), _step_spec(NG * N, bt),
                  _step_spec(LANES, bt), _layer_state_spec(tail, j, bt)],
        out_specs=[_layer_state_spec(tail, j, bt), _step_spec(NH * P, bt)],
        out_shape=[jax.ShapeDtypeStruct(states_t.shape, F32), jax.ShapeDtypeStruct((B, 1, NH * P), F32)],
        stack_out=0, prev=prev, name="ssd_step")


def _hg_lower_bound(lbl_ref, layer):
    logits = lbl_ref[...]
    e = jnp.exp(logits - jnp.max(logits, axis=0, keepdims=True))
    sm = e / jnp.sum(e, axis=0, keepdims=True)
    lb = jnp.zeros_like(sm[0:1])
    for i in range(1, layer + 1):
        lb = lb + sm[i:i + 1]
    return lb


def _k1_hg_body(x_ref, sh_ref, sc_ref, nw_ref, w_ref, lbl_ref, *out_refs, layer, KD, VD, prompt):
    hb = _norm_mod(x_ref, sh_ref, sc_ref, nw_ref)
    lb = _hg_lower_bound(lbl_ref, layer)
    w = 512
    if prompt:
        q_ref, la_ref, v_ref, g_ref = out_refs
    else:
        q_ref, k_ref, a_ref, v_ref, g_ref = out_refs
    for c in range(KD // w):
        cols = slice(c * w, (c + 1) * w)
        q = _silu(_dot(hb, w_ref[:, c * w:(c + 1) * w])) * (HG_KEY_DIM ** -0.5)
        q_ref[0, :, cols] = q.astype(q_ref.dtype)
        lbc = lb[:, cols]
        f = lbc + (1.0 - lbc) * jax.nn.sigmoid(_dot(hb, w_ref[:, KD + c * w:KD + (c + 1) * w]))
        if prompt:
            la_ref[0, :, cols] = jnp.log(f)
        else:
            k_ref[0, :, cols] = 1.0 - f
            a_ref[0, :, cols] = f
    for c in range(VD // w):
        cols = slice(c * w, (c + 1) * w)
        v_ref[0, :, cols] = _dot(hb, w_ref[:, 2 * KD + c * w:2 * KD + (c + 1) * w]).astype(v_ref.dtype)
        g_ref[0, :, cols] = _dot(hb, w_ref[:, 2 * KD + VD + c * w:2 * KD + VD + (c + 1) * w]).astype(g_ref.dtype)


def _k1_hg(x3, mod3, nw, w, lbl, tm, layer, KD, VD, prompt):
    G, R, d = x3.shape
    if prompt:
        dts = [(KD, BF16), (KD, F32), (VD, BF16), (VD, BF16)]
    else:
        dts = [(KD, F32), (KD, F32), (KD, F32), (VD, F32), (VD, F32)]
    return pl.pallas_call(
        functools.partial(_k1_hg_body, layer=layer, KD=KD, VD=VD, prompt=prompt),
        grid=(G, R // tm),
        in_specs=_k1_specs(x3, mod3, 0, 1, tm) + [_w_spec(w), _const_spec(lbl.shape)],
        out_specs=[_row_spec(tm, n) for n, _ in dts],
        out_shape=[jax.ShapeDtypeStruct((G, R, n), dt) for n, dt in dts],
        compiler_params=_params(("arbitrary", "arbitrary")),
        name="hg_inproj" if prompt else "hg_inproj_step",
    )(x3, mod3, mod3, nw, w[0], lbl)


def _hg_masks(C):
    i = np.arange(C)
    r, t = i[:, None], i[None, :]
    sels = [t <= r, (t <= r) & (t >= HG_SUB * (r // HG_SUB))]
    pairs = []
    s = C // 2
    while s >= HG_SUB:
        mid = 2 * s * (r // (2 * s)) + s - 1
        sels.append(np.where(r > mid, (t > mid) & (t <= r), (t > r) & (t <= mid)))
        pairs.append((r // (2 * s) == t // (2 * s)) & (r % (2 * s) >= s) & (t % (2 * s) < s))
        s //= 2
    pairs.append((r // HG_SUB == t // HG_SUB) & (t <= r))
    return (jnp.asarray(np.stack(sels), BF16), jnp.asarray(np.stack(pairs), F32))


def _hg_diag_collect(C, K):
    e = np.zeros((HG_SUB, K, C), np.float32)
    for j in range(HG_SUB):
        e[j, :, j::HG_SUB] = 1.0
    return jnp.asarray(e.reshape(HG_SUB * K, C), BF16)


def _k2_hg_body(q_ref, la_ref, v_ref, g_ref, sel_ref, pair_ref, coll_ref, nw_ref, y_ref, so_ref,
                st_scr, k_scr, c_scr, p_scr, *, C, H, K, V, nlev, nc, n_sub):
    ci = pl.program_id(1)

    @pl.when(ci == 0)
    def _():
        st_scr[...] = jnp.zeros_like(st_scr)

    for sub in range(n_sub):
        _hg_chunk(slice(sub * C, (sub + 1) * C), q_ref, la_ref, v_ref, g_ref, sel_ref, pair_ref, coll_ref,
                  nw_ref, y_ref, st_scr, k_scr.at[sub], c_scr.at[sub], p_scr.at[sub],
                  C=C, H=H, K=K, V=V, nlev=nlev)

    @pl.when(ci == nc - 1)
    def _():
        for h in range(H):
            so_ref[h] = st_scr[h].T


def _hg_chunk(rows, q_ref, la_ref, v_ref, g_ref, sel_ref, pair_ref, coll_ref, nw_ref, y_ref,
              st_scr, k_scr, c_scr, p_scr, *, C, H, K, V, nlev):
    D = H * K
    la2 = la_ref[0, rows, :] * LOG2E
    parts = _split3(la2)
    cum2 = _sel_dot(sel_ref[0], parts)
    cw2 = _sel_dot(sel_ref[1], parts)
    kf = 1.0 - jnp.exp2(la2)
    qf = q_ref[0, rows, :].astype(F32)
    vb = v_ref[0, rows, :]
    tot2 = cum2[C - 1:C, :]
    q_in = (qf * jnp.exp2(cum2)).astype(BF16)
    k_out = (kf * jnp.exp2(tot2 - cum2)).astype(BF16)
    e_tot = jnp.exp2(tot2)

    scores = [jnp.zeros((C, C), F32) for _ in range(H)]
    for lev in range(nlev):
        a = jnp.exp2(_sel_dot(sel_ref[2 + lev], parts))
        ql = (qf * a).astype(BF16)
        kl = (kf * a).astype(BF16)
        pm = pair_ref[lev] > 0.5
        for h in range(H):
            sl = slice(h * K, (h + 1) * K)
            scores[h] = scores[h] + jnp.where(pm, _dot_nt(ql[:, sl], kl[:, sl]), 0.0)

    k_scr[...] = kf
    c_scr[...] = cw2
    rowmod = lax.broadcasted_iota(jnp.int32, (C, D), 0) & (HG_SUB - 1)
    nblk = C // HG_SUB
    for j in range(HG_SUB):
        kb = jnp.concatenate([jnp.broadcast_to(k_scr[b * HG_SUB + j:b * HG_SUB + j + 1, :], (HG_SUB, D))
                              for b in range(nblk)], axis=0)
        cb = jnp.concatenate([jnp.broadcast_to(c_scr[b * HG_SUB + j:b * HG_SUB + j + 1, :], (HG_SUB, D))
                              for b in range(nblk)], axis=0)
        dec = jnp.exp2(jnp.where(rowmod >= j, cw2 - cb, MASKED_EXPONENT))
        p = (qf * kb * dec).astype(BF16)
        for h in range(H):
            p_scr[h * C:(h + 1) * C, j * K:(j + 1) * K] = p[:, h * K:(h + 1) * K]
    diag = _dot(p_scr[...], coll_ref[...])
    dm = pair_ref[nlev] > 0.5

    nw = nw_ref[...]
    for h in range(H):
        sl = slice(h * K, (h + 1) * K)
        vsl = slice(h * V, (h + 1) * V)
        st = st_scr[h]
        sc = scores[h] + jnp.where(dm, diag[h * C:(h + 1) * C, :], 0.0)
        o = _dot(sc.astype(BF16), vb[:, vsl]) + _dot_nt(q_in[:, sl], st.astype(BF16))
        st_scr[h] = st * e_tot[:, sl] + _dot_tn(vb[:, vsl], k_out[:, sl])
        gh = g_ref[0, rows, vsl].astype(F32)
        y_ref[0, rows, vsl] = (_rms(o) * nw[:, vsl] * _silu(gh)).astype(y_ref.dtype)


def _k2_hg(q, la, v, g, nw_rep, H, K, V, nl, j, prev):
    G, R, D = q.shape
    C = min(C_HG, R)
    n_sub = HG_CHUNKS_PER_STEP if R % (C * HG_CHUNKS_PER_STEP) == 0 else 1
    nc = R // (C * n_sub)
    sels, pairs = _hg_masks(C)
    nlev = pairs.shape[0] - 1
    coll = _hg_diag_collect(C, K)
    return _call_into_stack(
        functools.partial(_k2_hg_body, C=C, H=H, K=K, V=V, nlev=nlev, nc=nc, n_sub=n_sub),
        grid=(G, nc),
        inputs=[q, la, v, g, sels, pairs, coll, nw_rep],
        in_specs=[_row_spec(C * n_sub, D), _row_spec(C * n_sub, D), _row_spec(C * n_sub, H * V),
                  _row_spec(C * n_sub, H * V), _const_spec(sels.shape), _const_spec(pairs.shape),
                  _const_spec(coll.shape), _const_spec(nw_rep.shape)],
        out_specs=[_row_spec(C * n_sub, H * V), _layer_state_spec((H, K, V), j)],
        out_shape=[jax.ShapeDtypeStruct((G, R, H * V), BF16), jax.ShapeDtypeStruct((nl, G, H, K, V), F32)],
        scratch_shapes=[pltpu.VMEM((H, V, K), F32), pltpu.VMEM((n_sub, C, D), F32),
                        pltpu.VMEM((n_sub, C, D), F32), pltpu.VMEM((n_sub, H * C, HG_SUB * K), BF16)],
        stack_out=1, prev=prev, name="hg_core")


def _dec_hg_body(q_ref, k_ref, a_ref, v_ref, s_ref, so_ref, o_ref, *, H, K, V, bt):
    eye = lax.broadcasted_iota(jnp.int32, (K, K), 0) == lax.broadcasted_iota(jnp.int32, (K, K), 1)
    for i in range(bt):
        arow = a_ref[i]
        q_pad = _pad_rows(q_ref[i], False)
        k_pad = _pad_rows(k_ref[i], True)
        v_pad = _pad_rows(v_ref[i], False)
        for h in range(H):
            sl = slice(h * K, (h + 1) * K)
            vsl = slice(h * V, (h + 1) * V)
            ac = _col(arow[:, sl], eye)
            new = ac * s_ref[i, h] + _dot_tn(k_pad[:, sl], v_pad[:, vsl])
            so_ref[i, h] = new
            o_ref[i, :, vsl] = _dot(q_pad[:, sl], new.astype(BF16))[0:1]


def _dec_hg(q, k, a, v, states, H, K, V, j, prev):
    B = q.shape[0]
    bt = _step_batch(B, BT_HG)
    return _call_into_stack(
        functools.partial(_dec_hg_body, H=H, K=K, V=V, bt=bt),
        grid=(B // bt,),
        inputs=[_as_steps(q), _as_steps(k), _as_steps(a), _as_steps(v), states],
        in_specs=[_step_spec(H * K, bt), _step_spec(H * K, bt), _step_spec(H * K, bt), _step_spec(H * V, bt),
                  _layer_state_spec((H, K, V), j, bt)],
        out_specs=[_layer_state_spec((H, K, V), j, bt), _step_spec(H * V, bt)],
        out_shape=[jax.ShapeDtypeStruct(states.shape, F32), jax.ShapeDtypeStruct((B, 1, H * V), F32)],
        stack_out=0, prev=prev, name="hg_step")


def _f32_tile(ref):
    return ref[0].astype(F32)


def _pro_identity(y_ref):
    return y_ref[0]


def _pro_ret(o_ref, g_ref, *, H, V):
    return jnp.concatenate(_ret_gate(_f32_tile(o_ref), _f32_tile(g_ref), H, V), axis=-1).astype(BF16)


def _pro_ssd(o_ref, xs_ref, z_ref, drep_ref, nw_ref, *, NG):
    return jnp.concatenate(_ssd_gate(_f32_tile(o_ref), _f32_tile(xs_ref), _f32_tile(z_ref), drep_ref[...],
                                     nw_ref[...], NG), axis=-1).astype(BF16)


def _pro_hg(o_ref, g_ref, nw_ref, *, H, V):
    return jnp.concatenate(_hg_gate(_f32_tile(o_ref), _f32_tile(g_ref), nw_ref[...], H, V),
                           axis=-1).astype(BF16)


def _swiglu(hb, win_ref, wout_ref, F, bounds):
    acc = None
    for lo, hi in bounds:
        gt = _dot(hb, win_ref[:, lo:hi])
        up = _dot(hb, win_ref[:, F + lo:F + hi])
        a = (_silu(gt) * up).astype(BF16)
        part = _dot(a, wout_ref[lo:hi, :])
        acc = part if acc is None else acc + part
    return acc


def _k34_body(*refs, prologue, n_tile, n_const, F, bounds):
    n_pro = n_tile + n_const
    (wo_ref, x_ref, gm_ref, nwm_ref, sh_ref, sc_ref, nwf_ref, gf_ref, nwp_ref,
     win_ref, wout_ref, o_ref) = refs[n_pro:]
    y = prologue(*refs[:n_pro])
    x = x_ref[0] + gm_ref[0] * (_rms(_dot(y, wo_ref[...])) * nwm_ref[...])
    hb = (_rms(x) * nwf_ref[...] * (1.0 + sc_ref[0]) + sh_ref[0]).astype(BF16)
    acc = _swiglu(hb, win_ref, wout_ref, F, bounds)
    o_ref[0] = x + gf_ref[0] * (_rms(acc) * nwp_ref[...])


def _k34(prologue, tile_args, const_args, w_out, x3, mod3, nw_mix_post, nw_pre, nw_post, win, wout, tm):
    G, R, d = x3.shape
    rm = mod3.shape[1]
    F = wout[0].shape[1]
    mod_spec = lambda idx: pl.BlockSpec((1, rm, d), lambda g, t: (g, 0, idx))
    row = _const_spec((1, d))
    return pl.pallas_call(
        functools.partial(_k34_body, prologue=prologue, n_tile=len(tile_args), n_const=len(const_args),
                          F=F, bounds=_ffn_bounds(F)),
        grid=(G, R // tm),
        in_specs=[_row_spec(tm, a.shape[-1]) for a in tile_args] + [_const_spec(c.shape) for c in const_args]
        + [_w_spec(w_out), _row_spec(tm, d), mod_spec(2), row, mod_spec(3), mod_spec(4), row,
           mod_spec(5), row, _w_spec(win), _w_spec(wout)],
        out_specs=_row_spec(tm, d),
        out_shape=jax.ShapeDtypeStruct((G, R, d), F32),
        compiler_params=_params(("arbitrary", "arbitrary")),
        name="outproj_ffn",
    )(*tile_args, *const_args, w_out[0], x3, mod3, nw_mix_post, mod3, mod3, nw_pre, mod3, nw_post, win[0],
      wout[0])


def _ffn_bounds(F):
    tiles, rem = divmod(F, MXU_TILE)
    assert rem == 0, F
    n = -(-tiles // FFN_MAX_CHUNK_TILES)
    sizes = [tiles // n + (1 if i < tiles % n else 0) for i in range(n)]
    edges = np.cumsum([0] + sizes) * MXU_TILE
    return tuple((int(a), int(b)) for a, b in zip(edges[:-1], edges[1:]))


def _rope_table(pos, half):
    inv = 1.0 / (ROPE_BASE ** (jnp.arange(half, dtype=F32) / half))
    ang = pos.astype(F32)[:, None] * inv[None, :]
    return jnp.cos(ang), jnp.sin(ang)


def kernel(x_prompt, x_sample, c_prompt, c_sample, state_ret, state_ssd, state_conv, state_hgrn, w_ada, b_ada, norm_mix_pre, norm_mix_post, norm_ffn_pre, norm_ffn_post, ret_w_in, ret_w_out, ssd_w_in, ssd_conv_w, ssd_conv_b, ssd_dt_bias, ssd_a_log, ssd_d, ssd_norm, ssd_w_out, hg_w_in, hg_lb_logits, hg_norm, hg_w_out, ffn_w_in, ffn_w_out):
    bp, seq, d = x_prompt.shape
    bs = x_sample.shape[0]
    depth = w_ada.shape[0]
    ret_h = d // 256
    ret_k = d // ret_h
    ret_v = 2 * ret_k
    di = 2 * d
    ssd_nh = di // SSD_HEAD_DIM
    gn = SSD_GROUPS * SSD_STATE
    hg_h = d // HG_KEY_DIM
    hg_v = d // hg_h
    kd = hg_h * HG_KEY_DIM
    vd = hg_h * hg_v
    tm = min(TM_PROMPT, seq)

    mod = _adaln(jnp.concatenate([c_prompt, c_sample], axis=0), w_ada, b_ada)
    xp = x_prompt.astype(F32)
    xs = x_sample.astype(F32).reshape(1, bs, d)
    cos_p, sin_p = _rope_table(jnp.arange(seq, dtype=jnp.int32), ret_k // 2)
    cos_s, sin_s = _rope_table(jnp.full((1,), PAST_LEN, jnp.int32), ret_k // 2)

    hpg = ssd_nh // SSD_GROUPS
    ssd_states_t = jnp.swapaxes(state_ssd.astype(F32), 3, 4).reshape(
        state_ssd.shape[0], bs, SSD_GROUPS, hpg * SSD_HEAD_DIM, SSD_STATE)

    def ssd_untranspose(s):
        s = s.reshape(s.shape[0], s.shape[1], ssd_nh, SSD_HEAD_DIM, SSD_STATE)
        return jnp.swapaxes(s, 3, 4).astype(state_ssd.dtype)

    wb = {name: w.astype(BF16) for name, w in (
        ("ret_in", ret_w_in), ("ret_out", ret_w_out), ("ssd_in", ssd_w_in), ("ssd_out", ssd_w_out),
        ("hg_in", hg_w_in), ("hg_out", hg_w_out), ("ffn_in", ffn_w_in), ("ffn_out", ffn_w_out))}

    new = {k: [] for k in ("conv_p", "conv_s")}
    stk = {k: None for k in ("ret_p", "ret_s", "ssd_p", "ssd_s", "hg_p", "hg_s")}
    n_kind = [sum(1 for l in range(depth) if l % 3 == kind) for kind in range(3)]
    counts = [0, 0, 0]
    for layer in range(depth):
        mod_p = mod[layer, :bp].reshape(bp, 1, 6 * d)
        mod_s = mod[layer, bp:].reshape(1, bs, 6 * d)
        nw_pre = norm_mix_pre[layer].reshape(1, d)
        nw_post = norm_mix_post[layer].reshape(1, d)
        kind = layer % 3
        j = counts[kind]
        counts[kind] += 1
        if kind == 0:
            w_in = (wb["ret_in"], j)
            w_out = (wb["ret_out"], j)
            prologue = functools.partial(_pro_ret, H=ret_h, V=ret_v)
            q, k, v, g = _k1_ret(xp, mod_p, nw_pre, w_in, cos_p, sin_p, tm, BF16, ret_h, ret_k, ret_v)
            y, stk["ret_p"] = _k2_ret(q, k, v, g, ret_h, ret_k, ret_v, n_kind[0], j, stk["ret_p"])
            q, k, v, g = _k1_ret(xs, mod_s, nw_pre, w_in, cos_s, sin_s, bs, F32, ret_h, ret_k, ret_v)
            stk["ret_s"], o = _dec_ret(q[0], k[0], v[0], state_ret, ret_h, ret_k, ret_v, j, stk["ret_s"])
            pro_s = ([o.reshape(1, bs, -1), g], [])
        elif kind == 1:
            w_main = (wb["ssd_in"], j)
            w_dt = jnp.pad(ssd_w_in[j][:, di + di + 2 * gn:], ((0, 0), (0, LANES - ssd_nh))).astype(BF16)
            w_out = (wb["ssd_out"], j)
            cw = ssd_conv_w[j]
            cb = ssd_conv_b[j].reshape(1, -1)
            dtb = jnp.pad(ssd_dt_bias[j], (0, LANES - ssd_nh)).reshape(1, LANES)
            alog = jnp.pad(ssd_a_log[j].astype(F32), (0, LANES - ssd_nh)).reshape(1, LANES)
            drep = jnp.repeat(ssd_d[j], SSD_HEAD_DIM).reshape(1, di)
            nw_ssd = ssd_norm[j].reshape(1, di)
            z, xc, bm, cm, dt, nconv = _k1_ssd_prompt(xp, mod_p, nw_pre, w_main, w_dt, cw, cb, dtb, tm, di, gn)
            new["conv_p"].append(nconv)
            expand = _ssd_expand(ssd_nh, SSD_HEAD_DIM)
            prologue = functools.partial(_pro_ssd, NG=SSD_GROUPS)
            y, stk["ssd_p"] = _k2_ssd(z, xc, bm, cm, dt, alog, expand, drep, nw_ssd, ssd_nh, SSD_HEAD_DIM,
                                      SSD_STATE, SSD_GROUPS, n_kind[1], j, stk["ssd_p"])
            buf_t = jnp.transpose(state_conv[j].astype(F32), (1, 0, 2))
            z, xc, xdt, bm, cm, dec, nconv = _k1_ssd_sample(xs, mod_s, nw_pre, w_main, w_dt, cw, cb, dtb, buf_t,
                                                            alog, expand, di, gn)
            new["conv_s"].append(jnp.transpose(nconv, (1, 0, 2)))
            stk["ssd_s"], o = _dec_ssd(xdt[0], bm[0], cm[0], dec[0], ssd_states_t,
                                       ssd_nh, SSD_HEAD_DIM, SSD_STATE, SSD_GROUPS, j, stk["ssd_s"])
            pro_s = ([o.reshape(1, bs, -1), xc, z], [drep, nw_ssd])
        else:
            w_in = (wb["hg_in"], j)
            w_out = (wb["hg_out"], j)
            nw_hg = jnp.tile(hg_norm[j], hg_h).reshape(1, vd)
            lbl = hg_lb_logits.astype(F32)
            prologue = functools.partial(_pro_hg, H=hg_h, V=hg_v)
            q, la, v, g = _k1_hg(xp, mod_p, nw_pre, w_in, lbl, tm, layer, kd, vd, True)
            y, stk["hg_p"] = _k2_hg(q, la, v, g, nw_hg, hg_h, HG_KEY_DIM, hg_v, n_kind[2], j, stk["hg_p"])
            q, k, a, v, g = _k1_hg(xs, mod_s, nw_pre, w_in, lbl, bs, layer, kd, vd, False)
            stk["hg_s"], o = _dec_hg(q[0], k[0], a[0], v[0], state_hgrn, hg_h, HG_KEY_DIM, hg_v, j, stk["hg_s"])
            pro_s = ([o.reshape(1, bs, -1), g], [nw_hg])
        nf_pre = norm_ffn_pre[layer].reshape(1, d)
        nf_post = norm_ffn_post[layer].reshape(1, d)
        f_in = (wb["ffn_in"], layer)
        f_out = (wb["ffn_out"], layer)
        xp = _k34(_pro_identity, [y], [], w_out, xp, mod_p, nw_post, nf_pre, nf_post, f_in, f_out, tm)
        xs = _k34(prologue, *pro_s, w_out, xs, mod_s, nw_post, nf_pre, nf_post, f_in, f_out, bs)

    stack = lambda name, like: jnp.stack(new[name]).astype(like.dtype)
    return (xp.astype(x_prompt.dtype), xs.reshape(bs, 1, d).astype(x_sample.dtype),
            stk["ret_p"].astype(state_ret.dtype), stk["ret_s"].astype(state_ret.dtype),
            ssd_untranspose(stk["ssd_p"]), ssd_untranspose(stk["ssd_s"]),
            stack("conv_p", state_conv), stack("conv_s", state_conv),
            stk["hg_p"].astype(state_hgrn.dtype), stk["hg_s"].astype(state_hgrn.dtype))
```

```python
import functools

import numpy as np
import jax
import jax.numpy as jnp
from jax import lax
from jax.experimental import pallas as pl
from jax.experimental.pallas import tpu as pltpu

F32 = jnp.float32
BF16 = jnp.bfloat16
EPS = 1e-6
PAST_LEN = 16384
ROPE_BASE = 10000.0
SSD_HEAD_DIM = 64
SSD_GROUPS = 4
SSD_STATE = 128
SSD_CONV = 4
HG_KEY_DIM = 128
LANES = 128
SUBLANES = 8
VMEM_LIMIT = 56 * 1024 * 1024
TM_PROMPT = 512
MXU_TILE = 256
MXU_PAD_ROWS = 16
C_RET = 256
C_SSD = 128
C_HG = 128
HG_SUB = 8
SSD_CHUNKS_PER_STEP = 4
HG_CHUNKS_PER_STEP = 1
RET_CHUNKS_PER_STEP = 2
FFN_MAX_CHUNK_TILES = 6
LOG2E = 1.4426950408889634
MASKED_EXPONENT = -1e30
BT_RET = 4
BT_SSD = 4
BT_HG = 8


def _rms(x):
    return x * lax.rsqrt(jnp.mean(x * x, axis=-1, keepdims=True) + EPS)


def _silu(x):
    h = 0.5 * x
    return h + h * jnp.tanh(h)


def _sigmoid(x):
    return 0.5 + 0.5 * jnp.tanh(0.5 * x)


def _softplus(x):
    return jnp.maximum(x, 0.0) + jnp.log1p(jnp.exp(-jnp.abs(x)))


def _dot(a, b):
    return jnp.dot(a, b, preferred_element_type=F32)


def _dot_nt(a, b):
    return lax.dot_general(a, b, (((1,), (1,)), ((), ())), preferred_element_type=F32)


def _dot_tn(a, b):
    return lax.dot_general(a, b, (((0,), (0,)), ((), ())), preferred_element_type=F32)


def _split3(x):
    hi = x.astype(BF16)
    r = x - hi.astype(F32)
    mid = r.astype(BF16)
    lo = (r - mid.astype(F32)).astype(BF16)
    return hi, mid, lo


def _sel_dot(sel, parts):
    acc = _dot(sel, parts[0])
    for p in parts[1:]:
        acc = acc + _dot(sel, p)
    return acc


def _dot_sel(parts, sel):
    acc = _dot(parts[0], sel)
    for p in parts[1:]:
        acc = acc + _dot(p, sel)
    return acc


def _col(row, eye):
    return jnp.sum(jnp.where(eye, row, 0.0), axis=1, keepdims=True)


def _norm_mod(x_ref, sh_ref, sc_ref, nw_ref):
    x = x_ref[0]
    h = _rms(x) * nw_ref[...] * (1.0 + sc_ref[0]) + sh_ref[0]
    return h.astype(BF16)


def _params(sem):
    return pltpu.CompilerParams(dimension_semantics=sem, vmem_limit_bytes=VMEM_LIMIT)


def _const_spec(shape):
    nd = len(shape)
    return pl.BlockSpec(shape, lambda *_: (0,) * nd, pipeline_mode=pl.Buffered(1))


def _w_spec(wl):
    stack, j = wl
    return pl.BlockSpec((None,) + stack.shape[1:], lambda *_: (j, 0, 0), pipeline_mode=pl.Buffered(1))


def _ret_gate(o, g, H, V):
    outs = []
    for h in range(H):
        outs.append(_rms(o[:, h * V:(h + 1) * V]) * _silu(g[:, h * V:(h + 1) * V]))
    return outs


def _ssd_gate(y, xs, z, d_rep, nw, groups):
    y = (y + d_rep * xs) * _silu(z)
    w = y.shape[-1] // groups
    outs = []
    for g in range(groups):
        outs.append(_rms(y[:, g * w:(g + 1) * w]) * nw[:, g * w:(g + 1) * w])
    return outs


def _hg_gate(o, g, nw, H, V):
    outs = []
    for h in range(H):
        sl = slice(h * V, (h + 1) * V)
        outs.append(_rms(o[:, sl]) * nw[:, sl] * _silu(g[:, sl]))
    return outs


def _adaln_body(c_ref, w_ref, b_ref, o_ref):
    a = _silu(c_ref[...]).astype(BF16)
    o_ref[0] = _dot(a, w_ref[0].astype(BF16)) + b_ref[0]


def _adaln(c_all, w_ada, b_ada):
    depth, d, n = w_ada.shape
    bc = c_all.shape[0]
    tn = 1024
    return pl.pallas_call(
        _adaln_body,
        grid=(depth, n // tn),
        in_specs=[
            pl.BlockSpec((bc, d), lambda l, j: (0, 0)),
            pl.BlockSpec((1, d, tn), lambda l, j: (l, 0, j)),
            pl.BlockSpec((1, 1, tn), lambda l, j: (l, 0, j)),
        ],
        out_specs=pl.BlockSpec((1, bc, tn), lambda l, j: (l, 0, j)),
        out_shape=jax.ShapeDtypeStruct((depth, bc, n), F32),
        compiler_params=_params(("arbitrary", "arbitrary")),
        name="adaln",
    )(c_all, w_ada, b_ada.reshape(depth, 1, n))


def _k1_specs(x3, mod3, sh_idx, sc_idx, tm):
    _, _, d = x3.shape
    rm = mod3.shape[1]
    return [
        pl.BlockSpec((1, tm, d), lambda g, t: (g, t, 0)),
        pl.BlockSpec((1, rm, d), lambda g, t: (g, 0, sh_idx)),
        pl.BlockSpec((1, rm, d), lambda g, t: (g, 0, sc_idx)),
        _const_spec((1, d)),
    ]


def _row_spec(tm, n):
    return pl.BlockSpec((1, tm, n), lambda g, t: (g, t, 0))


def _layer_state_spec(tail, j, bt=None):
    nd = len(tail)
    return pl.BlockSpec((None, bt) + tuple(tail), lambda b, *_: (j, b) + (0,) * nd)


def _call_into_stack(body, *, grid, inputs, in_specs, out_specs, out_shape, stack_out, prev, name,
                     scratch_shapes=()):
    inputs, in_specs = list(inputs), list(in_specs)
    n_in = len(inputs)
    aliases = {}
    kernel_fn = body
    if prev is not None:
        inputs.append(prev)
        in_specs.append(pl.BlockSpec(memory_space=pl.ANY))
        aliases = {n_in: stack_out}

        def kernel_fn(*refs):
            body(*refs[:n_in], *refs[n_in + 1:])

    return pl.pallas_call(
        kernel_fn, grid=grid, in_specs=in_specs, out_specs=out_specs, out_shape=out_shape,
        scratch_shapes=scratch_shapes, input_output_aliases=aliases,
        compiler_params=_params(("arbitrary",) * len(grid)), name=name)(*inputs)


def _k1_ret_body(x_ref, sh_ref, sc_ref, nw_ref, w_ref, cos_ref, sin_ref,
                 q_ref, k_ref, v_ref, g_ref, *, H, K, V):
    hb = _norm_mod(x_ref, sh_ref, sc_ref, nw_ref)
    cos = cos_ref[...]
    sin = sin_ref[...]
    half = K // 2
    for h in range(H):
        for dst, base, scale in ((q_ref, 0, None), (k_ref, H * K, K ** -0.5)):
            a = _dot(hb, w_ref[:, base + h * K:base + (h + 1) * K])
            x1 = a[:, :half]
            x2 = a[:, half:]
            r1 = x1 * cos - x2 * sin
            r2 = x2 * cos + x1 * sin
            if scale is not None:
                r1 = r1 * scale
                r2 = r2 * scale
            dst[0, :, h * K:h * K + half] = r1.astype(dst.dtype)
            dst[0, :, h * K + half:(h + 1) * K] = r2.astype(dst.dtype)
    vb = 2 * H * K
    for c in range(H):
        v_ref[0, :, c * V:(c + 1) * V] = _dot(hb, w_ref[:, vb + c * V:vb + (c + 1) * V]).astype(v_ref.dtype)
        g_ref[0, :, c * V:(c + 1) * V] = _dot(
            hb, w_ref[:, vb + H * V + c * V:vb + H * V + (c + 1) * V]).astype(g_ref.dtype)


def _k1_ret(x3, mod3, nw, w, cos, sin, tm, out_dtype, H, K, V):
    G, R, d = x3.shape
    rc = cos.shape[0]
    cs_spec = (pl.BlockSpec((tm, K // 2), lambda g, t: (t, 0)) if rc == R
               else _const_spec((1, K // 2)))
    outs = [jax.ShapeDtypeStruct((G, R, H * K), out_dtype), jax.ShapeDtypeStruct((G, R, H * K), out_dtype),
            jax.ShapeDtypeStruct((G, R, H * V), out_dtype), jax.ShapeDtypeStruct((G, R, H * V), out_dtype)]
    return pl.pallas_call(
        functools.partial(_k1_ret_body, H=H, K=K, V=V),
        grid=(G, R // tm),
        in_specs=_k1_specs(x3, mod3, 0, 1, tm) + [_w_spec(w), cs_spec, cs_spec],
        out_specs=[_row_spec(tm, H * K), _row_spec(tm, H * K), _row_spec(tm, H * V), _row_spec(tm, H * V)],
        out_shape=outs,
        compiler_params=_params(("arbitrary", "arbitrary")),
        name="ret_inproj",
    )(x3, mod3, mod3, nw, w[0], cos, sin)


def _k2_ret_body(q_ref, k_ref, v_ref, g_ref, dmat_ref, qdec_ref, kdec_ref, y_ref, s_ref,
                 *, H, K, V, C, sdec, n_sub):
    @pl.when(pl.program_id(1) == 0)
    def _():
        s_ref[...] = jnp.zeros_like(s_ref)

    for sub in range(n_sub):
        rows = slice(sub * C, (sub + 1) * C)
        for h in range(H):
            qh = q_ref[0, rows, h * K:(h + 1) * K]
            kh = k_ref[0, rows, h * K:(h + 1) * K]
            vh = v_ref[0, rows, h * V:(h + 1) * V]
            state = s_ref[h]
            scores = _dot_nt(qh, kh) * dmat_ref[h]
            o = _dot(scores.astype(BF16), vh) + _dot(qh, state.astype(BF16)) * qdec_ref[h]
            k_out = (kh.astype(F32) * kdec_ref[h]).astype(BF16)
            s_ref[h] = sdec[h] * state + _dot_tn(k_out, vh)
            gh = g_ref[0, rows, h * V:(h + 1) * V].astype(F32)
            y_ref[0, rows, h * V:(h + 1) * V] = (_rms(o) * _silu(gh)).astype(y_ref.dtype)


def _ret_tables(H, K, V, C):
    lg = np.log1p(-np.exp2(-5.0 - np.arange(H, dtype=np.float64)))
    i = np.arange(C, dtype=np.float64)
    diff = i[:, None] - i[None, :]
    dmat = np.where(diff >= 0, np.exp(lg[:, None, None] * np.maximum(diff, 0.0)), 0.0)
    qdec = np.broadcast_to(np.exp(lg[:, None, None] * (i[None, :, None] + 1.0)), (H, C, V))
    kdec = np.broadcast_to(np.exp(lg[:, None, None] * (C - 1.0 - i[None, :, None])), (H, C, K))
    sdec = tuple(float(np.exp(l * C)) for l in lg)
    return (jnp.asarray(dmat, F32), jnp.asarray(qdec, F32), jnp.asarray(kdec, F32), sdec)


def _k2_ret(q, k, v, g, H, K, V, nl, j, prev):
    G, R, _ = q.shape
    C = min(C_RET, R)
    n_sub = RET_CHUNKS_PER_STEP if R % (C * RET_CHUNKS_PER_STEP) == 0 else 1
    rows = C * n_sub
    dmat, qdec, kdec, sdec = _ret_tables(H, K, V, C)
    return _call_into_stack(
        functools.partial(_k2_ret_body, H=H, K=K, V=V, C=C, sdec=sdec, n_sub=n_sub),
        grid=(G, R // rows),
        inputs=[q, k, v, g, dmat, qdec, kdec],
        in_specs=[_row_spec(rows, H * K), _row_spec(rows, H * K), _row_spec(rows, H * V),
                  _row_spec(rows, H * V), _const_spec(dmat.shape), _const_spec(qdec.shape),
                  _const_spec(kdec.shape)],
        out_specs=[_row_spec(rows, H * V), _layer_state_spec((H, K, V), j)],
        out_shape=[jax.ShapeDtypeStruct((G, R, H * V), BF16), jax.ShapeDtypeStruct((nl, G, H, K, V), F32)],
        stack_out=1, prev=prev, name="ret_core")


def _dec_ret_body(q_ref, k_ref, v_ref, s_ref, so_ref, o_ref, *, H, K, V, gammas, bt):
    eye = lax.broadcasted_iota(jnp.int32, (K, K), 0) == lax.broadcasted_iota(jnp.int32, (K, K), 1)
    for i in range(bt):
        qrow = q_ref[i]
        krow = k_ref[i]
        vrow = v_ref[i]
        for h in range(H):
            qc = _col(qrow[:, h * K:(h + 1) * K], eye)
            kc = _col(krow[:, h * K:(h + 1) * K], eye)
            new = gammas[h] * s_ref[i, h] + kc * vrow[:, h * V:(h + 1) * V]
            so_ref[i, h] = new
            o_ref[i, :, h * V:(h + 1) * V] = jnp.sum(qc * new, axis=0, keepdims=True)


def _step_spec(n, bt=1):
    return pl.BlockSpec((bt, 1, n), lambda b: (b, 0, 0))


def _as_steps(x):
    return x.reshape(x.shape[0], 1, x.shape[1])


def _step_batch(B, want):
    return want if B % want == 0 else 1


def _dec_ret(q, k, v, states, H, K, V, j, prev):
    B = q.shape[0]
    gammas = tuple(float(1.0 - 2.0 ** (-5 - h)) for h in range(H))
    bt = _step_batch(B, BT_RET)
    return _call_into_stack(
        functools.partial(_dec_ret_body, H=H, K=K, V=V, gammas=gammas, bt=bt),
        grid=(B // bt,),
        inputs=[_as_steps(q), _as_steps(k), _as_steps(v), states],
        in_specs=[_step_spec(H * K, bt), _step_spec(H * K, bt), _step_spec(H * V, bt),
                  _layer_state_spec((H, K, V), j, bt)],
        out_specs=[_layer_state_spec((H, K, V), j, bt), _step_spec(H * V, bt)],
        out_shape=[jax.ShapeDtypeStruct(states.shape, F32), jax.ShapeDtypeStruct((B, 1, H * V), F32)],
        stack_out=0, prev=prev, name="ret_step")


def _ssd_route(c, DI, xs_ref, b_ref, c_ref):
    w = 512
    if c * w < DI:
        return xs_ref, c * w
    if c * w < DI + b_ref.shape[-1]:
        return b_ref, c * w - DI
    return c_ref, c * w - DI - b_ref.shape[-1]


def _k1_ssd_prompt_body(x_ref, sh_ref, sc_ref, nw_ref, w_ref, wdt_ref, cw_ref, cb_ref, dtb_ref,
                        z_ref, xs_ref, b_ref, c_ref, dt_ref, nconv_ref, cbuf, *, tm, DI, CD, nt):
    t = pl.program_id(1)
    hb = _norm_mod(x_ref, sh_ref, sc_ref, nw_ref)
    w = 512
    for c in range(DI // w):
        z_ref[0, :, c * w:(c + 1) * w] = _dot(hb, w_ref[:, c * w:(c + 1) * w]).astype(z_ref.dtype)
    dt_ref[0] = _softplus(_dot(hb, wdt_ref[...]) + dtb_ref[...])

    @pl.when(t == 0)
    def _():
        cbuf[0:SUBLANES, :] = jnp.zeros((SUBLANES, CD), F32)

    for c in range(CD // w):
        cols = slice(c * w, (c + 1) * w)
        raw = _dot(hb, w_ref[:, DI + c * w:DI + (c + 1) * w])
        cbuf[SUBLANES:SUBLANES + tm, cols] = raw
        conv = cb_ref[:, cols] + cw_ref[3:4, cols] * raw
        for k in range(SSD_CONV - 1):
            conv = conv + cw_ref[k:k + 1, cols] * cbuf[SUBLANES - 3 + k:SUBLANES - 3 + k + tm, cols]
        dst, off = _ssd_route(c, DI, xs_ref, b_ref, c_ref)
        dst[0, :, off:off + w] = _silu(conv).astype(dst.dtype)

    @pl.when(t == nt - 1)
    def _():
        nconv_ref[0] = cbuf[tm + SUBLANES - 3:tm + SUBLANES, :]

    cbuf[0:SUBLANES, :] = cbuf[tm:tm + SUBLANES, :]


def _k1_ssd_prompt(x3, mod3, nw, w, wdt, cw, cb, dtb, tm, DI, GN):
    G, R, d = x3.shape
    CD = DI + 2 * GN
    nt = R // tm
    outs = [jax.ShapeDtypeStruct((G, R, DI), BF16), jax.ShapeDtypeStruct((G, R, DI), BF16),
            jax.ShapeDtypeStruct((G, R, GN), BF16), jax.ShapeDtypeStruct((G, R, GN), BF16),
            jax.ShapeDtypeStruct((G, R, LANES), F32), jax.ShapeDtypeStruct((G, SSD_CONV - 1, CD), F32)]
    return pl.pallas_call(
        functools.partial(_k1_ssd_prompt_body, tm=tm, DI=DI, CD=CD, nt=nt),
        grid=(G, nt),
        in_specs=_k1_specs(x3, mod3, 0, 1, tm) + [_w_spec(w), _const_spec(wdt.shape),
                                                  _const_spec(cw.shape), _const_spec(cb.shape),
                                                  _const_spec(dtb.shape)],
        out_specs=[_row_spec(tm, DI), _row_spec(tm, DI), _row_spec(tm, GN), _row_spec(tm, GN),
                   _row_spec(tm, LANES), pl.BlockSpec((1, SSD_CONV - 1, CD), lambda g, t: (g, 0, 0))],
        out_shape=outs,
        scratch_shapes=[pltpu.VMEM((tm + SUBLANES, CD), F32)],
        compiler_params=_params(("arbitrary", "arbitrary")),
        name="ssd_inproj",
    )(x3, mod3, mod3, nw, w[0], wdt, cw, cb, dtb)


def _k1_ssd_sample_body(x_ref, sh_ref, sc_ref, nw_ref, w_ref, wdt_ref, cw_ref, cb_ref, dtb_ref, buf_ref,
                        alog_ref, exp_ref, z_ref, xs_ref, xdt_ref, b_ref, c_ref, dec_ref, nconv_ref,
                        *, DI, CD):
    hb = _norm_mod(x_ref, sh_ref, sc_ref, nw_ref)
    w = 512
    for c in range(DI // w):
        z_ref[0, :, c * w:(c + 1) * w] = _dot(hb, w_ref[:, c * w:(c + 1) * w])
    dt = _softplus(_dot(hb, wdt_ref[...]) + dtb_ref[...])
    dec_ref[0] = jnp.exp(dt * (-jnp.exp(alog_ref[...])))
    dt_rep = _dot_sel(_split3(dt), exp_ref[...])
    for c in range(CD // w):
        cols = slice(c * w, (c + 1) * w)
        raw = _dot(hb, w_ref[:, DI + c * w:DI + (c + 1) * w])
        conv = cb_ref[:, cols] + cw_ref[3:4, cols] * raw
        for k in range(SSD_CONV - 1):
            conv = conv + cw_ref[k:k + 1, cols] * buf_ref[k, :, cols]
        act = _silu(conv)
        dst, off = _ssd_route(c, DI, xs_ref, b_ref, c_ref)
        dst[0, :, off:off + w] = act
        if dst is xs_ref:
            xdt_ref[0, :, cols] = act * dt_rep[:, cols]
        nconv_ref[0, :, cols] = buf_ref[1, :, cols]
        nconv_ref[1, :, cols] = buf_ref[2, :, cols]
        nconv_ref[2, :, cols] = raw


def _k1_ssd_sample(x3, mod3, nw, w, wdt, cw, cb, dtb, buf_t, alog_row, expand, DI, GN):
    G, R, d = x3.shape
    CD = DI + 2 * GN
    tm = R
    outs = [jax.ShapeDtypeStruct((G, R, DI), F32), jax.ShapeDtypeStruct((G, R, DI), F32),
            jax.ShapeDtypeStruct((G, R, DI), F32),
            jax.ShapeDtypeStruct((G, R, GN), F32), jax.ShapeDtypeStruct((G, R, GN), F32),
            jax.ShapeDtypeStruct((G, R, LANES), F32), jax.ShapeDtypeStruct((SSD_CONV - 1, R, CD), F32)]
    return pl.pallas_call(
        functools.partial(_k1_ssd_sample_body, DI=DI, CD=CD),
        grid=(G, 1),
        in_specs=_k1_specs(x3, mod3, 0, 1, tm) + [_w_spec(w), _const_spec(wdt.shape),
                                                  _const_spec(cw.shape), _const_spec(cb.shape),
                                                  _const_spec(dtb.shape), _const_spec(buf_t.shape),
                                                  _const_spec(alog_row.shape), _const_spec(expand.shape)],
        out_specs=[_row_spec(tm, DI), _row_spec(tm, DI), _row_spec(tm, DI), _row_spec(tm, GN),
                   _row_spec(tm, GN), _row_spec(tm, LANES),
                   pl.BlockSpec((SSD_CONV - 1, R, CD), lambda g, t: (0, 0, 0))],
        out_shape=outs,
        compiler_params=_params(("arbitrary", "arbitrary")),
        name="ssd_inproj_step",
    )(x3, mod3, mod3, nw, w[0], wdt, cw, cb, dtb, buf_t, alog_row, expand)


def _k2_ssd_body(z_ref, xs_ref, b_ref, c_ref, dt_ref, tri_ref, alog_ref, exp_ref, drep_ref, nw_ref,
                 y_ref, so_ref, s_scr, *, C, NH, P, N, NG, nc, n_sub):
    ci = pl.program_id(1)

    @pl.when(ci == 0)
    def _():
        s_scr[...] = jnp.zeros_like(s_scr)

    for sub in range(n_sub):
        _ssd_chunk(slice(sub * C, (sub + 1) * C), z_ref, xs_ref, b_ref, c_ref, dt_ref, tri_ref, alog_ref,
                   exp_ref, drep_ref, nw_ref, y_ref, s_scr, C=C, NH=NH, P=P, N=N, NG=NG)

    @pl.when(ci == nc - 1)
    def _():
        for g in range(NG):
            so_ref[g] = s_scr[g].T


def _ssd_chunk(rows, z_ref, xs_ref, b_ref, c_ref, dt_ref, tri_ref, alog_ref, exp_ref, drep_ref, nw_ref,
               y_ref, s_scr, *, C, NH, P, N, NG):
    hpg = NH // NG
    gw = hpg * P
    dt = dt_ref[0, rows, :]
    la = dt * (-jnp.exp(alog_ref[...]))
    cum = _sel_dot(tri_ref[...], _split3(la))
    tot = cum[C - 1:C, :]
    c2 = cum * LOG2E
    r2_t = (c2 - jnp.log(dt) * LOG2E).T
    w_state_parts = _split3(jnp.exp(tot - cum) * dt)
    w_in_parts = _split3(jnp.exp(cum))
    row = lax.broadcasted_iota(jnp.int32, (C, C), 0)
    colm = lax.broadcasted_iota(jnp.int32, (C, C), 1)
    causal = row >= colm
    lane = lax.broadcasted_iota(jnp.int32, (1, 2 * P), 1)
    head_keep = ((lane < P).astype(BF16), (lane >= P).astype(BF16))
    for g in range(NG):
        gsl = slice(g * gw, (g + 1) * gw)
        bg = b_ref[0, rows, g * N:(g + 1) * N]
        cg = c_ref[0, rows, g * N:(g + 1) * N]
        expand = exp_ref[:, gsl]
        w_state = _dot_sel(w_state_parts, expand)
        w_in = _dot_sel(w_in_parts, expand)
        xs = xs_ref[0, rows, gsl]
        xsf = xs.astype(F32)
        xp = (xsf * w_state).astype(BF16)
        gmat = _dot_nt(cg, bg)
        state = s_scr[g]
        y_inter = _dot(cg, state.astype(BF16)) * w_in
        s_scr[g] = state * w_in[C - 1:C, :] + _dot_tn(bg, xp)
        ys = []
        for hp in range(hpg // 2):
            h0 = g * hpg + 2 * hp
            xpair = xs[:, 2 * hp * P:(2 * hp + 2) * P]
            acc = y_inter[:, 2 * hp * P:(2 * hp + 2) * P]
            for e in range(2):
                h = h0 + e
                seg2 = jnp.where(causal, c2[:, h:h + 1] - r2_t[h:h + 1, :], MASKED_EXPONENT)
                m = (gmat * jnp.exp2(seg2)).astype(BF16)
                acc = acc + _dot(m, xpair * head_keep[e])
            ys.append(acc)
        out = _ssd_gate(jnp.concatenate(ys, axis=-1), xsf, z_ref[0, rows, gsl].astype(F32), drep_ref[:, gsl],
                        nw_ref[:, gsl], 1)[0]
        y_ref[0, rows, gsl] = out.astype(y_ref.dtype)


def _ssd_expand(NH, P):
    e = np.zeros((LANES, NH * P), np.float32)
    for h in range(NH):
        e[h, h * P:(h + 1) * P] = 1.0
    return jnp.asarray(e, BF16)


def _k2_ssd(z, xs, bm, cm, dt, alog_row, expand, drep, nw, NH, P, N, NG, nl, j, prev):
    G, R, DI = xs.shape
    C = min(C_SSD, R)
    n_sub = SSD_CHUNKS_PER_STEP if R % (C * SSD_CHUNKS_PER_STEP) == 0 else 1
    nc = R // (C * n_sub)
    rows = C * n_sub
    i = np.arange(C)
    tri = jnp.asarray(i[:, None] >= i[None, :], BF16)
    gw = (NH // NG) * P
    return _call_into_stack(
        functools.partial(_k2_ssd_body, C=C, NH=NH, P=P, N=N, NG=NG, nc=nc, n_sub=n_sub),
        grid=(G, nc),
        inputs=[z, xs, bm, cm, dt, tri, alog_row, expand, drep, nw],
        in_specs=[_row_spec(rows, DI), _row_spec(rows, DI), _row_spec(rows, NG * N), _row_spec(rows, NG * N),
                  _row_spec(rows, LANES), _const_spec(tri.shape), _const_spec(alog_row.shape),
                  _const_spec(expand.shape), _const_spec(drep.shape), _const_spec(nw.shape)],
        out_specs=[_row_spec(rows, DI), _layer_state_spec((NG, gw, N), j)],
        out_shape=[jax.ShapeDtypeStruct((G, R, DI), BF16), jax.ShapeDtypeStruct((nl, G, NG, gw, N), F32)],
        scratch_shapes=[pltpu.VMEM((NG, N, gw), F32)],
        stack_out=1, prev=prev, name="ssd_core")


def _pad_rows(row, keep_first_only):
    x = jnp.broadcast_to(row, (MXU_PAD_ROWS, row.shape[-1]))
    if keep_first_only:
        first = lax.broadcasted_iota(jnp.int32, (MXU_PAD_ROWS, 1), 0) == 0
        x = jnp.where(first, x, 0.0)
    return x.astype(BF16)


def _dec_ssd_body(xdt_ref, b_ref, c_ref, dec_ref, s_ref, so_ref, o_ref, *, NH, P, N, NG, bt):
    hpg = NH // NG
    gw = hpg * P
    for i in range(bt):
        arow = dec_ref[i]
        v_pad = _pad_rows(xdt_ref[i], True)
        k_pad = _pad_rows(b_ref[i], False)
        q_pad = _pad_rows(c_ref[i], False)
        for g in range(NG):
            upd = _dot_tn(v_pad[:, g * gw:(g + 1) * gw], k_pad[:, g * N:(g + 1) * N])
            news = []
            for hl in range(hpg):
                h = g * hpg + hl
                rows = slice(hl * P, (hl + 1) * P)
                new = arow[:, h:h + 1] * s_ref[i, g, rows, :] + upd[rows, :]
                so_ref[i, g, rows, :] = new
                news.append(new.astype(BF16))
            o = _dot_nt(q_pad[:, g * N:(g + 1) * N], jnp.concatenate(news, axis=0))
            o_ref[i, :, g * gw:(g + 1) * gw] = o[0:1]


def _dec_ssd(xdt, bm, cm, dec, states_t, NH, P, N, NG, j, prev):
    B = xdt.shape[0]
    tail = states_t.shape[2:]
    bt = _step_batch(B, BT_SSD)
    return _call_into_stack(
        functools.partial(_dec_ssd_body, NH=NH, P=P, N=N, NG=NG, bt=bt),
        grid=(B // bt,),
        inputs=[_as_steps(xdt), _as_steps(bm), _as_steps(cm), _as_steps(dec), states_t],
        in_specs=[_step_spec(NH * P, bt), _step_spec(NG * N, bt), _step_spec(NG * N, bt),
                  _step_spec(LANES, bt), _layer_state_spec(tail, j, bt)],
        out_specs=[_layer_state_spec(tail, j, bt), _step_spec(NH * P, bt)],
        out_shape=[jax.ShapeDtypeStruct(states_t.shape, F32), jax.ShapeDtypeStruct((B, 1, NH * P), F32)],
        stack_out=0, prev=prev, name="ssd_step")


def _hg_lower_bound(lbl_ref, layer):
    logits = lbl_ref[...]
    e = jnp.exp(logits - jnp.max(logits, axis=0, keepdims=True))
    sm = e / jnp.sum(e, axis=0, keepdims=True)
    lb = jnp.zeros_like(sm[0:1])
    for i in range(1, layer + 1):
        lb = lb + sm[i:i + 1]
    return lb


def _k1_hg_body(x_ref, sh_ref, sc_ref, nw_ref, w_ref, lbl_ref, *out_refs, layer, KD, VD, prompt):
    hb = _norm_mod(x_ref, sh_ref, sc_ref, nw_ref)
    lb = _hg_lower_bound(lbl_ref, layer)
    w = 512
    if prompt:
        q_ref, la_ref, v_ref, g_ref = out_refs
    else:
        q_ref, k_ref, a_ref, v_ref, g_ref = out_refs
    for c in range(KD // w):
        cols = slice(c * w, (c + 1) * w)
        q = _silu(_dot(hb, w_ref[:, c * w:(c + 1) * w])) * (HG_KEY_DIM ** -0.5)
        q_ref[0, :, cols] = q.astype(q_ref.dtype)
        lbc = lb[:, cols]
        f = lbc + (1.0 - lbc) * _sigmoid(_dot(hb, w_ref[:, KD + c * w:KD + (c + 1) * w]))
        if prompt:
            la_ref[0, :, cols] = jnp.log(f)
        else:
            k_ref[0, :, cols] = 1.0 - f
            a_ref[0, :, cols] = f
    for c in range(VD // w):
        cols = slice(c * w, (c + 1) * w)
        v_ref[0, :, cols] = _dot(hb, w_ref[:, 2 * KD + c * w:2 * KD + (c + 1) * w]).astype(v_ref.dtype)
        g_ref[0, :, cols] = _dot(hb, w_ref[:, 2 * KD + VD + c * w:2 * KD + VD + (c + 1) * w]).astype(g_ref.dtype)


def _k1_hg(x3, mod3, nw, w, lbl, tm, layer, KD, VD, prompt):
    G, R, d = x3.shape
    if prompt:
        dts = [(KD, BF16), (KD, F32), (VD, BF16), (VD, BF16)]
    else:
        dts = [(KD, F32), (KD, F32), (KD, F32), (VD, F32), (VD, F32)]
    return pl.pallas_call(
        functools.partial(_k1_hg_body, layer=layer, KD=KD, VD=VD, prompt=prompt),
        grid=(G, R // tm),
        in_specs=_k1_specs(x3, mod3, 0, 1, tm) + [_w_spec(w), _const_spec(lbl.shape)],
        out_specs=[_row_spec(tm, n) for n, _ in dts],
        out_shape=[jax.ShapeDtypeStruct((G, R, n), dt) for n, dt in dts],
        compiler_params=_params(("arbitrary", "arbitrary")),
        name="hg_inproj" if prompt else "hg_inproj_step",
    )(x3, mod3, mod3, nw, w[0], lbl)


def _hg_masks(C):
    i = np.arange(C)
    r, t = i[:, None], i[None, :]
    sels = [t <= r, (t <= r) & (t >= HG_SUB * (r // HG_SUB))]
    pairs = []
    s = C // 2
    while s >= HG_SUB:
        mid = 2 * s * (r // (2 * s)) + s - 1
        sels.append(np.where(r > mid, (t > mid) & (t <= r), (t > r) & (t <= mid)))
        pairs.append((r // (2 * s) == t // (2 * s)) & (r % (2 * s) >= s) & (t % (2 * s) < s))
        s //= 2
    pairs.append((r // HG_SUB == t // HG_SUB) & (t <= r))
    return (jnp.asarray(np.stack(sels), BF16), jnp.asarray(np.stack(pairs), F32))


def _hg_diag_collect(C, K):
    e = np.zeros((HG_SUB, K, C), np.float32)
    for j in range(HG_SUB):
        e[j, :, j::HG_SUB] = 1.0
    return jnp.asarray(e.reshape(HG_SUB * K, C), BF16)


def _k2_hg_body(q_ref, la_ref, v_ref, g_ref, sel_ref, pair_ref, coll_ref, nw_ref, y_ref, so_ref,
                st_scr, k_scr, c_scr, p_scr, *, C, H, K, V, nlev, nc, n_sub):
    ci = pl.program_id(1)

    @pl.when(ci == 0)
    def _():
        st_scr[...] = jnp.zeros_like(st_scr)

    for sub in range(n_sub):
        _hg_chunk(slice(sub * C, (sub + 1) * C), q_ref, la_ref, v_ref, g_ref, sel_ref, pair_ref, coll_ref,
                  nw_ref, y_ref, st_scr, k_scr.at[sub], c_scr.at[sub], p_scr.at[sub],
                  C=C, H=H, K=K, V=V, nlev=nlev)

    @pl.when(ci == nc - 1)
    def _():
        for h in range(H):
            so_ref[h] = st_scr[h].T


def _hg_chunk(rows, q_ref, la_ref, v_ref, g_ref, sel_ref, pair_ref, coll_ref, nw_ref, y_ref,
              st_scr, k_scr, c_scr, p_scr, *, C, H, K, V, nlev):
    D = H * K
    la2 = la_ref[0, rows, :] * LOG2E
    parts = _split3(la2)
    cum2 = _sel_dot(sel_ref[0], parts)
    cw2 = _sel_dot(sel_ref[1], parts)
    kf = 1.0 - jnp.exp2(la2)
    qf = q_ref[0, rows, :].astype(F32)
    vb = v_ref[0, rows, :]
    tot2 = cum2[C - 1:C, :]
    q_in = (qf * jnp.exp2(cum2)).astype(BF16)
    k_out = (kf * jnp.exp2(tot2 - cum2)).astype(BF16)
    e_tot = jnp.exp2(tot2)

    scores = [jnp.zeros((C, C), F32) for _ in range(H)]
    for lev in range(nlev):
        a = jnp.exp2(_sel_dot(sel_ref[2 + lev], parts))
        ql = (qf * a).astype(BF16)
        kl = (kf * a).astype(BF16)
        pm = pair_ref[lev] > 0.5
        for h in range(H):
            sl = slice(h * K, (h + 1) * K)
            scores[h] = scores[h] + jnp.where(pm, _dot_nt(ql[:, sl], kl[:, sl]), 0.0)

    k_scr[...] = kf
    c_scr[...] = cw2
    rowmod = lax.broadcasted_iota(jnp.int32, (C, D), 0) & (HG_SUB - 1)
    nblk = C // HG_SUB
    for j in range(HG_SUB):
        kb = jnp.concatenate([jnp.broadcast_to(k_scr[b * HG_SUB + j:b * HG_SUB + j + 1, :], (HG_SUB, D))
                              for b in range(nblk)], axis=0)
        cb = jnp.concatenate([jnp.broadcast_to(c_scr[b * HG_SUB + j:b * HG_SUB + j + 1, :], (HG_SUB, D))
                              for b in range(nblk)], axis=0)
        dec = jnp.exp2(jnp.where(rowmod >= j, cw2 - cb, MASKED_EXPONENT))
        p = (qf * kb * dec).astype(BF16)
        for h in range(H):
            p_scr[h * C:(h + 1) * C, j * K:(j + 1) * K] = p[:, h * K:(h + 1) * K]
    diag = _dot(p_scr[...], coll_ref[...])
    dm = pair_ref[nlev] > 0.5

    nw = nw_ref[...]
    for h in range(H):
        sl = slice(h * K, (h + 1) * K)
        vsl = slice(h * V, (h + 1) * V)
        st = st_scr[h]
        sc = scores[h] + jnp.where(dm, diag[h * C:(h + 1) * C, :], 0.0)
        o = _dot(sc.astype(BF16), vb[:, vsl]) + _dot_nt(q_in[:, sl], st.astype(BF16))
        st_scr[h] = st * e_tot[:, sl] + _dot_tn(vb[:, vsl], k_out[:, sl])
        gh = g_ref[0, rows, vsl].astype(F32)
        y_ref[0, rows, vsl] = (_rms(o) * nw[:, vsl] * _silu(gh)).astype(y_ref.dtype)


def _k2_hg(q, la, v, g, nw_rep, H, K, V, nl, j, prev):
    G, R, D = q.shape
    C = min(C_HG, R)
    n_sub = HG_CHUNKS_PER_STEP if R % (C * HG_CHUNKS_PER_STEP) == 0 else 1
    nc = R // (C * n_sub)
    sels, pairs = _hg_masks(C)
    nlev = pairs.shape[0] - 1
    coll = _hg_diag_collect(C, K)
    return _call_into_stack(
        functools.partial(_k2_hg_body, C=C, H=H, K=K, V=V, nlev=nlev, nc=nc, n_sub=n_sub),
        grid=(G, nc),
        inputs=[q, la, v, g, sels, pairs, coll, nw_rep],
        in_specs=[_row_spec(C * n_sub, D), _row_spec(C * n_sub, D), _row_spec(C * n_sub, H * V),
                  _row_spec(C * n_sub, H * V), _const_spec(sels.shape), _const_spec(pairs.shape),
                  _const_spec(coll.shape), _const_spec(nw_rep.shape)],
        out_specs=[_row_spec(C * n_sub, H * V), _layer_state_spec((H, K, V), j)],
        out_shape=[jax.ShapeDtypeStruct((G, R, H * V), BF16), jax.ShapeDtypeStruct((nl, G, H, K, V), F32)],
        scratch_shapes=[pltpu.VMEM((H, V, K), F32), pltpu.VMEM((n_sub, C, D), F32),
                        pltpu.VMEM((n_sub, C, D), F32), pltpu.VMEM((n_sub, H * C, HG_SUB * K), BF16)],
        stack_out=1, prev=prev, name="hg_core")


def _dec_hg_body(q_ref, k_ref, a_ref, v_ref, s_ref, so_ref, o_ref, *, H, K, V, bt):
    eye = lax.broadcasted_iota(jnp.int32, (K, K), 0) == lax.broadcasted_iota(jnp.int32, (K, K), 1)
    for i in range(bt):
        arow = a_ref[i]
        q_pad = _pad_rows(q_ref[i], False)
        k_pad = _pad_rows(k_ref[i], True)
        v_pad = _pad_rows(v_ref[i], False)
        for h in range(H):
            sl = slice(h * K, (h + 1) * K)
            vsl = slice(h * V, (h + 1) * V)
            ac = _col(arow[:, sl], eye)
            new = ac * s_ref[i, h] + _dot_tn(k_pad[:, sl], v_pad[:, vsl])
            so_ref[i, h] = new
            o_ref[i, :, vsl] = _dot(q_pad[:, sl], new.astype(BF16))[0:1]


def _dec_hg(q, k, a, v, states, H, K, V, j, prev):
    B = q.shape[0]
    bt = _step_batch(B, BT_HG)
    return _call_into_stack(
        functools.partial(_dec_hg_body, H=H, K=K, V=V, bt=bt),
        grid=(B // bt,),
        inputs=[_as_steps(q), _as_steps(k), _as_steps(a), _as_steps(v), states],
        in_specs=[_step_spec(H * K, bt), _step_spec(H * K, bt), _step_spec(H * K, bt), _step_spec(H * V, bt),
                  _layer_state_spec((H, K, V), j, bt)],
        out_specs=[_layer_state_spec((H, K, V), j, bt), _step_spec(H * V, bt)],
        out_shape=[jax.ShapeDtypeStruct(states.shape, F32), jax.ShapeDtypeStruct((B, 1, H * V), F32)],
        stack_out=0, prev=prev, name="hg_step")


def _f32_tile(ref):
    return ref[0].astype(F32)


def _pro_identity(y_ref):
    return y_ref[0]


def _pro_ret(o_ref, g_ref, *, H, V):
    return jnp.concatenate(_ret_gate(_f32_tile(o_ref), _f32_tile(g_ref), H, V), axis=-1).astype(BF16)


def _pro_ssd(o_ref, xs_ref, z_ref, drep_ref, nw_ref, *, NG):
    return jnp.concatenate(_ssd_gate(_f32_tile(o_ref), _f32_tile(xs_ref), _f32_tile(z_ref), drep_ref[...],
                                     nw_ref[...], NG), axis=-1).astype(BF16)


def _pro_hg(o_ref, g_ref, nw_ref, *, H, V):
    return jnp.concatenate(_hg_gate(_f32_tile(o_ref), _f32_tile(g_ref), nw_ref[...], H, V),
                           axis=-1).astype(BF16)


def _swiglu(hb, win_ref, wout_ref, F, bounds):
    acc = None
    for lo, hi in bounds:
        gt = _dot(hb, win_ref[:, lo:hi])
        up = _dot(hb, win_ref[:, F + lo:F + hi])
        a = (_silu(gt) * up).astype(BF16)
        part = _dot(a, wout_ref[lo:hi, :])
        acc = part if acc is None else acc + part
    return acc


def _k34_body(*refs, prologue, n_tile, n_const, F, bounds):
    n_pro = n_tile + n_const
    (wo_ref, x_ref, gm_ref, nwm_ref, sh_ref, sc_ref, nwf_ref, gf_ref, nwp_ref,
     win_ref, wout_ref, o_ref) = refs[n_pro:]
    y = prologue(*refs[:n_pro])
    x = x_ref[0] + gm_ref[0] * (_rms(_dot(y, wo_ref[...])) * nwm_ref[...])
    hb = (_rms(x) * nwf_ref[...] * (1.0 + sc_ref[0]) + sh_ref[0]).astype(BF16)
    acc = _swiglu(hb, win_ref, wout_ref, F, bounds)
    o_ref[0] = x + gf_ref[0] * (_rms(acc) * nwp_ref[...])


def _k34(prologue, tile_args, const_args, w_out, x3, mod3, nw_mix_post, nw_pre, nw_post, win, wout, tm):
    G, R, d = x3.shape
    rm = mod3.shape[1]
    F = wout[0].shape[1]
    mod_spec = lambda idx: pl.BlockSpec((1, rm, d), lambda g, t: (g, 0, idx))
    row = _const_spec((1, d))
    return pl.pallas_call(
        functools.partial(_k34_body, prologue=prologue, n_tile=len(tile_args), n_const=len(const_args),
                          F=F, bounds=_ffn_bounds(F)),
        grid=(G, R // tm),
        in_specs=[_row_spec(tm, a.shape[-1]) for a in tile_args] + [_const_spec(c.shape) for c in const_args]
        + [_w_spec(w_out), _row_spec(tm, d), mod_spec(2), row, mod_spec(3), mod_spec(4), row,
           mod_spec(5), row, _w_spec(win), _w_spec(wout)],
        out_specs=_row_spec(tm, d),
        out_shape=jax.ShapeDtypeStruct((G, R, d), F32),
        compiler_params=_params(("arbitrary", "arbitrary")),
        name="outproj_ffn",
    )(*tile_args, *const_args, w_out[0], x3, mod3, nw_mix_post, mod3, mod3, nw_pre, mod3, nw_post, win[0],
      wout[0])


def _ffn_bounds(F):
    tiles, rem = divmod(F, MXU_TILE)
    assert rem == 0, F
    n = -(-tiles // FFN_MAX_CHUNK_TILES)
    sizes = [tiles // n + (1 if i < tiles % n else 0) for i in range(n)]
    edges = np.cumsum([0] + sizes) * MXU_TILE
    return tuple((int(a), int(b)) for a, b in zip(edges[:-1], edges[1:]))


def _rope_table(pos, half):
    inv = 1.0 / (ROPE_BASE ** (jnp.arange(half, dtype=F32) / half))
    ang = pos.astype(F32)[:, None] * inv[None, :]
    return jnp.cos(ang), jnp.sin(ang)


def kernel(x_prompt, x_sample, c_prompt, c_sample, state_ret, state_ssd, state_conv, state_hgrn, w_ada, b_ada, norm_mix_pre, norm_mix_post, norm_ffn_pre, norm_ffn_post, ret_w_in, ret_w_out, ssd_w_in, ssd_conv_w, ssd_conv_b, ssd_dt_bias, ssd_a_log, ssd_d, ssd_norm, ssd_w_out, hg_w_in, hg_lb_logits, hg_norm, hg_w_out, ffn_w_in, ffn_w_out):
    bp, seq, d = x_prompt.shape
    bs = x_sample.shape[0]
    depth = w_ada.shape[0]
    ret_h = d // 256
    ret_k = d // ret_h
    ret_v = 2 * ret_k
    di = 2 * d
    ssd_nh = di // SSD_HEAD_DIM
    gn = SSD_GROUPS * SSD_STATE
    hg_h = d // HG_KEY_DIM
    hg_v = d // hg_h
    kd = hg_h * HG_KEY_DIM
    vd = hg_h * hg_v
    tm = min(TM_PROMPT, seq)

    mod = _adaln(jnp.concatenate([c_prompt, c_sample], axis=0), w_ada, b_ada)
    xp = x_prompt.astype(F32)
    xs = x_sample.astype(F32).reshape(1, bs, d)
    cos_p, sin_p = _rope_table(jnp.arange(seq, dtype=jnp.int32), ret_k // 2)
    cos_s, sin_s = _rope_table(jnp.full((1,), PAST_LEN, jnp.int32), ret_k // 2)

    hpg = ssd_nh // SSD_GROUPS
    ssd_states_t = jnp.swapaxes(state_ssd.astype(F32), 3, 4).reshape(
        state_ssd.shape[0], bs, SSD_GROUPS, hpg * SSD_HEAD_DIM, SSD_STATE)

    def ssd_untranspose(s):
        s = s.reshape(s.shape[0], s.shape[1], ssd_nh, SSD_HEAD_DIM, SSD_STATE)
        return jnp.swapaxes(s, 3, 4).astype(state_ssd.dtype)

    wb = {name: w.astype(BF16) for name, w in (
        ("ret_in", ret_w_in), ("ret_out", ret_w_out), ("ssd_in", ssd_w_in), ("ssd_out", ssd_w_out),
        ("hg_in", hg_w_in), ("hg_out", hg_w_out), ("ffn_in", ffn_w_in), ("ffn_out", ffn_w_out))}

    new = {k: [] for k in ("conv_p", "conv_s")}
    stk = {k: None for k in ("ret_p", "ret_s", "ssd_p", "ssd_s", "hg_p", "hg_s")}
    n_kind = [sum(1 for l in range(depth) if l % 3 == kind) for kind in range(3)]
    counts = [0, 0, 0]
    for layer in range(depth):
        mod_p = mod[layer, :bp].reshape(bp, 1, 6 * d)
        mod_s = mod[layer, bp:].reshape(1, bs, 6 * d)
        nw_pre = norm_mix_pre[layer].reshape(1, d)
        nw_post = norm_mix_post[layer].reshape(1, d)
        kind = layer % 3
        j = counts[kind]
        counts[kind] += 1
        if kind == 0:
            w_in = (wb["ret_in"], j)
            w_out = (wb["ret_out"], j)
            prologue = functools.partial(_pro_ret, H=ret_h, V=ret_v)
            q, k, v, g = _k1_ret(xp, mod_p, nw_pre, w_in, cos_p, sin_p, tm, BF16, ret_h, ret_k, ret_v)
            y, stk["ret_p"] = _k2_ret(q, k, v, g, ret_h, ret_k, ret_v, n_kind[0], j, stk["ret_p"])
            q, k, v, g = _k1_ret(xs, mod_s, nw_pre, w_in, cos_s, sin_s, bs, F32, ret_h, ret_k, ret_v)
            stk["ret_s"], o = _dec_ret(q[0], k[0], v[0], state_ret, ret_h, ret_k, ret_v, j, stk["ret_s"])
            pro_s = ([o.reshape(1, bs, -1), g], [])
        elif kind == 1:
            w_main = (wb["ssd_in"], j)
            w_dt = jnp.pad(ssd_w_in[j][:, di + di + 2 * gn:], ((0, 0), (0, LANES - ssd_nh))).astype(BF16)
            w_out = (wb["ssd_out"], j)
            cw = ssd_conv_w[j]
            cb = ssd_conv_b[j].reshape(1, -1)
            dtb = jnp.pad(ssd_dt_bias[j], (0, LANES - ssd_nh)).reshape(1, LANES)
            alog = jnp.pad(ssd_a_log[j].astype(F32), (0, LANES - ssd_nh)).reshape(1, LANES)
            drep = jnp.repeat(ssd_d[j], SSD_HEAD_DIM).reshape(1, di)
            nw_ssd = ssd_norm[j].reshape(1, di)
            z, xc, bm, cm, dt, nconv = _k1_ssd_prompt(xp, mod_p, nw_pre, w_main, w_dt, cw, cb, dtb, tm, di, gn)
            new["conv_p"].append(nconv)
            expand = _ssd_expand(ssd_nh, SSD_HEAD_DIM)
            prologue = functools.partial(_pro_ssd, NG=SSD_GROUPS)
            y, stk["ssd_p"] = _k2_ssd(z, xc, bm, cm, dt, alog, expand, drep, nw_ssd, ssd_nh, SSD_HEAD_DIM,
                                      SSD_STATE, SSD_GROUPS, n_kind[1], j, stk["ssd_p"])
            buf_t = jnp.transpose(state_conv[j].astype(F32), (1, 0, 2))
            z, xc, xdt, bm, cm, dec, nconv = _k1_ssd_sample(xs, mod_s, nw_pre, w_main, w_dt, cw, cb, dtb, buf_t,
                                                            alog, expand, di, gn)
            new["conv_s"].append(jnp.transpose(nconv, (1, 0, 2)))
            stk["ssd_s"], o = _dec_ssd(xdt[0], bm[0], cm[0], dec[0], ssd_states_t,
                                       ssd_nh, SSD_HEAD_DIM, SSD_STATE, SSD_GROUPS, j, stk["ssd_s"])
            pro_s = ([o.reshape(1, bs, -1), xc, z], [drep, nw_ssd])
        else:
            w_in = (wb["hg_in"], j)
            w_out = (wb["hg_out"], j)
            nw_hg = jnp.tile(hg_norm[j], hg_h).reshape(1, vd)
            lbl = hg_lb_logits.astype(F32)
            prologue = functools.partial(_pro_hg, H=hg_h, V=hg_v)
            q, la, v, g = _k1_hg(xp, mod_p, nw_pre, w_in, lbl, tm, layer, kd, vd, True)
            y, stk["hg_p"] = _k2_hg(q, la, v, g, nw_hg, hg_h, HG_KEY_DIM, hg_v, n_kind[2], j, stk["hg_p"])
            q, k, a, v, g = _k1_hg(xs, mod_s, nw_pre, w_in, lbl, bs, layer, kd, vd, False)
            stk["hg_s"], o = _dec_hg(q[0], k[0], a[0], v[0], state_hgrn, hg_h, HG_KEY_DIM, hg_v, j, stk["hg_s"])
            pro_s = ([o.reshape(1, bs, -1), g], [nw_hg])
        nf_pre = norm_ffn_pre[layer].reshape(1, d)
        nf_post = norm_ffn_post[layer].reshape(1, d)
        f_in = (wb["ffn_in"], layer)
        f_out = (wb["ffn_out"], layer)
        xp = _k34(_pro_identity, [y], [], w_out, xp, mod_p, nw_post, nf_pre, nf_post, f_in, f_out, tm)
        xs = _k34(prologue, *pro_s, w_out, xs, mod_s, nw_post, nf_pre, nf_post, f_in, f_out, bs)

    stack = lambda name, like: jnp.stack(new[name]).astype(like.dtype)
    return (xp.astype(x_prompt.dtype), xs.reshape(bs, 1, d).astype(x_sample.dtype),
            stk["ret_p"].astype(state_ret.dtype), stk["ret_s"].astype(state_ret.dtype),
            ssd_untranspose(stk["ssd_p"]), ssd_untranspose(stk["ssd_s"]),
            stack("conv_p", state_conv), stack("conv_s", state_conv),
            stk["hg_p"].astype(state_hgrn.dtype), stk["hg_s"].astype(state_hgrn.dtype))
```

```python
import functools

import numpy as np
import jax
import jax.numpy as jnp
from jax import lax
from jax.experimental import pallas as pl
from jax.experimental.pallas import tpu as pltpu

F32 = jnp.float32
BF16 = jnp.bfloat16
EPS = 1e-6
PAST_LEN = 16384
ROPE_BASE = 10000.0
SSD_HEAD_DIM = 64
SSD_GROUPS = 4
SSD_STATE = 128
SSD_CONV = 4
HG_KEY_DIM = 128
LANES = 128
SUBLANES = 8
VMEM_LIMIT = 56 * 1024 * 1024
TM_PROMPT = 512
MXU_TILE = 256
MXU_PAD_ROWS = 16
C_RET = 256
C_SSD = 128
C_HG = 128
HG_SUB = 8
SSD_CHUNKS_PER_STEP = 4
HG_CHUNKS_PER_STEP = 1
RET_CHUNKS_PER_STEP = 2
FFN_MAX_CHUNK_TILES = 6
LOG2E = 1.4426950408889634
MASKED_EXPONENT = -1e30
BT_RET = 4
BT_SSD = 4
BT_HG = 8


def _rms(x):
    return x * lax.rsqrt(jnp.mean(x * x, axis=-1, keepdims=True) + EPS)


def _silu(x):
    h = 0.5 * x
    return h + h * jnp.tanh(h)


def _sigmoid(x):
    return 0.5 + 0.5 * jnp.tanh(0.5 * x)


def _softplus(x):
    return jnp.maximum(x, 0.0) + jnp.log1p(jnp.exp(-jnp.abs(x)))


def _dot(a, b):
    return jnp.dot(a, b, preferred_element_type=F32)


def _dot_nt(a, b):
    return lax.dot_general(a, b, (((1,), (1,)), ((), ())), preferred_element_type=F32)


def _dot_tn(a, b):
    return lax.dot_general(a, b, (((0,), (0,)), ((), ())), preferred_element_type=F32)


def _split3(x):
    hi = x.astype(BF16)
    r = x - hi.astype(F32)
    mid = r.astype(BF16)
    lo = (r - mid.astype(F32)).astype(BF16)
    return hi, mid, lo


def _sel_dot(sel, parts):
    acc = _dot(sel, parts[0])
    for p in parts[1:]:
        acc = acc + _dot(sel, p)
    return acc


def _dot_sel(parts, sel):
    acc = _dot(parts[0], sel)
    for p in parts[1:]:
        acc = acc + _dot(p, sel)
    return acc


def _col(row, eye):
    return jnp.sum(jnp.where(eye, row, 0.0), axis=1, keepdims=True)


def _norm_mod(x_ref, sh_ref, sc_ref, nw_ref):
    x = x_ref[0]
    h = _rms(x) * nw_ref[...] * (1.0 + sc_ref[0]) + sh_ref[0]
    return h.astype(BF16)


def _params(sem):
    return pltpu.CompilerParams(dimension_semantics=sem, vmem_limit_bytes=VMEM_LIMIT)


def _const_spec(shape):
    nd = len(shape)
    return pl.BlockSpec(shape, lambda *_: (0,) * nd, pipeline_mode=pl.Buffered(1))


def _w_spec(wl):
    stack, j = wl
    return pl.BlockSpec((None,) + stack.shape[1:], lambda *_: (j, 0, 0), pipeline_mode=pl.Buffered(1))


def _ret_gate(o, g, H, V):
    outs = []
    for h in range(H):
        outs.append(_rms(o[:, h * V:(h + 1) * V]) * _silu(g[:, h * V:(h + 1) * V]))
    return outs


def _ssd_gate(y, xs, z, d_rep, nw, groups):
    y = (y + d_rep * xs) * _silu(z)
    w = y.shape[-1] // groups
    outs = []
    for g in range(groups):
        outs.append(_rms(y[:, g * w:(g + 1) * w]) * nw[:, g * w:(g + 1) * w])
    return outs


def _hg_gate(o, g, nw, H, V):
    outs = []
    for h in range(H):
        sl = slice(h * V, (h + 1) * V)
        outs.append(_rms(o[:, sl]) * nw[:, sl] * _silu(g[:, sl]))
    return outs


def _adaln_body(c_ref, w_ref, b_ref, o_ref):
    a = _silu(c_ref[...]).astype(BF16)
    o_ref[0] = _dot(a, w_ref[0].astype(BF16)) + b_ref[0]


def _adaln(c_all, w_ada, b_ada):
    depth, d, n = w_ada.shape
    bc = c_all.shape[0]
    tn = 1024
    return pl.pallas_call(
        _adaln_body,
        grid=(depth, n // tn),
        in_specs=[
            pl.BlockSpec((bc, d), lambda l, j: (0, 0)),
            pl.BlockSpec((1, d, tn), lambda l, j: (l, 0, j)),
            pl.BlockSpec((1, 1, tn), lambda l, j: (l, 0, j)),
        ],
        out_specs=pl.BlockSpec((1, bc, tn), lambda l, j: (l, 0, j)),
        out_shape=jax.ShapeDtypeStruct((depth, bc, n), F32),
        compiler_params=_params(("arbitrary", "arbitrary")),
        name="adaln",
    )(c_all, w_ada, b_ada.reshape(depth, 1, n))


def _k1_specs(x3, mod3, sh_idx, sc_idx, tm):
    _, _, d = x3.shape
    rm = mod3.shape[1]
    return [
        pl.BlockSpec((1, tm, d), lambda g, t: (g, t, 0)),
        pl.BlockSpec((1, rm, d), lambda g, t: (g, 0, sh_idx)),
        pl.BlockSpec((1, rm, d), lambda g, t: (g, 0, sc_idx)),
        _const_spec((1, d)),
    ]


def _row_spec(tm, n):
    return pl.BlockSpec((1, tm, n), lambda g, t: (g, t, 0))


def _layer_state_spec(tail, j, bt=None):
    nd = len(tail)
    return pl.BlockSpec((None, bt) + tuple(tail), lambda b, *_: (j, b) + (0,) * nd)


def _call_into_stack(body, *, grid, inputs, in_specs, out_specs, out_shape, stack_out, prev, name,
                     scratch_shapes=()):
    inputs, in_specs = list(inputs), list(in_specs)
    n_in = len(inputs)
    aliases = {}
    kernel_fn = body
    if prev is not None:
        inputs.append(prev)
        in_specs.append(pl.BlockSpec(memory_space=pl.ANY))
        aliases = {n_in: stack_out}

        def kernel_fn(*refs):
            body(*refs[:n_in], *refs[n_in + 1:])

    return pl.pallas_call(
        kernel_fn, grid=grid, in_specs=in_specs, out_specs=out_specs, out_shape=out_shape,
        scratch_shapes=scratch_shapes, input_output_aliases=aliases,
        compiler_params=_params(("arbitrary",) * len(grid)), name=name)(*inputs)


def _k1_ret_body(x_ref, sh_ref, sc_ref, nw_ref, w_ref, cos_ref, sin_ref,
                 q_ref, k_ref, v_ref, g_ref, *, H, K, V):
    hb = _norm_mod(x_ref, sh_ref, sc_ref, nw_ref)
    cos = cos_ref[...]
    sin = sin_ref[...]
    half = K // 2
    for h in range(H):
        for dst, base, scale in ((q_ref, 0, None), (k_ref, H * K, K ** -0.5)):
            a = _dot(hb, w_ref[:, base + h * K:base + (h + 1) * K])
            x1 = a[:, :half]
            x2 = a[:, half:]
            r1 = x1 * cos - x2 * sin
            r2 = x2 * cos + x1 * sin
            if scale is not None:
                r1 = r1 * scale
                r2 = r2 * scale
            dst[0, :, h * K:h * K + half] = r1.astype(dst.dtype)
            dst[0, :, h * K + half:(h + 1) * K] = r2.astype(dst.dtype)
    vb = 2 * H * K
    for c in range(H):
        v_ref[0, :, c * V:(c + 1) * V] = _dot(hb, w_ref[:, vb + c * V:vb + (c + 1) * V]).astype(v_ref.dtype)
        g_ref[0, :, c * V:(c + 1) * V] = _dot(
            hb, w_ref[:, vb + H * V + c * V:vb + H * V + (c + 1) * V]).astype(g_ref.dtype)


def _k1_ret(x3, mod3, nw, w, cos, sin, tm, out_dtype, H, K, V):
    G, R, d = x3.shape
    rc = cos.shape[0]
    cs_spec = (pl.BlockSpec((tm, K // 2), lambda g, t: (t, 0)) if rc == R
               else _const_spec((1, K // 2)))
    outs = [jax.ShapeDtypeStruct((G, R, H * K), out_dtype), jax.ShapeDtypeStruct((G, R, H * K), out_dtype),
            jax.ShapeDtypeStruct((G, R, H * V), out_dtype), jax.ShapeDtypeStruct((G, R, H * V), out_dtype)]
    return pl.pallas_call(
        functools.partial(_k1_ret_body, H=H, K=K, V=V),
        grid=(G, R // tm),
        in_specs=_k1_specs(x3, mod3, 0, 1, tm) + [_w_spec(w), cs_spec, cs_spec],
        out_specs=[_row_spec(tm, H * K), _row_spec(tm, H * K), _row_spec(tm, H * V), _row_spec(tm, H * V)],
        out_shape=outs,
        compiler_params=_params(("arbitrary", "arbitrary")),
        name="ret_inproj",
    )(x3, mod3, mod3, nw, w[0], cos, sin)


def _k2_ret_body(q_ref, k_ref, v_ref, g_ref, dmat_ref, qdec_ref, kdec_ref, y_ref, s_ref,
                 *, H, K, V, C, sdec, n_sub):
    @pl.when(pl.program_id(1) == 0)
    def _():
        s_ref[...] = jnp.zeros_like(s_ref)

    for sub in range(n_sub):
        rows = slice(sub * C, (sub + 1) * C)
        for h in range(H):
            qh = q_ref[0, rows, h * K:(h + 1) * K]
            kh = k_ref[0, rows, h * K:(h + 1) * K]
            vh = v_ref[0, rows, h * V:(h + 1) * V]
            state = s_ref[h]
            scores = _dot_nt(qh, kh) * dmat_ref[h]
            o = _dot(scores.astype(BF16), vh) + _dot(qh, state.astype(BF16)) * qdec_ref[h]
            k_out = (kh.astype(F32) * kdec_ref[h]).astype(BF16)
            s_ref[h] = sdec[h] * state + _dot_tn(k_out, vh)
            gh = g_ref[0, rows, h * V:(h + 1) * V].astype(F32)
            y_ref[0, rows, h * V:(h + 1) * V] = (_rms(o) * _silu(gh)).astype(y_ref.dtype)


def _ret_tables(H, K, V, C):
    lg = np.log1p(-np.exp2(-5.0 - np.arange(H, dtype=np.float64)))
    i = np.arange(C, dtype=np.float64)
    diff = i[:, None] - i[None, :]
    dmat = np.where(diff >= 0, np.exp(lg[:, None, None] * np.maximum(diff, 0.0)), 0.0)
    qdec = np.broadcast_to(np.exp(lg[:, None, None] * (i[None, :, None] + 1.0)), (H, C, V))
    kdec = np.broadcast_to(np.exp(lg[:, None, None] * (C - 1.0 - i[None, :, None])), (H, C, K))
    sdec = tuple(float(np.exp(l * C)) for l in lg)
    return (jnp.asarray(dmat, F32), jnp.asarray(qdec, F32), jnp.asarray(kdec, F32), sdec)


def _k2_ret(q, k, v, g, H, K, V, nl, j, prev):
    G, R, _ = q.shape
    C = min(C_RET, R)
    n_sub = RET_CHUNKS_PER_STEP if R % (C * RET_CHUNKS_PER_STEP) == 0 else 1
    rows = C * n_sub
    dmat, qdec, kdec, sdec = _ret_tables(H, K, V, C)
    return _call_into_stack(
        functools.partial(_k2_ret_body, H=H, K=K, V=V, C=C, sdec=sdec, n_sub=n_sub),
        grid=(G, R // rows),
        inputs=[q, k, v, g, dmat, qdec, kdec],
        in_specs=[_row_spec(rows, H * K), _row_spec(rows, H * K), _row_spec(rows, H * V),
                  _row_spec(rows, H * V), _const_spec(dmat.shape), _const_spec(qdec.shape),
                  _const_spec(kdec.shape)],
        out_specs=[_row_spec(rows, H * V), _layer_state_spec((H, K, V), j)],
        out_shape=[jax.ShapeDtypeStruct((G, R, H * V), BF16), jax.ShapeDtypeStruct((nl, G, H, K, V), F32)],
        stack_out=1, prev=prev, name="ret_core")


def _dec_ret_body(q_ref, k_ref, v_ref, s_ref, so_ref, o_ref, *, H, K, V, gammas, bt):
    eye = lax.broadcasted_iota(jnp.int32, (K, K), 0) == lax.broadcasted_iota(jnp.int32, (K, K), 1)
    for i in range(bt):
        qrow = _sample_row(q_ref, i, bt)
        krow = _sample_row(k_ref, i, bt)
        vrow = _sample_row(v_ref, i, bt)
        for h in range(H):
            qc = _col(qrow[:, h * K:(h + 1) * K], eye)
            kc = _col(krow[:, h * K:(h + 1) * K], eye)
            new = gammas[h] * s_ref[i, h] + kc * vrow[:, h * V:(h + 1) * V]
            so_ref[i, h] = new
            o_ref[i, :, h * V:(h + 1) * V] = jnp.sum(qc * new, axis=0, keepdims=True)


def _step_spec(n, bt=1):
    return pl.BlockSpec((bt, 1, n), lambda b: (b, 0, 0))


def _rows_spec(x):
    return pl.BlockSpec((None,) + x.shape[1:], lambda b: (0, 0, 0), pipeline_mode=pl.Buffered(1))


def _sample_row(ref, i, bt):
    return ref[pl.ds(pl.program_id(0) * bt + i, 1), :]


def _step_batch(B, want):
    return want if B % want == 0 else 1


def _dec_ret(q, k, v, states, H, K, V, j, prev):
    B = q.shape[1]
    gammas = tuple(float(1.0 - 2.0 ** (-5 - h)) for h in range(H))
    bt = _step_batch(B, BT_RET)
    return _call_into_stack(
        functools.partial(_dec_ret_body, H=H, K=K, V=V, gammas=gammas, bt=bt),
        grid=(B // bt,),
        inputs=[q, k, v, states],
        in_specs=[_rows_spec(q), _rows_spec(k), _rows_spec(v), _layer_state_spec((H, K, V), j, bt)],
        out_specs=[_layer_state_spec((H, K, V), j, bt), _step_spec(H * V, bt)],
        out_shape=[jax.ShapeDtypeStruct(states.shape, F32), jax.ShapeDtypeStruct((B, 1, H * V), F32)],
        stack_out=0, prev=prev, name="ret_step")


def _ssd_route(c, DI, xs_ref, b_ref, c_ref):
    w = 512
    if c * w < DI:
        return xs_ref, c * w
    if c * w < DI + b_ref.shape[-1]:
        return b_ref, c * w - DI
    return c_ref, c * w - DI - b_ref.shape[-1]


def _k1_ssd_prompt_body(x_ref, sh_ref, sc_ref, nw_ref, w_ref, wdt_ref, cw_ref, cb_ref, dtb_ref,
                        z_ref, xs_ref, b_ref, c_ref, dt_ref, nconv_ref, cbuf, *, tm, DI, CD, nt):
    t = pl.program_id(1)
    hb = _norm_mod(x_ref, sh_ref, sc_ref, nw_ref)
    w = 512
    for c in range(DI // w):
        z_ref[0, :, c * w:(c + 1) * w] = _dot(hb, w_ref[:, c * w:(c + 1) * w]).astype(z_ref.dtype)
    dt_ref[0] = _softplus(_dot(hb, wdt_ref[...]) + dtb_ref[...])

    @pl.when(t == 0)
    def _():
        cbuf[0:SUBLANES, :] = jnp.zeros((SUBLANES, CD), F32)

    for c in range(CD // w):
        cols = slice(c * w, (c + 1) * w)
        raw = _dot(hb, w_ref[:, DI + c * w:DI + (c + 1) * w])
        cbuf[SUBLANES:SUBLANES + tm, cols] = raw
        conv = cb_ref[:, cols] + cw_ref[3:4, cols] * raw
        for k in range(SSD_CONV - 1):
            conv = conv + cw_ref[k:k + 1, cols] * cbuf[SUBLANES - 3 + k:SUBLANES - 3 + k + tm, cols]
        dst, off = _ssd_route(c, DI, xs_ref, b_ref, c_ref)
        dst[0, :, off:off + w] = _silu(conv).astype(dst.dtype)

    @pl.when(t == nt - 1)
    def _():
        nconv_ref[0] = cbuf[tm + SUBLANES - 3:tm + SUBLANES, :]

    cbuf[0:SUBLANES, :] = cbuf[tm:tm + SUBLANES, :]


def _k1_ssd_prompt(x3, mod3, nw, w, wdt, cw, cb, dtb, tm, DI, GN):
    G, R, d = x3.shape
    CD = DI + 2 * GN
    nt = R // tm
    outs = [jax.ShapeDtypeStruct((G, R, DI), BF16), jax.ShapeDtypeStruct((G, R, DI), BF16),
            jax.ShapeDtypeStruct((G, R, GN), BF16), jax.ShapeDtypeStruct((G, R, GN), BF16),
            jax.ShapeDtypeStruct((G, R, LANES), F32), jax.ShapeDtypeStruct((G, SSD_CONV - 1, CD), F32)]
    return pl.pallas_call(
        functools.partial(_k1_ssd_prompt_body, tm=tm, DI=DI, CD=CD, nt=nt),
        grid=(G, nt),
        in_specs=_k1_specs(x3, mod3, 0, 1, tm) + [_w_spec(w), _const_spec(wdt.shape),
                                                  _const_spec(cw.shape), _const_spec(cb.shape),
                                                  _const_spec(dtb.shape)],
        out_specs=[_row_spec(tm, DI), _row_spec(tm, DI), _row_spec(tm, GN), _row_spec(tm, GN),
                   _row_spec(tm, LANES), pl.BlockSpec((1, SSD_CONV - 1, CD), lambda g, t: (g, 0, 0))],
        out_shape=outs,
        scratch_shapes=[pltpu.VMEM((tm + SUBLANES, CD), F32)],
        compiler_params=_params(("arbitrary", "arbitrary")),
        name="ssd_inproj",
    )(x3, mod3, mod3, nw, w[0], wdt, cw, cb, dtb)


def _k1_ssd_sample_body(x_ref, sh_ref, sc_ref, nw_ref, w_ref, wdt_ref, cw_ref, cb_ref, dtb_ref, buf_ref,
                        alog_ref, exp_ref, z_ref, xs_ref, xdt_ref, b_ref, c_ref, dec_ref, nconv_ref,
                        *, DI, CD):
    hb = _norm_mod(x_ref, sh_ref, sc_ref, nw_ref)
    w = 512
    for c in range(DI // w):
        z_ref[0, :, c * w:(c + 1) * w] = _dot(hb, w_ref[:, c * w:(c + 1) * w])
    dt = _softplus(_dot(hb, wdt_ref[...]) + dtb_ref[...])
    dec_ref[0] = jnp.exp(dt * (-jnp.exp(alog_ref[...])))
    dt_rep = _dot_sel(_split3(dt), exp_ref[...])
    for c in range(CD // w):
        cols = slice(c * w, (c + 1) * w)
        raw = _dot(hb, w_ref[:, DI + c * w:DI + (c + 1) * w])
        conv = cb_ref[:, cols] + cw_ref[3:4, cols] * raw
        for k in range(SSD_CONV - 1):
            conv = conv + cw_ref[k:k + 1, cols] * buf_ref[k, :, cols]
        act = _silu(conv)
        dst, off = _ssd_route(c, DI, xs_ref, b_ref, c_ref)
        dst[0, :, off:off + w] = act
        if dst is xs_ref:
            xdt_ref[0, :, cols] = act * dt_rep[:, cols]
        nconv_ref[0, :, cols] = buf_ref[1, :, cols]
        nconv_ref[1, :, cols] = buf_ref[2, :, cols]
        nconv_ref[2, :, cols] = raw


def _k1_ssd_sample(x3, mod3, nw, w, wdt, cw, cb, dtb, buf_t, alog_row, expand, DI, GN):
    G, R, d = x3.shape
    CD = DI + 2 * GN
    tm = R
    outs = [jax.ShapeDtypeStruct((G, R, DI), F32), jax.ShapeDtypeStruct((G, R, DI), F32),
            jax.ShapeDtypeStruct((G, R, DI), F32),
            jax.ShapeDtypeStruct((G, R, GN), F32), jax.ShapeDtypeStruct((G, R, GN), F32),
            jax.ShapeDtypeStruct((G, R, LANES), F32), jax.ShapeDtypeStruct((SSD_CONV - 1, R, CD), F32)]
    return pl.pallas_call(
        functools.partial(_k1_ssd_sample_body, DI=DI, CD=CD),
        grid=(G, 1),
        in_specs=_k1_specs(x3, mod3, 0, 1, tm) + [_w_spec(w), _const_spec(wdt.shape),
                                                  _const_spec(cw.shape), _const_spec(cb.shape),
                                                  _const_spec(dtb.shape), _const_spec(buf_t.shape),
                                                  _const_spec(alog_row.shape), _const_spec(expand.shape)],
        out_specs=[_row_spec(tm, DI), _row_spec(tm, DI), _row_spec(tm, DI), _row_spec(tm, GN),
                   _row_spec(tm, GN), _row_spec(tm, LANES),
                   pl.BlockSpec((SSD_CONV - 1, R, CD), lambda g, t: (0, 0, 0))],
        out_shape=outs,
        compiler_params=_params(("arbitrary", "arbitrary")),
        name="ssd_inproj_step",
    )(x3, mod3, mod3, nw, w[0], wdt, cw, cb, dtb, buf_t, alog_row, expand)


def _k2_ssd_body(z_ref, xs_ref, b_ref, c_ref, dt_ref, tri_ref, alog_ref, exp_ref, drep_ref, nw_ref,
                 y_ref, so_ref, s_scr, *, C, NH, P, N, NG, nc, n_sub):
    ci = pl.program_id(1)

    @pl.when(ci == 0)
    def _():
        s_scr[...] = jnp.zeros_like(s_scr)

    for sub in range(n_sub):
        _ssd_chunk(slice(sub * C, (sub + 1) * C), z_ref, xs_ref, b_ref, c_ref, dt_ref, tri_ref, alog_ref,
                   exp_ref, drep_ref, nw_ref, y_ref, s_scr, C=C, NH=NH, P=P, N=N, NG=NG)

    @pl.when(ci == nc - 1)
    def _():
        for g in range(NG):
            so_ref[g] = s_scr[g].T


def _ssd_chunk(rows, z_ref, xs_ref, b_ref, c_ref, dt_ref, tri_ref, alog_ref, exp_ref, drep_ref, nw_ref,
               y_ref, s_scr, *, C, NH, P, N, NG):
    hpg = NH // NG
    gw = hpg * P
    dt = dt_ref[0, rows, :]
    la = dt * (-jnp.exp(alog_ref[...]))
    cum = _sel_dot(tri_ref[...], _split3(la))
    tot = cum[C - 1:C, :]
    c2 = cum * LOG2E
    r2_t = (c2 - jnp.log(dt) * LOG2E).T
    w_state_parts = _split3(jnp.exp(tot - cum) * dt)
    w_in_parts = _split3(jnp.exp(cum))
    row = lax.broadcasted_iota(jnp.int32, (C, C), 0)
    colm = lax.broadcasted_iota(jnp.int32, (C, C), 1)
    causal = row >= colm
    lane = lax.broadcasted_iota(jnp.int32, (1, 2 * P), 1)
    head_keep = ((lane < P).astype(BF16), (lane >= P).astype(BF16))
    for g in range(NG):
        gsl = slice(g * gw, (g + 1) * gw)
        bg = b_ref[0, rows, g * N:(g + 1) * N]
        cg = c_ref[0, rows, g * N:(g + 1) * N]
        expand = exp_ref[:, gsl]
        w_state = _dot_sel(w_state_parts, expand)
        w_in = _dot_sel(w_in_parts, expand)
        xs = xs_ref[0, rows, gsl]
        xsf = xs.astype(F32)
        xp = (xsf * w_state).astype(BF16)
        gmat = _dot_nt(cg, bg)
        state = s_scr[g]
        y_inter = _dot(cg, state.astype(BF16)) * w_in
        s_scr[g] = state * w_in[C - 1:C, :] + _dot_tn(bg, xp)
        ys = []
        for hp in range(hpg // 2):
            h0 = g * hpg + 2 * hp
            xpair = xs[:, 2 * hp * P:(2 * hp + 2) * P]
            acc = y_inter[:, 2 * hp * P:(2 * hp + 2) * P]
            for e in range(2):
                h = h0 + e
                seg2 = jnp.where(causal, c2[:, h:h + 1] - r2_t[h:h + 1, :], MASKED_EXPONENT)
                m = (gmat * jnp.exp2(seg2)).astype(BF16)
                acc = acc + _dot(m, xpair * head_keep[e])
            ys.append(acc)
        out = _ssd_gate(jnp.concatenate(ys, axis=-1), xsf, z_ref[0, rows, gsl].astype(F32), drep_ref[:, gsl],
                        nw_ref[:, gsl], 1)[0]
        y_ref[0, rows, gsl] = out.astype(y_ref.dtype)


def _ssd_expand(NH, P):
    e = np.zeros((LANES, NH * P), np.float32)
    for h in range(NH):
        e[h, h * P:(h + 1) * P] = 1.0
    return jnp.asarray(e, BF16)


def _k2_ssd(z, xs, bm, cm, dt, alog_row, expand, drep, nw, NH, P, N, NG, nl, j, prev):
    G, R, DI = xs.shape
    C = min(C_SSD, R)
    n_sub = SSD_CHUNKS_PER_STEP if R % (C * SSD_CHUNKS_PER_STEP) == 0 else 1
    nc = R // (C * n_sub)
    rows = C * n_sub
    i = np.arange(C)
    tri = jnp.asarray(i[:, None] >= i[None, :], BF16)
    gw = (NH // NG) * P
    return _call_into_stack(
        functools.partial(_k2_ssd_body, C=C, NH=NH, P=P, N=N, NG=NG, nc=nc, n_sub=n_sub),
        grid=(G, nc),
        inputs=[z, xs, bm, cm, dt, tri, alog_row, expand, drep, nw],
        in_specs=[_row_spec(rows, DI), _row_spec(rows, DI), _row_spec(rows, NG * N), _row_spec(rows, NG * N),
                  _row_spec(rows, LANES), _const_spec(tri.shape), _const_spec(alog_row.shape),
                  _const_spec(expand.shape), _const_spec(drep.shape), _const_spec(nw.shape)],
        out_specs=[_row_spec(rows, DI), _layer_state_spec((NG, gw, N), j)],
        out_shape=[jax.ShapeDtypeStruct((G, R, DI), BF16), jax.ShapeDtypeStruct((nl, G, NG, gw, N), F32)],
        scratch_shapes=[pltpu.VMEM((NG, N, gw), F32)],
        stack_out=1, prev=prev, name="ssd_core")


def _pad_rows(row, keep_first_only):
    x = jnp.broadcast_to(row, (MXU_PAD_ROWS, row.shape[-1]))
    if keep_first_only:
        first = lax.broadcasted_iota(jnp.int32, (MXU_PAD_ROWS, 1), 0) == 0
        x = jnp.where(first, x, 0.0)
    return x.astype(BF16)


def _dec_ssd_body(xdt_ref, b_ref, c_ref, dec_ref, s_ref, so_ref, o_ref, *, NH, P, N, NG, bt):
    hpg = NH // NG
    gw = hpg * P
    for i in range(bt):
        arow = _sample_row(dec_ref, i, bt)
        v_pad = _pad_rows(_sample_row(xdt_ref, i, bt), True)
        k_pad = _pad_rows(_sample_row(b_ref, i, bt), False)
        q_pad = _pad_rows(_sample_row(c_ref, i, bt), False)
        for g in range(NG):
            upd = _dot_tn(v_pad[:, g * gw:(g + 1) * gw], k_pad[:, g * N:(g + 1) * N])
            news = []
            for hl in range(hpg):
                h = g * hpg + hl
                rows = slice(hl * P, (hl + 1) * P)
                new = arow[:, h:h + 1] * s_ref[i, g, rows, :] + upd[rows, :]
                so_ref[i, g, rows, :] = new
                news.append(new.astype(BF16))
            o = _dot_nt(q_pad[:, g * N:(g + 1) * N], jnp.concatenate(news, axis=0))
            o_ref[i, :, g * gw:(g + 1) * gw] = o[0:1]


def _dec_ssd(xdt, bm, cm, dec, states_t, NH, P, N, NG, j, prev):
    B = xdt.shape[1]
    tail = states_t.shape[2:]
    bt = _step_batch(B, BT_SSD)
    return _call_into_stack(
        functools.partial(_dec_ssd_body, NH=NH, P=P, N=N, NG=NG, bt=bt),
        grid=(B // bt,),
        inputs=[xdt, bm, cm, dec, states_t],
        in_specs=[_rows_spec(xdt), _rows_spec(bm), _rows_spec(cm), _rows_spec(dec),
                  _layer_state_spec(tail, j, bt)],
        out_specs=[_layer_state_spec(tail, j, bt), _step_spec(NH * P, bt)],
        out_shape=[jax.ShapeDtypeStruct(states_t.shape, F32), jax.ShapeDtypeStruct((B, 1, NH * P), F32)],
        stack_out=0, prev=prev, name="ssd_step")


def _hg_lower_bound(lbl_ref, layer):
    logits = lbl_ref[...]
    e = jnp.exp(logits - jnp.max(logits, axis=0, keepdims=True))
    sm = e / jnp.sum(e, axis=0, keepdims=True)
    lb = jnp.zeros_like(sm[0:1])
    for i in range(1, layer + 1):
        lb = lb + sm[i:i + 1]
    return lb


def _k1_hg_body(x_ref, sh_ref, sc_ref, nw_ref, w_ref, lbl_ref, *out_refs, layer, KD, VD, prompt):
    hb = _norm_mod(x_ref, sh_ref, sc_ref, nw_ref)
    lb = _hg_lower_bound(lbl_ref, layer)
    w = 512
    if prompt:
        q_ref, la_ref, v_ref, g_ref = out_refs
    else:
        q_ref, k_ref, a_ref, v_ref, g_ref = out_refs
    for c in range(KD // w):
        cols = slice(c * w, (c + 1) * w)
        q = _silu(_dot(hb, w_ref[:, c * w:(c + 1) * w])) * (HG_KEY_DIM ** -0.5)
        q_ref[0, :, cols] = q.astype(q_ref.dtype)
        lbc = lb[:, cols]
        f = lbc + (1.0 - lbc) * _sigmoid(_dot(hb, w_ref[:, KD + c * w:KD + (c + 1) * w]))
        if prompt:
            la_ref[0, :, cols] = jnp.log(f)
        else:
            k_ref[0, :, cols] = 1.0 - f
            a_ref[0, :, cols] = f
    for c in range(VD // w):
        cols = slice(c * w, (c + 1) * w)
        v_ref[0, :, cols] = _dot(hb, w_ref[:, 2 * KD + c * w:2 * KD + (c + 1) * w]).astype(v_ref.dtype)
        g_ref[0, :, cols] = _dot(hb, w_ref[:, 2 * KD + VD + c * w:2 * KD + VD + (c + 1) * w]).astype(g_ref.dtype)


def _k1_hg(x3, mod3, nw, w, lbl, tm, layer, KD, VD, prompt):
    G, R, d = x3.shape
    if prompt:
        dts = [(KD, BF16), (KD, F32), (VD, BF16), (VD, BF16)]
    else:
        dts = [(KD, F32), (KD, F32), (KD, F32), (VD, F32), (VD, F32)]
    return pl.pallas_call(
        functools.partial(_k1_hg_body, layer=layer, KD=KD, VD=VD, prompt=prompt),
        grid=(G, R // tm),
        in_specs=_k1_specs(x3, mod3, 0, 1, tm) + [_w_spec(w), _const_spec(lbl.shape)],
        out_specs=[_row_spec(tm, n) for n, _ in dts],
        out_shape=[jax.ShapeDtypeStruct((G, R, n), dt) for n, dt in dts],
        compiler_params=_params(("arbitrary", "arbitrary")),
        name="hg_inproj" if prompt else "hg_inproj_step",
    )(x3, mod3, mod3, nw, w[0], lbl)


def _hg_masks(C):
    i = np.arange(C)
    r, t = i[:, None], i[None, :]
    sels = [t <= r, (t <= r) & (t >= HG_SUB * (r // HG_SUB))]
    pairs = []
    s = C // 2
    while s >= HG_SUB:
        mid = 2 * s * (r // (2 * s)) + s - 1
        sels.append(np.where(r > mid, (t > mid) & (t <= r), (t > r) & (t <= mid)))
        pairs.append((r // (2 * s) == t // (2 * s)) & (r % (2 * s) >= s) & (t % (2 * s) < s))
        s //= 2
    pairs.append((r // HG_SUB == t // HG_SUB) & (t <= r))
    return (jnp.asarray(np.stack(sels), BF16), jnp.asarray(np.stack(pairs), F32))


def _hg_diag_collect(C, K):
    e = np.zeros((HG_SUB, K, C), np.float32)
    for j in range(HG_SUB):
        e[j, :, j::HG_SUB] = 1.0
    return jnp.asarray(e.reshape(HG_SUB * K, C), BF16)


def _k2_hg_body(q_ref, la_ref, v_ref, g_ref, sel_ref, pair_ref, coll_ref, nw_ref, y_ref, so_ref,
                st_scr, k_scr, c_scr, p_scr, *, C, H, K, V, nlev, nc, n_sub):
    ci = pl.program_id(1)

    @pl.when(ci == 0)
    def _():
        st_scr[...] = jnp.zeros_like(st_scr)

    for sub in range(n_sub):
        _hg_chunk(slice(sub * C, (sub + 1) * C), q_ref, la_ref, v_ref, g_ref, sel_ref, pair_ref, coll_ref,
                  nw_ref, y_ref, st_scr, k_scr.at[sub], c_scr.at[sub], p_scr.at[sub],
                  C=C, H=H, K=K, V=V, nlev=nlev)

    @pl.when(ci == nc - 1)
    def _():
        for h in range(H):
            so_ref[h] = st_scr[h].T


def _hg_chunk(rows, q_ref, la_ref, v_ref, g_ref, sel_ref, pair_ref, coll_ref, nw_ref, y_ref,
              st_scr, k_scr, c_scr, p_scr, *, C, H, K, V, nlev):
    D = H * K
    la2 = la_ref[0, rows, :] * LOG2E
    parts = _split3(la2)
    cum2 = _sel_dot(sel_ref[0], parts)
    cw2 = _sel_dot(sel_ref[1], parts)
    kf = 1.0 - jnp.exp2(la2)
    qf = q_ref[0, rows, :].astype(F32)
    vb = v_ref[0, rows, :]
    tot2 = cum2[C - 1:C, :]
    q_in = (qf * jnp.exp2(cum2)).astype(BF16)
    k_out = (kf * jnp.exp2(tot2 - cum2)).astype(BF16)
    e_tot = jnp.exp2(tot2)

    scores = [jnp.zeros((C, C), F32) for _ in range(H)]
    for lev in range(nlev):
        a = jnp.exp2(_sel_dot(sel_ref[2 + lev], parts))
        ql = (qf * a).astype(BF16)
        kl = (kf * a).astype(BF16)
        pm = pair_ref[lev] > 0.5
        for h in range(H):
            sl = slice(h * K, (h + 1) * K)
            scores[h] = scores[h] + jnp.where(pm, _dot_nt(ql[:, sl], kl[:, sl]), 0.0)

    k_scr[...] = kf
    c_scr[...] = cw2
    rowmod = lax.broadcasted_iota(jnp.int32, (C, D), 0) & (HG_SUB - 1)
    nblk = C // HG_SUB
    for j in range(HG_SUB):
        kb = jnp.concatenate([jnp.broadcast_to(k_scr[b * HG_SUB + j:b * HG_SUB + j + 1, :], (HG_SUB, D))
                              for b in range(nblk)], axis=0)
        cb = jnp.concatenate([jnp.broadcast_to(c_scr[b * HG_SUB + j:b * HG_SUB + j + 1, :], (HG_SUB, D))
                              for b in range(nblk)], axis=0)
        dec = jnp.exp2(jnp.where(rowmod >= j, cw2 - cb, MASKED_EXPONENT))
        p = (qf * kb * dec).astype(BF16)
        for h in range(H):
            p_scr[h * C:(h + 1) * C, j * K:(j + 1) * K] = p[:, h * K:(h + 1) * K]
    diag = _dot(p_scr[...], coll_ref[...])
    dm = pair_ref[nlev] > 0.5

    nw = nw_ref[...]
    for h in range(H):
        sl = slice(h * K, (h + 1) * K)
        vsl = slice(h * V, (h + 1) * V)
        st = st_scr[h]
        sc = scores[h] + jnp.where(dm, diag[h * C:(h + 1) * C, :], 0.0)
        o = _dot(sc.astype(BF16), vb[:, vsl]) + _dot_nt(q_in[:, sl], st.astype(BF16))
        st_scr[h] = st * e_tot[:, sl] + _dot_tn(vb[:, vsl], k_out[:, sl])
        gh = g_ref[0, rows, vsl].astype(F32)
        y_ref[0, rows, vsl] = (_rms(o) * nw[:, vsl] * _silu(gh)).astype(y_ref.dtype)


def _k2_hg(q, la, v, g, nw_rep, H, K, V, nl, j, prev):
    G, R, D = q.shape
    C = min(C_HG, R)
    n_sub = HG_CHUNKS_PER_STEP if R % (C * HG_CHUNKS_PER_STEP) == 0 else 1
    nc = R // (C * n_sub)
    sels, pairs = _hg_masks(C)
    nlev = pairs.shape[0] - 1
    coll = _hg_diag_collect(C, K)
    return _call_into_stack(
        functools.partial(_k2_hg_body, C=C, H=H, K=K, V=V, nlev=nlev, nc=nc, n_sub=n_sub),
        grid=(G, nc),
        inputs=[q, la, v, g, sels, pairs, coll, nw_rep],
        in_specs=[_row_spec(C * n_sub, D), _row_spec(C * n_sub, D), _row_spec(C * n_sub, H * V),
                  _row_spec(C * n_sub, H * V), _const_spec(sels.shape), _const_spec(pairs.shape),
                  _const_spec(coll.shape), _const_spec(nw_rep.shape)],
        out_specs=[_row_spec(C * n_sub, H * V), _layer_state_spec((H, K, V), j)],
        out_shape=[jax.ShapeDtypeStruct((G, R, H * V), BF16), jax.ShapeDtypeStruct((nl, G, H, K, V), F32)],
        scratch_shapes=[pltpu.VMEM((H, V, K), F32), pltpu.VMEM((n_sub, C, D), F32),
                        pltpu.VMEM((n_sub, C, D), F32), pltpu.VMEM((n_sub, H * C, HG_SUB * K), BF16)],
        stack_out=1, prev=prev, name="hg_core")


def _dec_hg_body(q_ref, k_ref, a_ref, v_ref, s_ref, so_ref, o_ref, *, H, K, V, bt):
    eye = lax.broadcasted_iota(jnp.int32, (K, K), 0) == lax.broadcasted_iota(jnp.int32, (K, K), 1)
    for i in range(bt):
        arow = _sample_row(a_ref, i, bt)
        q_pad = _pad_rows(_sample_row(q_ref, i, bt), False)
        k_pad = _pad_rows(_sample_row(k_ref, i, bt), True)
        v_pad = _pad_rows(_sample_row(v_ref, i, bt), False)
        for h in range(H):
            sl = slice(h * K, (h + 1) * K)
            vsl = slice(h * V, (h + 1) * V)
            ac = _col(arow[:, sl], eye)
            new = ac * s_ref[i, h] + _dot_tn(k_pad[:, sl], v_pad[:, vsl])
            so_ref[i, h] = new
            o_ref[i, :, vsl] = _dot(q_pad[:, sl], new.astype(BF16))[0:1]


def _dec_hg(q, k, a, v, states, H, K, V, j, prev):
    B = q.shape[1]
    bt = _step_batch(B, BT_HG)
    return _call_into_stack(
        functools.partial(_dec_hg_body, H=H, K=K, V=V, bt=bt),
        grid=(B // bt,),
        inputs=[q, k, a, v, states],
        in_specs=[_rows_spec(q), _rows_spec(k), _rows_spec(a), _rows_spec(v),
                  _layer_state_spec((H, K, V), j, bt)],
        out_specs=[_layer_state_spec((H, K, V), j, bt), _step_spec(H * V, bt)],
        out_shape=[jax.ShapeDtypeStruct(states.shape, F32), jax.ShapeDtypeStruct((B, 1, H * V), F32)],
        stack_out=0, prev=prev, name="hg_step")


def _f32_tile(ref):
    return ref[0].astype(F32)


def _pro_identity(y_ref):
    return y_ref[0]


def _pro_ret(o_ref, g_ref, *, H, V):
    return jnp.concatenate(_ret_gate(_f32_tile(o_ref), _f32_tile(g_ref), H, V), axis=-1).astype(BF16)


def _pro_ssd(o_ref, xs_ref, z_ref, drep_ref, nw_ref, *, NG):
    return jnp.concatenate(_ssd_gate(_f32_tile(o_ref), _f32_tile(xs_ref), _f32_tile(z_ref), drep_ref[...],
                                     nw_ref[...], NG), axis=-1).astype(BF16)


def _pro_hg(o_ref, g_ref, nw_ref, *, H, V):
    return jnp.concatenate(_hg_gate(_f32_tile(o_ref), _f32_tile(g_ref), nw_ref[...], H, V),
                           axis=-1).astype(BF16)


def _swiglu(hb, win_ref, wout_ref, F, bounds):
    acc = None
    for lo, hi in bounds:
        gt = _dot(hb, win_ref[:, lo:hi])
        up = _dot(hb, win_ref[:, F + lo:F + hi])
        a = (_silu(gt) * up).astype(BF16)
        part = _dot(a, wout_ref[lo:hi, :])
        acc = part if acc is None else acc + part
    return acc


def _k34_body(*refs, prologue, n_tile, n_const, F, bounds):
    n_pro = n_tile + n_const
    (wo_ref, x_ref, gm_ref, nwm_ref, sh_ref, sc_ref, nwf_ref, gf_ref, nwp_ref,
     win_ref, wout_ref, o_ref) = refs[n_pro:]
    y = prologue(*refs[:n_pro])
    x = x_ref[0] + gm_ref[0] * (_rms(_dot(y, wo_ref[...])) * nwm_ref[...])
    hb = (_rms(x) * nwf_ref[...] * (1.0 + sc_ref[0]) + sh_ref[0]).astype(BF16)
    acc = _swiglu(hb, win_ref, wout_ref, F, bounds)
    o_ref[0] = x + gf_ref[0] * (_rms(acc) * nwp_ref[...])


def _k34(prologue, tile_args, const_args, w_out, x3, mod3, nw_mix_post, nw_pre, nw_post, win, wout, tm):
    G, R, d = x3.shape
    rm = mod3.shape[1]
    F = wout[0].shape[1]
    mod_spec = lambda idx: pl.BlockSpec((1, rm, d), lambda g, t: (g, 0, idx))
    row = _const_spec((1, d))
    return pl.pallas_call(
        functools.partial(_k34_body, prologue=prologue, n_tile=len(tile_args), n_const=len(const_args),
                          F=F, bounds=_ffn_bounds(F)),
        grid=(G, R // tm),
        in_specs=[_row_spec(tm, a.shape[-1]) for a in tile_args] + [_const_spec(c.shape) for c in const_args]
        + [_w_spec(w_out), _row_spec(tm, d), mod_spec(2), row, mod_spec(3), mod_spec(4), row,
           mod_spec(5), row, _w_spec(win), _w_spec(wout)],
        out_specs=_row_spec(tm, d),
        out_shape=jax.ShapeDtypeStruct((G, R, d), F32),
        compiler_params=_params(("arbitrary", "arbitrary")),
        name="outproj_ffn",
    )(*tile_args, *const_args, w_out[0], x3, mod3, nw_mix_post, mod3, mod3, nw_pre, mod3, nw_post, win[0],
      wout[0])


def _ffn_bounds(F):
    tiles, rem = divmod(F, MXU_TILE)
    assert rem == 0, F
    n = -(-tiles // FFN_MAX_CHUNK_TILES)
    sizes = [tiles // n + (1 if i < tiles % n else 0) for i in range(n)]
    edges = np.cumsum([0] + sizes) * MXU_TILE
    return tuple((int(a), int(b)) for a, b in zip(edges[:-1], edges[1:]))


def _rope_table(pos, half):
    inv = 1.0 / (ROPE_BASE ** (jnp.arange(half, dtype=F32) / half))
    ang = pos.astype(F32)[:, None] * inv[None, :]
    return jnp.cos(ang), jnp.sin(ang)


def kernel(x_prompt, x_sample, c_prompt, c_sample, state_ret, state_ssd, state_conv, state_hgrn, w_ada, b_ada, norm_mix_pre, norm_mix_post, norm_ffn_pre, norm_ffn_post, ret_w_in, ret_w_out, ssd_w_in, ssd_conv_w, ssd_conv_b, ssd_dt_bias, ssd_a_log, ssd_d, ssd_norm, ssd_w_out, hg_w_in, hg_lb_logits, hg_norm, hg_w_out, ffn_w_in, ffn_w_out):
    bp, seq, d = x_prompt.shape
    bs = x_sample.shape[0]
    depth = w_ada.shape[0]
    ret_h = d // 256
    ret_k = d // ret_h
    ret_v = 2 * ret_k
    di = 2 * d
    ssd_nh = di // SSD_HEAD_DIM
    gn = SSD_GROUPS * SSD_STATE
    hg_h = d // HG_KEY_DIM
    hg_v = d // hg_h
    kd = hg_h * HG_KEY_DIM
    vd = hg_h * hg_v
    tm = min(TM_PROMPT, seq)

    mod = _adaln(jnp.concatenate([c_prompt, c_sample], axis=0), w_ada, b_ada)
    xp = x_prompt.astype(F32)
    xs = x_sample.astype(F32).reshape(1, bs, d)
    cos_p, sin_p = _rope_table(jnp.arange(seq, dtype=jnp.int32), ret_k // 2)
    cos_s, sin_s = _rope_table(jnp.full((1,), PAST_LEN, jnp.int32), ret_k // 2)

    hpg = ssd_nh // SSD_GROUPS
    ssd_states_t = jnp.swapaxes(state_ssd.astype(F32), 3, 4).reshape(
        state_ssd.shape[0], bs, SSD_GROUPS, hpg * SSD_HEAD_DIM, SSD_STATE)

    def ssd_untranspose(s):
        s = s.reshape(s.shape[0], s.shape[1], ssd_nh, SSD_HEAD_DIM, SSD_STATE)
        return jnp.swapaxes(s, 3, 4).astype(state_ssd.dtype)

    wb = {name: w.astype(BF16) for name, w in (
        ("ret_in", ret_w_in), ("ret_out", ret_w_out), ("ssd_in", ssd_w_in), ("ssd_out", ssd_w_out),
        ("hg_in", hg_w_in), ("hg_out", hg_w_out), ("ffn_in", ffn_w_in), ("ffn_out", ffn_w_out))}

    new = {k: [] for k in ("conv_p", "conv_s")}
    stk = {k: None for k in ("ret_p", "ret_s", "ssd_p", "ssd_s", "hg_p", "hg_s")}
    n_kind = [sum(1 for l in range(depth) if l % 3 == kind) for kind in range(3)]
    counts = [0, 0, 0]
    for layer in range(depth):
        mod_p = mod[layer, :bp].reshape(bp, 1, 6 * d)
        mod_s = mod[layer, bp:].reshape(1, bs, 6 * d)
        nw_pre = norm_mix_pre[layer].reshape(1, d)
        nw_post = norm_mix_post[layer].reshape(1, d)
        kind = layer % 3
        j = counts[kind]
        counts[kind] += 1
        if kind == 0:
            w_in = (wb["ret_in"], j)
            w_out = (wb["ret_out"], j)
            prologue = functools.partial(_pro_ret, H=ret_h, V=ret_v)
            q, k, v, g = _k1_ret(xp, mod_p, nw_pre, w_in, cos_p, sin_p, tm, BF16, ret_h, ret_k, ret_v)
            y, stk["ret_p"] = _k2_ret(q, k, v, g, ret_h, ret_k, ret_v, n_kind[0], j, stk["ret_p"])
            q, k, v, g = _k1_ret(xs, mod_s, nw_pre, w_in, cos_s, sin_s, bs, F32, ret_h, ret_k, ret_v)
            stk["ret_s"], o = _dec_ret(q, k, v, state_ret, ret_h, ret_k, ret_v, j, stk["ret_s"])
            pro_s = ([o.reshape(1, bs, -1), g], [])
        elif kind == 1:
            w_main = (wb["ssd_in"], j)
            w_dt = jnp.pad(ssd_w_in[j][:, di + di + 2 * gn:], ((0, 0), (0, LANES - ssd_nh))).astype(BF16)
            w_out = (wb["ssd_out"], j)
            cw = ssd_conv_w[j]
            cb = ssd_conv_b[j].reshape(1, -1)
            dtb = jnp.pad(ssd_dt_bias[j], (0, LANES - ssd_nh)).reshape(1, LANES)
            alog = jnp.pad(ssd_a_log[j].astype(F32), (0, LANES - ssd_nh)).reshape(1, LANES)
            drep = jnp.repeat(ssd_d[j], SSD_HEAD_DIM).reshape(1, di)
            nw_ssd = ssd_norm[j].reshape(1, di)
            z, xc, bm, cm, dt, nconv = _k1_ssd_prompt(xp, mod_p, nw_pre, w_main, w_dt, cw, cb, dtb, tm, di, gn)
            new["conv_p"].append(nconv)
            expand = _ssd_expand(ssd_nh, SSD_HEAD_DIM)
            prologue = functools.partial(_pro_ssd, NG=SSD_GROUPS)
            y, stk["ssd_p"] = _k2_ssd(z, xc, bm, cm, dt, alog, expand, drep, nw_ssd, ssd_nh, SSD_HEAD_DIM,
                                      SSD_STATE, SSD_GROUPS, n_kind[1], j, stk["ssd_p"])
            buf_t = jnp.transpose(state_conv[j].astype(F32), (1, 0, 2))
            z, xc, xdt, bm, cm, dec, nconv = _k1_ssd_sample(xs, mod_s, nw_pre, w_main, w_dt, cw, cb, dtb, buf_t,
                                                            alog, expand, di, gn)
            new["conv_s"].append(jnp.transpose(nconv, (1, 0, 2)))
            stk["ssd_s"], o = _dec_ssd(xdt, bm, cm, dec, ssd_states_t,
                                       ssd_nh, SSD_HEAD_DIM, SSD_STATE, SSD_GROUPS, j, stk["ssd_s"])
            pro_s = ([o.reshape(1, bs, -1), xc, z], [drep, nw_ssd])
        else:
            w_in = (wb["hg_in"], j)
            w_out = (wb["hg_out"], j)
            nw_hg = jnp.tile(hg_norm[j], hg_h).reshape(1, vd)
            lbl = hg_lb_logits.astype(F32)
            prologue = functools.partial(_pro_hg, H=hg_h, V=hg_v)
            q, la, v, g = _k1_hg(xp, mod_p, nw_pre, w_in, lbl, tm, layer, kd, vd, True)
            y, stk["hg_p"] = _k2_hg(q, la, v, g, nw_hg, hg_h, HG_KEY_DIM, hg_v, n_kind[2], j, stk["hg_p"])
            q, k, a, v, g = _k1_hg(xs, mod_s, nw_pre, w_in, lbl, bs, layer, kd, vd, False)
            stk["hg_s"], o = _dec_hg(q, k, a, v, state_hgrn, hg_h, HG_KEY_DIM, hg_v, j, stk["hg_s"])
            pro_s = ([o.reshape(1, bs, -1), g], [nw_hg])
        nf_pre = norm_ffn_pre[layer].reshape(1, d)
        nf_post = norm_ffn_post[layer].reshape(1, d)
        f_in = (wb["ffn_in"], layer)
        f_out = (wb["ffn_out"], layer)
        xp = _k34(_pro_identity, [y], [], w_out, xp, mod_p, nw_post, nf_pre, nf_post, f_in, f_out, tm)
        xs = _k34(prologue, *pro_s, w_out, xs, mod_s, nw_post, nf_pre, nf_post, f_in, f_out, bs)

    stack = lambda name, like: jnp.stack(new[name]).astype(like.dtype)
    return (xp.astype(x_prompt.dtype), xs.reshape(bs, 1, d).astype(x_sample.dtype),
            stk["ret_p"].astype(state_ret.dtype), stk["ret_s"].astype(state_ret.dtype),
            ssd_untranspose(stk["ssd_p"]), ssd_untranspose(stk["ssd_s"]),
            stack("conv_p", state_conv), stack("conv_s", state_conv),
            stk["hg_p"].astype(state_hgrn.dtype), stk["hg_s"].astype(state_hgrn.dtype))
```
